```python
import math
import jax, jax.numpy as jnp
from jax import lax
import numpy as np

D_MODEL = 1024
BATCH = 16
SEQ = 4096
DEPTH = 1

ATTN_HEADS = 8
ATTN_HEAD_DIM = 64
ATTN_W = ATTN_HEADS * ATTN_HEAD_DIM
ROT_DIM = ATTN_HEAD_DIM // 4
ROPE_THETA = 500000.0
MOBA_BLOCK = 256
MOBA_TOPK = 3
MOBA_Q_CHUNK = 32
MLSTM_HEADS = 4
MLSTM_HEAD_DIM = 128
MLSTM_W = MLSTM_HEADS * MLSTM_HEAD_DIM
MLSTM_CHUNK = 64
CONV_WIDTH = 4
PEER_HEADS = 8
PEER_N_KEYS = 128
PEER_N_EXPERTS = PEER_N_KEYS * PEER_N_KEYS
PEER_KEY_DIM = 256
PEER_HALF = PEER_KEY_DIM // 2
PEER_TOPK = 16
PEER_TOKEN_CHUNK = 128
EPS = 1e-6

IN_SPLITS = (ATTN_W, ATTN_W, ATTN_W, MLSTM_W, MLSTM_W, MLSTM_W, MLSTM_W,
             MLSTM_HEADS, MLSTM_HEADS, D_MODEL, D_MODEL)
IN_COLS = sum(IN_SPLITS)

kernel_name = "hybrid_moba_mlstm_peer_block"


def rms_norm(x, w):
    xf = x.astype(jnp.float32)
    xf = xf * lax.rsqrt(jnp.mean(xf * xf, axis=-1, keepdims=True) + EPS)
    return (xf * w.astype(jnp.float32)).astype(x.dtype)


def apply_partial_rope(t):
    S = t.shape[2]
    pos = jnp.arange(S, dtype=jnp.float32)
    inv_freq = ROPE_THETA ** (-jnp.arange(0, ROT_DIM, 2, dtype=jnp.float32) / ROT_DIM)
    ang = pos[:, None] * inv_freq[None, :]
    cos, sin = jnp.cos(ang), jnp.sin(ang)
    tf = t.astype(jnp.float32)
    half = ROT_DIM // 2
    x1, x2, rest = tf[..., :half], tf[..., half:ROT_DIM], tf[..., ROT_DIM:]
    out = jnp.concatenate([x1 * cos - x2 * sin, x2 * cos + x1 * sin, rest], axis=-1)
    return out.astype(t.dtype)


def causal_depthwise_conv(u, w, b):
    S = u.shape[1]
    up = jnp.pad(u, ((0, 0), (CONV_WIDTH - 1, 0), (0, 0)))
    y = sum(up[:, j:j + S] * w[j] for j in range(CONV_WIDTH))
    return y + b


def moba_attention(q, k, v):
    B, H, S, hd = q.shape
    n_blocks = -(-S // MOBA_BLOCK)
    s_pad = n_blocks * MOBA_BLOCK
    pad = ((0, 0), (0, 0), (0, s_pad - S), (0, 0))
    q, k, v = jnp.pad(q, pad), jnp.pad(k, pad), jnp.pad(v, pad)
    k_blocks = k.reshape(B, H, n_blocks, MOBA_BLOCK, hd)
    v_blocks = v.reshape(B, H, n_blocks, MOBA_BLOCK, hd)
    k_mean = jnp.mean(k_blocks.astype(jnp.float32), axis=3)
    topk = min(MOBA_TOPK, n_blocks)
    scale = hd ** -0.5
    b_idx = jnp.arange(B)[:, None, None, None]
    h_idx = jnp.arange(H)[None, :, None, None]
    block_ids = jnp.arange(n_blocks)

    def one_chunk(ci):
        start = ci * MOBA_Q_CHUNK
        blk = start // MOBA_BLOCK
        qc = lax.dynamic_slice_in_dim(q, start, MOBA_Q_CHUNK, axis=2).astype(jnp.float32)
        k_own = lax.dynamic_slice_in_dim(k, blk * MOBA_BLOCK, MOBA_BLOCK, axis=2).astype(jnp.float32)
        v_own = lax.dynamic_slice_in_dim(v, blk * MOBA_BLOCK, MOBA_BLOCK, axis=2).astype(jnp.float32)
        q_pos = start + jnp.arange(MOBA_Q_CHUNK)
        k_pos = blk * MOBA_BLOCK + jnp.arange(MOBA_BLOCK)
        s_own = jnp.einsum('bhqd,bhkd->bhqk', qc, k_own) * scale
        s_own = jnp.where(k_pos[None, :] <= q_pos[:, None], s_own, -jnp.inf)
        gate = jnp.einsum('bhqd,bhnd->bhqn', qc, k_mean)
        gate = jnp.where(block_ids < blk, gate, -jnp.inf)
        _, sel = lax.top_k(gate, topk)
        valid = sel < blk
        k_sel = k_blocks[b_idx, h_idx, sel].astype(jnp.float32)
        v_sel = v_blocks[b_idx, h_idx, sel].astype(jnp.float32)
        s_sel = jnp.einsum('bhqd,bhqjkd->bhqjk', qc, k_sel) * scale
        s_sel = jnp.where(valid[..., None], s_sel, -jnp.inf)
        s_sel = s_sel.reshape(B, H, MOBA_Q_CHUNK, topk * MOBA_BLOCK)
        p = jax.nn.softmax(jnp.concatenate([s_sel, s_own], axis=-1), axis=-1)
        p_sel = p[..., :topk * MOBA_BLOCK].reshape(B, H, MOBA_Q_CHUNK, topk, MOBA_BLOCK)
        p_own = p[..., topk * MOBA_BLOCK:]
        out = (jnp.einsum('bhqjk,bhqjkd->bhqd', p_sel, v_sel)
               + jnp.einsum('bhqk,bhkd->bhqd', p_own, v_own))
        return out.astype(q.dtype)

    n_chunks = s_pad // MOBA_Q_CHUNK
    out = lax.map(one_chunk, jnp.arange(n_chunks))
    out = jnp.moveaxis(out, 0, 2).reshape(B, H, s_pad, hd)[:, :, :S]
    return out


def mlstm_chunkwise(q, k, v, i_pre, f_pre):
    B, H, S, d = q.shape
    L = MLSTM_CHUNK
    nc = S // L
    log_f = jax.nn.log_sigmoid(f_pre)

    def to_chunks(t):
        return jnp.moveaxis(t.reshape((B, H, nc, L) + t.shape[3:]), 2, 0)

    causal = jnp.tril(jnp.ones((L, L), dtype=bool))

    def step(carry, inp):
        C, n, m = carry
        qc, kc, vc, ig, lf = inp
        b = jnp.cumsum(lf, axis=-1)
        log_d = b[..., :, None] - b[..., None, :] + ig[..., None, :]
        log_d = jnp.where(causal, log_d, -jnp.inf)
        m_inter = b + m[..., None]
        m_t = jnp.maximum(m_inter, jnp.max(log_d, axis=-1))
        A = jnp.exp(log_d - m_t[..., None]) * jnp.einsum('bhtd,bhsd->bhts', qc, kc)
        w_inter = jnp.exp(m_inter - m_t)
        num = (jnp.einsum('bhts,bhsv->bhtv', A, vc)
               + w_inter[..., None] * jnp.einsum('bhtd,bhdv->bhtv', qc, C))
        den = jnp.sum(A, axis=-1) + w_inter * jnp.einsum('bhtd,bhd->bht', qc, n)
        h = num / jnp.maximum(jnp.abs(den), jnp.exp(-m_t))[..., None]
        m_new = m_t[..., -1]
        w_s = jnp.exp(b[..., -1:] - b + ig - m_new[..., None])
        decay = jnp.exp(b[..., -1] + m - m_new)
        C_new = decay[..., None, None] * C + jnp.einsum('bhs,bhsd,bhsv->bhdv', w_s, kc, vc)
        n_new = decay[..., None] * n + jnp.einsum('bhs,bhsd->bhd', w_s, kc)
        return (C_new, n_new, m_new), h

    init = (jnp.zeros((B, H, d, d), jnp.float32),
            jnp.zeros((B, H, d), jnp.float32),
            jnp.zeros((B, H), jnp.float32))
    _, h = lax.scan(step, init, (to_chunks(q), to_chunks(k), to_chunks(v),
                                 to_chunks(i_pre), to_chunks(log_f)))
    return jnp.moveaxis(h, 0, 2).reshape(B, H, S, d)


def peer_ffn(h, w_query, keys_1, keys_2, w_down, w_up):
    B, S, D = h.shape
    T = B * S
    hf = h.reshape(T, D)
    q = (hf @ w_query).reshape(T, PEER_HEADS, PEER_KEY_DIM)
    q1, q2 = q[..., :PEER_HALF], q[..., PEER_HALF:]
    s1 = jnp.einsum('thd,hnd->thn', q1, keys_1).astype(jnp.float32)
    s2 = jnp.einsum('thd,hnd->thn', q2, keys_2).astype(jnp.float32)
    v1, i1 = lax.top_k(s1, PEER_TOPK)
    v2, i2 = lax.top_k(s2, PEER_TOPK)
    cand = (v1[..., :, None] + v2[..., None, :]).reshape(T, PEER_HEADS, PEER_TOPK * PEER_TOPK)
    sc, ci = lax.top_k(cand, PEER_TOPK)
    e1 = jnp.take_along_axis(i1, ci // PEER_TOPK, axis=-1)
    e2 = jnp.take_along_axis(i2, ci % PEER_TOPK, axis=-1)
    expert = e1 * PEER_N_KEYS + e2
    g = jax.nn.softmax(sc, axis=-1).astype(h.dtype)

    def one_chunk(args):
        xc, ec, gc = args
        u = w_down[ec]
        a = jax.nn.gelu(jnp.einsum('cd,chkd->chk', xc, u), approximate=False)
        vv = w_up[ec]
        return jnp.einsum('chk,chkd->cd', gc * a, vv)

    nc = T // PEER_TOKEN_CHUNK
    out = lax.map(one_chunk, (hf.reshape(nc, PEER_TOKEN_CHUNK, D),
                              expert.reshape(nc, PEER_TOKEN_CHUNK, PEER_HEADS, PEER_TOPK),
                              g.reshape(nc, PEER_TOKEN_CHUNK, PEER_HEADS, PEER_TOPK)))
    return out.reshape(B, S, D)


def setup_inputs(seed: int = 0) -> dict:
    key = jax.random.key(seed)
    ks = jax.random.split(key, 24)
    D = D_MODEL
    nrm = lambda k, shape, s: jax.random.normal(k, shape, jnp.float32) * s
    gain = lambda k, shape: 1.0 + 0.02 * jax.random.normal(k, shape, jnp.float32)
    return {
        "x": nrm(ks[0], (BATCH, SEQ, D), 1.0),
        "mix_norm_w": gain(ks[1], (D,)),
        "w_in": nrm(ks[2], (D, IN_COLS), D ** -0.5),
        "q_norm_w": gain(ks[3], (ATTN_HEAD_DIM,)),
        "k_norm_w": gain(ks[4], (ATTN_HEAD_DIM,)),
        "conv_w": nrm(ks[5], (CONV_WIDTH, 2 * MLSTM_W), CONV_WIDTH ** -0.5),
        "conv_b": nrm(ks[6], (2 * MLSTM_W,), 0.01),
        "igate_b": nrm(ks[7], (MLSTM_HEADS,), 0.1),
        "fgate_b": 3.0 + jax.random.uniform(ks[8], (MLSTM_HEADS,), jnp.float32, 0.0, 3.0),
        "mlstm_norm_w": gain(ks[9], (MLSTM_W,)),
        "w_attn_branch": nrm(ks[10], (ATTN_W, D), ATTN_W ** -0.5),
        "w_mlstm_branch": nrm(ks[11], (MLSTM_W, D), MLSTM_W ** -0.5),
        "merge_b": nrm(ks[12], (2, D), 0.01),
        "w_out": nrm(ks[13], (D, D), D ** -0.5),
        "ffn_norm_w": gain(ks[14], (D,)),
        "peer_w_query": nrm(ks[15], (D, PEER_HEADS * PEER_KEY_DIM), D ** -0.5),
        "peer_keys_1": nrm(ks[16], (PEER_HEADS, PEER_N_KEYS, PEER_HALF), PEER_HALF ** -0.5),
        "peer_keys_2": nrm(ks[17], (PEER_HEADS, PEER_N_KEYS, PEER_HALF), PEER_HALF ** -0.5),
        "peer_w_down": nrm(ks[18], (PEER_N_EXPERTS, D), D ** -0.5),
        "peer_w_up": nrm(ks[19], (PEER_N_EXPERTS, D), 0.5),
    }


def reference(x, mix_norm_w, w_in, q_norm_w, k_norm_w, conv_w, conv_b, igate_b, fgate_b,
              mlstm_norm_w, w_attn_branch, w_mlstm_branch, merge_b, w_out, ffn_norm_w,
              peer_w_query, peer_keys_1, peer_keys_2, peer_w_down, peer_w_up):
    B, S, _ = x.shape
    offsets = [int(o) for o in np.cumsum(IN_SPLITS)[:-1]]

    def to_heads(t, n, d):
        return t.reshape(B, S, n, d).transpose(0, 2, 1, 3)

    for _layer in range(DEPTH):
        h = rms_norm(x, mix_norm_w)
        proj = h @ w_in
        aq, ak, av, mq, mk, mv, mo, mi, mf, ga, gm = jnp.split(proj, offsets, axis=-1)

        qa = apply_partial_rope(rms_norm(to_heads(aq, ATTN_HEADS, ATTN_HEAD_DIM), q_norm_w))
        ka = apply_partial_rope(rms_norm(to_heads(ak, ATTN_HEADS, ATTN_HEAD_DIM), k_norm_w))
        va = to_heads(av, ATTN_HEADS, ATTN_HEAD_DIM)
        attn = moba_attention(qa, ka, va)
        attn = attn.transpose(0, 2, 1, 3).reshape(B, S, ATTN_W)

        qk = jax.nn.silu(causal_depthwise_conv(jnp.concatenate([mq, mk], axis=-1), conv_w, conv_b))
        mq_c, mk_c = qk[..., :MLSTM_W], qk[..., MLSTM_W:]
        qm = to_heads(mq_c, MLSTM_HEADS, MLSTM_HEAD_DIM).astype(jnp.float32)
        km = to_heads(mk_c, MLSTM_HEADS, MLSTM_HEAD_DIM).astype(jnp.float32) * (MLSTM_HEAD_DIM ** -0.5)
        vm = to_heads(mv, MLSTM_HEADS, MLSTM_HEAD_DIM).astype(jnp.float32)
        i_pre = (mi + igate_b).astype(jnp.float32).transpose(0, 2, 1)
        f_pre = (mf + fgate_b).astype(jnp.float32).transpose(0, 2, 1)
        hm = mlstm_chunkwise(qm, km, vm, i_pre, f_pre)
        hm = hm.transpose(0, 2, 1, 3).astype(x.dtype)
        hm = rms_norm(hm, mlstm_norm_w.reshape(MLSTM_HEADS, MLSTM_HEAD_DIM)).reshape(B, S, MLSTM_W)
        hm = jax.nn.sigmoid(mo) * hm

        y = (jax.nn.sigmoid(ga + merge_b[0]) * (attn @ w_attn_branch)
             + jax.nn.sigmoid(gm + merge_b[1]) * (hm @ w_mlstm_branch))
        x = x + y @ w_out

        x = x + peer_ffn(rms_norm(x, ffn_norm_w), peer_w_query, peer_keys_1, peer_keys_2,
                         peer_w_down, peer_w_up)
    return x
```

```python
import functools

import jax
import jax.numpy as jnp
from jax import lax
from jax.experimental import pallas as pl
from jax.experimental.pallas import tpu as pltpu

F32 = jnp.float32
BF16 = jnp.bfloat16

EPS = 1e-6
ATTN_HEADS = 8
ATTN_HEAD_DIM = 64
ATTN_W = ATTN_HEADS * ATTN_HEAD_DIM
ROT_DIM = ATTN_HEAD_DIM // 4
ROPE_THETA = 500000.0
MOBA_BLOCK = 256
MOBA_TOPK = 3
MLSTM_HEADS = 4
MLSTM_HEAD_DIM = 128
MLSTM_W = MLSTM_HEADS * MLSTM_HEAD_DIM
CONV_WIDTH = 4
PEER_HEADS = 8
PEER_N_KEYS = 128
PEER_HALF = 128
PEER_TOPK = 16

LANES = 128
NEG = -1e30
VMEM_LIMIT = 56 * 1024 * 1024

INPROJ_TM = 512
MERGE_TM = 512
MLSTM_CHUNK = 256
SELECT_TN = 256
EXPERT_TN = 512
EXPERT_CHUNK = 256


def _dot(a, b):
    return jnp.dot(a, b, preferred_element_type=F32)


def _dot_nt(a, b):
    return lax.dot_general(a, b, (((1,), (1,)), ((), ())), preferred_element_type=F32)


def _dot_tn(a, b):
    return lax.dot_general(a, b, (((0,), (0,)), ((), ())), preferred_element_type=F32)


def _split2(x):
    hi = x.astype(BF16)
    lo = (x - hi.astype(F32)).astype(BF16)
    return hi, lo


def _params(sem):
    return pltpu.CompilerParams(dimension_semantics=sem, vmem_limit_bytes=VMEM_LIMIT)


def _inproj_kernel(x_ref, nw_ref, w_ref, cos_ref, sin_ref, bd_ref, qnw_ref, knw_ref,
                   aq_ref, ak_ref, av_ref, mq_ref, mk_ref, mv_ref, mo_ref, ga_ref, gm_ref, gt_ref):
    x = x_ref[...]
    h = x * lax.rsqrt(jnp.mean(x * x, axis=-1, keepdims=True) + EPS) * nw_ref[...]
    hb = h.astype(BF16)

    def proj(lo, width):
        return _dot(hb, w_ref[:, lo:lo + width])

    lane = lax.broadcasted_iota(jnp.int32, (x.shape[0], ATTN_W), 1) % ATTN_HEAD_DIM
    half = ROT_DIM // 2

    def qk_norm_rope(t, nw):
        hi, lo = _split2(t * t)
        ms = (_dot(hi, bd_ref[...]) + _dot(lo, bd_ref[...])) * (1.0 / ATTN_HEAD_DIM)
        tn = t * lax.rsqrt(ms + EPS) * nw
        fwd = pltpu.roll(tn, ATTN_W - half, axis=1)
        bwd = pltpu.roll(tn, half, axis=1)
        swapped = jnp.where(lane < half, fwd, jnp.where(lane < ROT_DIM, bwd, 0.0))
        return tn * cos_ref[...] + swapped * sin_ref[...]

    aq_ref[...] = qk_norm_rope(proj(0, ATTN_W), qnw_ref[...]).astype(BF16)
    ak_ref[...] = qk_norm_rope(proj(ATTN_W, ATTN_W), knw_ref[...]).astype(BF16)
    av_ref[...] = proj(2 * ATTN_W, ATTN_W).astype(BF16)
    base = 3 * ATTN_W
    mq_ref[...] = proj(base, MLSTM_W).astype(BF16)
    mk_ref[...] = proj(base + MLSTM_W, MLSTM_W).astype(BF16)
    mv_ref[...] = proj(base + 2 * MLSTM_W, MLSTM_W).astype(BF16)
    mo_ref[...] = proj(base + 3 * MLSTM_W, MLSTM_W).astype(BF16)
    base = base + 4 * MLSTM_W
    d = x.shape[1]
    ga_ref[...] = proj(base, d).astype(BF16)
    gm_ref[...] = proj(base + d, d).astype(BF16)
    gt_ref[...] = proj(base + 2 * d, LANES)


def _inproj(x2d, mix_norm_w, w_cat, cos_t, sin_t, bd, qnw, knw, seq):
    t, d = x2d.shape
    tm = INPROJ_TM
    assert t % tm == 0 and seq % tm == 0
    tiles_per_seq = seq // tm
    ncols = w_cat.shape[1]
    row = lambda w: pl.BlockSpec((tm, w), lambda i: (i, 0))
    const = lambda shape: pl.BlockSpec(shape, lambda i: (0, 0))
    pos = pl.BlockSpec((tm, ATTN_W), lambda i: (i % tiles_per_seq, 0))
    outs = [ATTN_W] * 3 + [MLSTM_W] * 4 + [d, d]
    return pl.pallas_call(
        _inproj_kernel,
        grid=(t // tm,),
        in_specs=[row(d), const((1, d)), const((d, ncols)), pos, pos, const((ATTN_W, ATTN_W)),
                  const((1, ATTN_W)), const((1, ATTN_W))],
        out_specs=[row(w) for w in outs] + [row(LANES)],
        out_shape=[jax.ShapeDtypeStruct((t, w), BF16) for w in outs]
        + [jax.ShapeDtypeStruct((t, LANES), F32)],
        compiler_params=_params(("parallel",)),
        name="inproj",
    )(x2d, mix_norm_w, w_cat, cos_t, sin_t, bd, qnw, knw)


def _moba_kernel(q_ref, k_ref, v_ref, o_ref, kmean_ref, *, n_blocks):
    i = pl.program_id(2)
    blk = MOBA_BLOCK
    hd = ATTN_HEAD_DIM

    @pl.when(i == 0)
    def _():
        kmean_ref[...] = jnp.zeros_like(kmean_ref)
        for n in range(n_blocks):
            kb = k_ref[n * blk:(n + 1) * blk, :].astype(F32)
            kmean_ref[n:n + 1, :] = jnp.mean(kb, axis=0, keepdims=True)

    q = q_ref[...]
    lane = lax.broadcasted_iota(jnp.int32, (blk, LANES), 1)
    kmean_hi, kmean_lo = _split2(kmean_ref[...])
    own = pl.multiple_of(i * blk, blk)
    k_own = k_ref[pl.ds(own, blk), :]
    v_own = v_ref[pl.ds(own, blk), :]
    qpos = lax.broadcasted_iota(jnp.int32, (blk, blk), 0)
    kpos = lax.broadcasted_iota(jnp.int32, (blk, blk), 1)
    brow = lax.broadcasted_iota(jnp.int32, (16, blk), 0)

    def one_head(hh):
        in_head = (lane >= hh * hd) & (lane < (hh + 1) * hd)
        qh = jnp.where(in_head, q, jnp.zeros_like(q)) * jnp.asarray(hd ** -0.5, BF16)

        g_t = (_dot_nt(kmean_hi, qh) + _dot_nt(kmean_lo, qh))[0:16, :]
        rank = jnp.zeros((16, blk), F32)
        for m in range(min(n_blocks, 16)):
            gm = g_t[m:m + 1, :]
            beats = jnp.where(gm > g_t, 1.0, jnp.where(gm == g_t, jnp.where(brow > m, 1.0, 0.0), 0.0))
            rank = rank + jnp.where(m < i, beats, 0.0)
        sel_t = jnp.where(brow < i, jnp.where(rank < MOBA_TOPK, 1.0, 0.0), 0.0)
        sel = jnp.concatenate([sel_t, jnp.zeros((LANES - 16, blk), F32)], axis=0).T

        s = _dot_nt(qh, k_own)
        s = jnp.where(kpos <= qpos, s, NEG)
        m0 = jnp.max(s, axis=1, keepdims=True)
        p = jnp.exp(s - m0)
        l0 = jnp.sum(p, axis=1, keepdims=True)
        acc0 = _dot(p.astype(BF16), v_own)

        def body(n, carry):
            m_run, l_run, acc = carry
            start = pl.multiple_of(n * blk, blk)
            k_n = k_ref[pl.ds(start, blk), :]
            v_n = v_ref[pl.ds(start, blk), :]
            col = jnp.sum(jnp.where(lane == n, sel, 0.0), axis=1, keepdims=True)
            s_n = jnp.where(col > 0.5, _dot_nt(qh, k_n), NEG)
            m_new = jnp.maximum(m_run, jnp.max(s_n, axis=1, keepdims=True))
            alpha = jnp.exp(m_run - m_new)
            p_n = jnp.exp(s_n - m_new)
            l_new = alpha * l_run + jnp.sum(p_n, axis=1, keepdims=True)
            acc_new = alpha * acc + _dot(p_n.astype(BF16), v_n)
            return m_new, l_new, acc_new

        _, l_fin, acc_fin = lax.fori_loop(0, i, body, (m0, l0, acc0))
        return acc_fin / l_fin

    out = jnp.where(lane < hd, one_head(0), one_head(1))
    o_ref[...] = out.astype(o_ref.dtype)


def _moba(q, k, v):
    b, s, w = q.shape
    assert s % MOBA_BLOCK == 0 and w % LANES == 0
    n_blocks = s // MOBA_BLOCK
    assert n_blocks <= 16
    qspec = pl.BlockSpec((None, MOBA_BLOCK, LANES), lambda bi, hp, i: (bi, i, hp))
    kvspec = pl.BlockSpec((None, s, LANES), lambda bi, hp, i: (bi, 0, hp))
    return pl.pallas_call(
        functools.partial(_moba_kernel, n_blocks=n_blocks),
        grid=(b, w // LANES, n_blocks),
        in_specs=[qspec, kvspec, kvspec],
        out_specs=qspec,
        out_shape=jax.ShapeDtypeStruct((b, s, w), BF16),
        scratch_shapes=[pltpu.VMEM((LANES, LANES), F32)],
        compiler_params=_params(("parallel", "parallel", "arbitrary")),
        name="moba",
    )(q, k, v)


def _log_sigmoid(f):
    return jnp.minimum(f, 0.0) - jnp.log(1.0 + jnp.exp(-jnp.abs(f)))


def _mlstm_kernel(gb_ref, mq_ref, mk_ref, mv_ref, mo_ref, gc_ref, gr_ref, cwq_ref, cwk_ref,
                  cbq_ref, cbk_ref, nw_ref, o_ref, c_ref, n_ref, m_ref):
    hh = pl.program_id(1)
    c = pl.program_id(2)
    L = MLSTM_CHUNK
    d = MLSTM_HEAD_DIM

    @pl.when(c == 0)
    def _():
        c_ref[...] = jnp.zeros_like(c_ref)
        n_ref[...] = jnp.zeros_like(n_ref)
        m_ref[...] = jnp.zeros_like(m_ref)

    start = pl.multiple_of(c * L, L)
    prev = pl.multiple_of(jnp.maximum(c - 1, 0) * L, L)
    row_d = lax.broadcasted_iota(jnp.int32, (L, d), 0)

    def conv_silu(u_ref, w_ref, b_ref):
        u = u_ref[pl.ds(start, L), :].astype(F32)
        pv = u_ref[pl.ds(prev, L), :].astype(F32)
        pv = jnp.where(c > 0, pv, 0.0)
        y = u * w_ref[CONV_WIDTH - 1:CONV_WIDTH, :] + b_ref[...]
        for k in range(1, CONV_WIDTH):
            shifted = jnp.where(row_d < k, pltpu.roll(pv, k, axis=0), pltpu.roll(u, k, axis=0))
            y = y + shifted * w_ref[CONV_WIDTH - 1 - k:CONV_WIDTH - k, :]
        return y * jax.nn.sigmoid(y)

    qc = conv_silu(mq_ref, cwq_ref, cbq_ref)
    kc = conv_silu(mk_ref, cwk_ref, cbk_ref) * (d ** -0.5)
    qb = qc.astype(BF16)
    kb = kc.astype(BF16)
    vb = mv_ref[...]

    ib = gb_ref[0, hh]
    fb = gb_ref[1, hh]
    gcol = gc_ref[...]
    grow = gr_ref[...]
    ig_col = gcol[:, 0:1] + ib
    ig_row = grow[0:1, :] + ib
    lf_col = _log_sigmoid(gcol[:, 1:2] + fb)
    lf_row = _log_sigmoid(grow[1:2, :] + fb)

    r = lax.broadcasted_iota(jnp.int32, (L, L), 0)
    s = lax.broadcasted_iota(jnp.int32, (L, L), 1)
    causal = s <= r
    tri_l = jnp.where(causal, 1.0, 0.0).astype(BF16)
    tri_u = jnp.where(r <= s, 1.0, 0.0).astype(BF16)
    lfc_hi, lfc_lo = _split2(jnp.broadcast_to(lf_col, (L, L)))
    lfr_hi, lfr_lo = _split2(jnp.broadcast_to(lf_row, (L, L)))
    b_t = _dot(tri_l, lfc_hi) + _dot(tri_l, lfc_lo)
    b_s = _dot(lfr_hi, tri_u) + _dot(lfr_lo, tri_u)

    m_prev = m_ref[...]
    b_col = b_t[:, 0:1]
    log_d = jnp.where(causal, b_t - b_s + ig_row, NEG)
    m_inter = b_col + m_prev
    m_t = jnp.maximum(m_inter, jnp.max(log_d, axis=1, keepdims=True))
    a = jnp.exp(log_d - m_t) * _dot_nt(qb, kb)
    w_inter = jnp.exp(m_inter - m_t)
    c_prev = c_ref[...]
    n_prev = n_ref[...]
    num = _dot(a.astype(BF16), vb) + w_inter * _dot(qb, c_prev.astype(BF16))
    den = jnp.sum(a, axis=1, keepdims=True) + w_inter * jnp.sum(qc * n_prev, axis=1, keepdims=True)
    h = num / jnp.maximum(jnp.abs(den), jnp.exp(-m_t))

    m_new = m_t[L - 1:L, :]
    b_last = b_col[L - 1:L, :]
    ws_col = jnp.exp(b_last - b_col + ig_col - m_new)
    decay = jnp.exp(b_last + m_prev - m_new)
    c_ref[...] = decay * c_prev + _dot_tn(kb, (ws_col * vb.astype(F32)).astype(BF16))
    n_ref[...] = decay * n_prev + jnp.sum(ws_col * kc, axis=0, keepdims=True)
    m_ref[...] = m_new

    hn = h * lax.rsqrt(jnp.mean(h * h, axis=-1, keepdims=True) + EPS) * nw_ref[...]
    o_ref[...] = (jax.nn.sigmoid(mo_ref[...].astype(F32)) * hn).astype(o_ref.dtype)


def _mlstm(mq, mk, mv, mo, g_col, g_row, gate_b, conv_wq, conv_wk, conv_bq, conv_bk, norm_w):
    b, s, w = mq.shape
    L = MLSTM_CHUNK
    d = MLSTM_HEAD_DIM
    assert s % L == 0
    seq = pl.BlockSpec((None, s, d), lambda bi, h, c: (bi, 0, h))
    chunk = pl.BlockSpec((None, L, d), lambda bi, h, c: (bi, c, h))
    per_head = lambda rows: pl.BlockSpec((rows, d), lambda bi, h, c: (0, h))
    return pl.pallas_call(
        _mlstm_kernel,
        grid=(b, w // d, s // L),
        in_specs=[pl.BlockSpec(memory_space=pltpu.SMEM), seq, seq, chunk, chunk,
                  pl.BlockSpec((None, None, L, 2), lambda bi, h, c: (bi, h, c, 0)),
                  pl.BlockSpec((None, None, 2, L), lambda bi, h, c: (bi, h, 0, c)),
                  per_head(CONV_WIDTH), per_head(CONV_WIDTH), per_head(1), per_head(1), per_head(1)],
        out_specs=chunk,
        out_shape=jax.ShapeDtypeStruct((b, s, w), BF16),
        scratch_shapes=[pltpu.VMEM((d, d), F32), pltpu.VMEM((1, d), F32), pltpu.VMEM((1, 1), F32)],
        compiler_params=_params(("parallel", "parallel", "arbitrary")),
        name="mlstm",
    )(gate_b, mq, mk, mv, mo, g_col, g_row, conv_wq, conv_wk, conv_bq, conv_bk, norm_w)


def _merge_kernel(x_ref, attn_ref, hm_ref, ga_ref, gm_ref, wab_ref, wmb_ref, mb_ref, wout_ref,
                  fnw_ref, wq_ref, x2_ref, h2_ref, q_ref):
    ya = _dot(attn_ref[...], wab_ref[...])
    ym = _dot(hm_ref[...], wmb_ref[...])
    y = (jax.nn.sigmoid(ga_ref[...].astype(F32) + mb_ref[0:1, :]) * ya
         + jax.nn.sigmoid(gm_ref[...].astype(F32) + mb_ref[1:2, :]) * ym)
    x2 = x_ref[...] + _dot(y.astype(BF16), wout_ref[...])
    x2_ref[...] = x2
    h2 = x2 * lax.rsqrt(jnp.mean(x2 * x2, axis=-1, keepdims=True) + EPS) * fnw_ref[...]
    h2b = h2.astype(BF16)
    h2_ref[...] = h2b
    q_ref[...] = _dot(h2b, wq_ref[...])


def _merge(x2d, attn, hm, ga, gm, wab, wmb, merge_b, wout, fnw, wq):
    t, d = x2d.shape
    tm = MERGE_TM
    nq = wq.shape[1]
    row = lambda w: pl.BlockSpec((tm, w), lambda i: (i, 0))
    const = lambda a: pl.BlockSpec(a.shape, lambda i: (0, 0))
    return pl.pallas_call(
        _merge_kernel,
        grid=(t // tm,),
        in_specs=[row(d), row(ATTN_W), row(MLSTM_W), row(d), row(d), const(wab), const(wmb),
                  const(merge_b), const(wout), const(fnw), const(wq)],
        out_specs=[row(d), row(d), row(nq)],
        out_shape=[jax.ShapeDtypeStruct((t, d), F32), jax.ShapeDtypeStruct((t, d), BF16),
                   jax.ShapeDtypeStruct((t, nq), F32)],
        compiler_params=_params(("parallel",)),
        name="merge",
    )(x2d, attn, hm, ga, gm, wab, wmb, merge_b, wout, fnw, wq)


_CAND_ROWS = [16] + [8] * (PEER_TOPK - 1)


def _top16(s, key):
    n, tn = s.shape
    row16 = lax.broadcasted_iota(jnp.int32, (PEER_TOPK, tn), 0)

    def body(it, carry):
        s_cur, rank, vals = carry
        m = jnp.max(s_cur, axis=0, keepdims=True)
        idx = jnp.min(jnp.where(s_cur == m, key, float(n)), axis=0, keepdims=True)
        hit = key == idx
        itf = it.astype(F32)
        rank = jnp.where(hit, itf, rank)
        vals = jnp.where(row16 == it, m, vals)
        return jnp.where(hit, -jnp.inf, s_cur), rank, vals

    init = (s, jnp.full((n, tn), float(PEER_TOPK), F32), jnp.zeros((PEER_TOPK, tn), F32))
    _, rank, vals = lax.fori_loop(0, PEER_TOPK, body, init)
    return vals, rank


def _select_kernel(q1_ref, q2_ref, k1_ref, k2_ref, e1_ref, na_ref, e2_ref, r2_ref):
    tn = q1_ref.shape[0]
    nk = PEER_N_KEYS

    def scores(k_ref, q_ref):
        kh, kl = _split2(k_ref[...])
        qh, ql = _split2(q_ref[...])
        return _dot_nt(kh, qh) + _dot_nt(kh, ql) + _dot_nt(kl, qh)

    s1 = scores(k1_ref, q1_ref)
    s2 = scores(k2_ref, q2_ref)
    key = lax.broadcasted_iota(jnp.int32, (nk, tn), 0).astype(F32)
    v1, r1 = _top16(s1, key)
    v2, r2 = _top16(s2, key)

    blocks, flats = [], []
    for i, rows in enumerate(_CAND_ROWS):
        j = lax.broadcasted_iota(jnp.int32, (rows, tn), 0)
        cand = v1[i:i + 1, :] + v2[0:rows, :]
        blocks.append(jnp.where((i + 1) * (j + 1) <= PEER_TOPK, cand, -jnp.inf))
        flats.append((j + i * PEER_TOPK).astype(F32))
    cand = jnp.concatenate(blocks, axis=0)
    flat = jnp.concatenate(flats, axis=0)

    def pick(_, carry):
        c_cur, chosen = carry
        m = jnp.max(c_cur, axis=0, keepdims=True)
        idx = jnp.min(jnp.where(c_cur == m, flat, 1e9), axis=0, keepdims=True)
        hit = flat == idx
        return jnp.where(hit, -jnp.inf, c_cur), jnp.where(hit, 1.0, chosen)

    _, chosen = lax.fori_loop(0, PEER_TOPK, pick, (cand, jnp.zeros_like(cand)))

    ev1 = jnp.exp(v1 - v1[0:1, :])
    ev2 = jnp.exp(v2 - v2[0:1, :])
    z = jnp.zeros((1, tn), F32)
    counts = []
    off = 0
    for i, rows in enumerate(_CAND_ROWS):
        ch = chosen[off:off + rows, :]
        counts.append(jnp.sum(ch, axis=0, keepdims=True))
        z = z + jnp.sum(ch * (ev1[i:i + 1, :] * ev2[0:rows, :]), axis=0, keepdims=True)
        off += rows

    n_a = jnp.zeros((nk, tn), F32)
    for i in range(PEER_TOPK):
        n_a = jnp.where(r1 == float(i), counts[i], n_a)
    in1 = r1 < float(PEER_TOPK)
    in2 = r2 < float(PEER_TOPK)
    e1_ref[...] = jnp.where(in1, jnp.exp(s1 - v1[0:1, :]) / z, 0.0)
    na_ref[...] = n_a
    e2_ref[...] = jnp.where(in2, jnp.exp(s2 - v2[0:1, :]), 0.0)
    r2_ref[...] = r2


def _select(q, keys_1, keys_2):
    t = q.shape[0]
    tn = SELECT_TN
    assert t % tn == 0
    nk = PEER_N_KEYS
    q1 = pl.BlockSpec((tn, PEER_HALF), lambda i, h: (i, 2 * h))
    q2 = pl.BlockSpec((tn, PEER_HALF), lambda i, h: (i, 2 * h + 1))
    kspec = pl.BlockSpec((None, nk, PEER_HALF), lambda i, h: (h, 0, 0))
    ospec = pl.BlockSpec((None, nk, tn), lambda i, h: (h, 0, i))
    oshape = jax.ShapeDtypeStruct((PEER_HEADS, nk, t), F32)
    return pl.pallas_call(
        _select_kernel,
        grid=(t // tn, PEER_HEADS),
        in_specs=[q1, q2, kspec, kspec],
        out_specs=[ospec] * 4,
        out_shape=[oshape] * 4,
        compiler_params=_params(("parallel", "parallel")),
        name="peer_select",
    )(q, q, keys_1, keys_2)


def _experts_kernel(h2_ref, wd_ref, wut_ref, e1_ref, na_ref, e2_ref, r2_ref, x2_ref, o_ref, acc_ref):
    j = pl.program_id(1)
    nk = PEER_N_KEYS

    @pl.when(j == 0)
    def _():
        acc_ref[...] = jnp.zeros_like(acc_ref)

    z = _dot_nt(wd_ref[...], h2_ref[...])
    act = 0.5 * z * (1.0 + lax.erf(z * (2.0 ** -0.5)))
    gates = []
    for aa in range(EXPERT_CHUNK // nk):
        a = j * (EXPERT_CHUNK // nk) + aa
        g = jnp.zeros((nk, z.shape[1]), F32)
        for h in range(PEER_HEADS):
            e1 = e1_ref[h, pl.ds(a, 1), :]
            n_a = na_ref[h, pl.ds(a, 1), :]
            g = g + jnp.where(r2_ref[h] < n_a, e2_ref[h], 0.0) * e1
        gates.append(g)
    gate = jnp.concatenate(gates, axis=0)
    acc_ref[...] += _dot(wut_ref[...], (act * gate).astype(BF16))

    @pl.when(j == pl.num_programs(1) - 1)
    def _():
        o_ref[...] = x2_ref[...] + acc_ref[...].T


def _experts(h2, wd, wut, e1, na, e2, r2, x2):
    t, d = h2.shape
    tn = EXPERT_TN
    ne = wd.shape[0]
    nk = PEER_N_KEYS
    assert t % tn == 0 and ne % EXPERT_CHUNK == 0 and EXPERT_CHUNK % nk == 0
    tok = pl.BlockSpec((tn, d), lambda i, j: (i, 0))
    gate = pl.BlockSpec((PEER_HEADS, nk, tn), lambda i, j: (0, 0, i))
    return pl.pallas_call(
        _experts_kernel,
        grid=(t // tn, ne // EXPERT_CHUNK),
        in_specs=[tok, pl.BlockSpec((EXPERT_CHUNK, d), lambda i, j: (j, 0)),
                  pl.BlockSpec((d, EXPERT_CHUNK), lambda i, j: (0, j)), gate, gate, gate, gate, tok],
        out_specs=tok,
        out_shape=jax.ShapeDtypeStruct((t, d), F32),
        scratch_shapes=[pltpu.VMEM((d, tn), F32)],
        compiler_params=_params(("parallel", "arbitrary")),
        name="peer_experts",
    )(h2, wd, wut, e1, na, e2, r2, x2)


def _rope_tables(seq):
    pos = jnp.arange(seq, dtype=F32)
    inv_freq = ROPE_THETA ** (-jnp.arange(0, ROT_DIM, 2, dtype=F32) / ROT_DIM)
    ang = pos[:, None] * inv_freq[None, :]
    cos, sin = jnp.cos(ang), jnp.sin(ang)
    rest = ATTN_HEAD_DIM - ROT_DIM
    cos_h = jnp.concatenate([cos, cos, jnp.ones((seq, rest), F32)], axis=-1)
    sin_h = jnp.concatenate([-sin, sin, jnp.zeros((seq, rest), F32)], axis=-1)
    return jnp.tile(cos_h, (1, ATTN_HEADS)), jnp.tile(sin_h, (1, ATTN_HEADS))


def kernel(x, mix_norm_w, w_in, q_norm_w, k_norm_w, conv_w, conv_b, igate_b, fgate_b, mlstm_norm_w,
           w_attn_branch, w_mlstm_branch, merge_b, w_out, ffn_norm_w, peer_w_query, peer_keys_1,
           peer_keys_2, peer_w_down, peer_w_up):
    b, s, d = x.shape
    t = b * s
    x2d = x.reshape(t, d)

    n_main = 3 * ATTN_W + 4 * MLSTM_W
    n_gate = 2 * MLSTM_HEADS
    w_cat = jnp.concatenate(
        [w_in[:, :n_main], w_in[:, n_main + n_gate:], w_in[:, n_main:n_main + n_gate],
         jnp.zeros((d, LANES - n_gate), w_in.dtype)], axis=1).astype(BF16)
    cos_t, sin_t = _rope_tables(s)
    head_of = jnp.arange(ATTN_W) // ATTN_HEAD_DIM
    bd = (head_of[:, None] == head_of[None, :]).astype(BF16)
    qnw = jnp.tile(q_norm_w, ATTN_HEADS).reshape(1, ATTN_W)
    knw = jnp.tile(k_norm_w, ATTN_HEADS).reshape(1, ATTN_W)

    aq, ak, av, mq, mk, mv, mo, ga, gm, gt = _inproj(
        x2d, mix_norm_w.reshape(1, d), w_cat, cos_t, sin_t, bd, qnw, knw, s)

    attn = _moba(aq.reshape(b, s, ATTN_W), ak.reshape(b, s, ATTN_W), av.reshape(b, s, ATTN_W))

    gates = gt[:, :n_gate].reshape(b, s, 2, MLSTM_HEADS)
    g_col = gates.transpose(0, 3, 1, 2)
    g_row = gates.transpose(0, 3, 2, 1)
    gate_b = jnp.stack([igate_b, fgate_b]).astype(F32)
    r3 = lambda a: a.reshape(b, s, MLSTM_W)
    hm = _mlstm(r3(mq), r3(mk), r3(mv), r3(mo), g_col, g_row, gate_b,
                conv_w[:, :MLSTM_W], conv_w[:, MLSTM_W:], conv_b[:MLSTM_W].reshape(1, MLSTM_W),
                conv_b[MLSTM_W:].reshape(1, MLSTM_W), mlstm_norm_w.reshape(1, MLSTM_W))

    x2, h2, q = _merge(x2d, attn.reshape(t, ATTN_W), hm.reshape(t, MLSTM_W), ga, gm,
                       w_attn_branch.astype(BF16), w_mlstm_branch.astype(BF16), merge_b,
                       w_out.astype(BF16), ffn_norm_w.reshape(1, d), peer_w_query.astype(BF16))

    e1, na, e2, r2 = _select(q, peer_keys_1, peer_keys_2)
    out = _experts(h2, peer_w_down.astype(BF16), peer_w_up.T.astype(BF16), e1, na, e2, r2, x2)
    return out.reshape(b, s, d)
```

```python
import functools

import jax
import jax.numpy as jnp
from jax import lax
from jax.experimental import pallas as pl
from jax.experimental.pallas import tpu as pltpu

F32 = jnp.float32
BF16 = jnp.bfloat16

EPS = 1e-6
ATTN_HEADS = 8
ATTN_HEAD_DIM = 64
ATTN_W = ATTN_HEADS * ATTN_HEAD_DIM
ROT_DIM = ATTN_HEAD_DIM // 4
ROPE_THETA = 500000.0
MOBA_BLOCK = 256
MOBA_TOPK = 3
MLSTM_HEADS = 4
MLSTM_HEAD_DIM = 128
MLSTM_W = MLSTM_HEADS * MLSTM_HEAD_DIM
CONV_WIDTH = 4
PEER_HEADS = 8
PEER_N_KEYS = 128
PEER_HALF = 128
PEER_TOPK = 16

LANES = 128
SUBLANES = 8
NEG = -1e30
VMEM_LIMIT = 56 * 1024 * 1024

INPROJ_TM = 512
MERGE_TM = 512
MLSTM_CHUNK = 256
SELECT_TN = 256
EXPERT_TN = 512
EXPERT_CHUNK = 256
MXU_PIECES = 2


def _dot(a, b):
    return jnp.dot(a, b, preferred_element_type=F32)


def _dot_nt(a, b):
    return lax.dot_general(a, b, (((1,), (1,)), ((), ())), preferred_element_type=F32)


def _dot_tn(a, b):
    return lax.dot_general(a, b, (((0,), (0,)), ((), ())), preferred_element_type=F32)


def _bf16_bits(x):
    return lax.bitcast_convert_type(x.astype(BF16).astype(F32), jnp.int32)


def _pack_pairs(lo, hi):
    return lax.shift_right_logical(_bf16_bits(lo), jnp.int32(16)) | _bf16_bits(hi)


def _as_bf16_pairs(words):
    return pltpu.bitcast(words, BF16)


def _unpack_pairs(pairs):
    words = pltpu.bitcast(pairs, jnp.int32)
    lo = lax.bitcast_convert_type(lax.shift_left(words, jnp.int32(16)), F32)
    hi = lax.bitcast_convert_type(words & jnp.int32(-65536), F32)
    return lo, hi


def _split2(x):
    hi = x.astype(BF16)
    lo = (x - hi.astype(F32)).astype(BF16)
    return hi, lo


def _params(sem):
    return pltpu.CompilerParams(dimension_semantics=sem, vmem_limit_bytes=VMEM_LIMIT)


def _inproj_kernel(x_ref, nw_ref, w_ref, cos_ref, sin_ref, bd_ref, qnw_ref, knw_ref,
                   aq_ref, ak_ref, av_ref, mq_ref, mk_ref, mv_ref, mo_ref, ga_ref, gm_ref, gt_ref):
    x = x_ref[...]
    h = x * lax.rsqrt(jnp.mean(x * x, axis=-1, keepdims=True) + EPS) * nw_ref[...]
    hb = h.astype(BF16)

    def proj(lo, width):
        return _dot(hb, w_ref[:, lo:lo + width])

    lane = lax.broadcasted_iota(jnp.int32, (x.shape[0], ATTN_W), 1) % ATTN_HEAD_DIM
    half = ROT_DIM // 2

    def qk_norm_rope(t, nw):
        hi, lo = _split2(t * t)
        ms = (_dot(hi, bd_ref[...]) + _dot(lo, bd_ref[...])) * (1.0 / ATTN_HEAD_DIM)
        tn = t * lax.rsqrt(ms + EPS) * nw
        fwd = pltpu.roll(tn, ATTN_W - half, axis=1)
        bwd = pltpu.roll(tn, half, axis=1)
        swapped = jnp.where(lane < half, fwd, jnp.where(lane < ROT_DIM, bwd, 0.0))
        return tn * cos_ref[...] + swapped * sin_ref[...]

    aq_ref[...] = qk_norm_rope(proj(0, ATTN_W), qnw_ref[...]).astype(BF16)
    ak_ref[...] = qk_norm_rope(proj(ATTN_W, ATTN_W), knw_ref[...]).astype(BF16)
    av_ref[...] = proj(2 * ATTN_W, ATTN_W).astype(BF16)
    base = 3 * ATTN_W
    mq_ref[...] = proj(base, MLSTM_W).astype(BF16)
    mk_ref[...] = proj(base + MLSTM_W, MLSTM_W).astype(BF16)
    mv_ref[...] = proj(base + 2 * MLSTM_W, MLSTM_W).astype(BF16)
    mo_ref[...] = proj(base + 3 * MLSTM_W, MLSTM_W).astype(BF16)
    base = base + 4 * MLSTM_W
    d = x.shape[1]
    ga_ref[...] = proj(base, d).astype(BF16)
    gm_ref[...] = proj(base + d, d).astype(BF16)
    gt_ref[...] = proj(base + 2 * d, LANES)


def _inproj(x2d, mix_norm_w, w_cat, cos_t, sin_t, bd, qnw, knw, seq):
    t, d = x2d.shape
    tm = INPROJ_TM
    assert t % tm == 0 and seq % tm == 0
    tiles_per_seq = seq // tm
    ncols = w_cat.shape[1]
    row = lambda w: pl.BlockSpec((tm, w), lambda i: (i, 0))
    const = lambda shape: pl.BlockSpec(shape, lambda i: (0, 0))
    pos = pl.BlockSpec((tm, ATTN_W), lambda i: (i % tiles_per_seq, 0))
    outs = [ATTN_W] * 3 + [MLSTM_W] * 4 + [d, d]
    return pl.pallas_call(
        _inproj_kernel,
        grid=(t // tm,),
        in_specs=[row(d), const((1, d)), const((d, ncols)), pos, pos, const((ATTN_W, ATTN_W)),
                  const((1, ATTN_W)), const((1, ATTN_W))],
        out_specs=[row(w) for w in outs] + [row(LANES)],
        out_shape=[jax.ShapeDtypeStruct((t, w), BF16) for w in outs]
        + [jax.ShapeDtypeStruct((t, LANES), F32)],
        compiler_params=_params(("parallel",)),
        name="inproj",
    )(x2d, mix_norm_w, w_cat, cos_t, sin_t, bd, qnw, knw)


def _moba_kernel(q_ref, k_ref, v_ref, o_ref, kmean_ref, *, n_blocks):
    i = pl.program_id(2)
    blk = MOBA_BLOCK
    hd = ATTN_HEAD_DIM

    @pl.when(i == 0)
    def _():
        kmean_ref[...] = jnp.zeros_like(kmean_ref)
        for n in range(n_blocks):
            kb = k_ref[n * blk:(n + 1) * blk, :].astype(F32)
            kmean_ref[n:n + 1, :] = jnp.mean(kb, axis=0, keepdims=True)

    q = q_ref[...]
    lane = lax.broadcasted_iota(jnp.int32, (blk, LANES), 1)
    kmean_hi, kmean_lo = _split2(kmean_ref[...])
    own = pl.multiple_of(i * blk, blk)
    k_own = k_ref[pl.ds(own, blk), :]
    v_own = v_ref[pl.ds(own, blk), :]
    qpos = lax.broadcasted_iota(jnp.int32, (blk, blk), 0)
    kpos = lax.broadcasted_iota(jnp.int32, (blk, blk), 1)
    brow = lax.broadcasted_iota(jnp.int32, (16, blk), 0)

    def one_head(hh):
        in_head = (lane >= hh * hd) & (lane < (hh + 1) * hd)
        qh = jnp.where(in_head, q, jnp.zeros_like(q)) * jnp.asarray(hd ** -0.5, BF16)

        g_t = (_dot_nt(kmean_hi, qh) + _dot_nt(kmean_lo, qh))[0:16, :]
        rank = jnp.zeros((16, blk), F32)
        for m in range(min(n_blocks, 16)):
            gm = g_t[m:m + 1, :]
            beats = jnp.where(gm > g_t, 1.0, jnp.where(gm == g_t, jnp.where(brow > m, 1.0, 0.0), 0.0))
            rank = rank + jnp.where(m < i, beats, 0.0)
        sel_t = jnp.where(brow < i, jnp.where(rank < MOBA_TOPK, 1.0, 0.0), 0.0)
        sel = jnp.concatenate([sel_t, jnp.zeros((LANES - 16, blk), F32)], axis=0).T

        s = _dot_nt(qh, k_own)
        s = jnp.where(kpos <= qpos, s, NEG)
        m0 = jnp.max(s, axis=1, keepdims=True)
        p = jnp.exp(s - m0)
        l0 = jnp.sum(p, axis=1, keepdims=True)
        acc0 = _dot(p.astype(BF16), v_own)

        def body(n, carry):
            m_run, l_run, acc = carry
            start = pl.multiple_of(n * blk, blk)
            k_n = k_ref[pl.ds(start, blk), :]
            v_n = v_ref[pl.ds(start, blk), :]
            col = jnp.sum(jnp.where(lane == n, sel, 0.0), axis=1, keepdims=True)
            s_n = jnp.where(col > 0.5, _dot_nt(qh, k_n), NEG)
            m_new = jnp.maximum(m_run, jnp.max(s_n, axis=1, keepdims=True))
            alpha = jnp.exp(m_run - m_new)
            p_n = jnp.exp(s_n - m_new)
            l_new = alpha * l_run + jnp.sum(p_n, axis=1, keepdims=True)
            acc_new = alpha * acc + _dot(p_n.astype(BF16), v_n)
            return m_new, l_new, acc_new

        _, l_fin, acc_fin = lax.fori_loop(0, i, body, (m0, l0, acc0))
        return acc_fin / l_fin

    out = jnp.where(lane < hd, one_head(0), one_head(1))
    o_ref[...] = out.astype(o_ref.dtype)


def _moba(q, k, v):
    b, s, w = q.shape
    assert s % MOBA_BLOCK == 0 and w % LANES == 0
    n_blocks = s // MOBA_BLOCK
    assert n_blocks <= 16
    qspec = pl.BlockSpec((None, MOBA_BLOCK, LANES), lambda bi, hp, i: (bi, i, hp))
    kvspec = pl.BlockSpec((None, s, LANES), lambda bi, hp, i: (bi, 0, hp))
    return pl.pallas_call(
        functools.partial(_moba_kernel, n_blocks=n_blocks),
        grid=(b, w // LANES, n_blocks),
        in_specs=[qspec, kvspec, kvspec],
        out_specs=qspec,
        out_shape=jax.ShapeDtypeStruct((b, s, w), BF16),
        scratch_shapes=[pltpu.VMEM((LANES, LANES), F32)],
        compiler_params=_params(("parallel", "parallel", "arbitrary")),
        name="moba",
    )(q, k, v)


def _log_sigmoid(f):
    return jnp.minimum(f, 0.0) - jnp.log(1.0 + jnp.exp(-jnp.abs(f)))


def _mlstm_kernel(gb_ref, mq_ref, mk_ref, mv_ref, mo_ref, gc_ref, gr_ref, cwq_ref, cwk_ref,
                  cbq_ref, cbk_ref, nw_ref, o_ref, c_ref, n_ref, m_ref):
    hh = pl.program_id(1)
    c = pl.program_id(2)
    L = MLSTM_CHUNK
    d = MLSTM_HEAD_DIM

    @pl.when(c == 0)
    def _():
        c_ref[...] = jnp.zeros_like(c_ref)
        n_ref[...] = jnp.zeros_like(n_ref)
        m_ref[...] = jnp.zeros_like(m_ref)

    start = pl.multiple_of(c * L, L)
    prev = pl.multiple_of(jnp.maximum(c - 1, 0) * L, L)
    row_d = lax.broadcasted_iota(jnp.int32, (L, d), 0)

    def conv_silu(u_ref, w_ref, b_ref):
        u = u_ref[pl.ds(start, L), :].astype(F32)
        pv = u_ref[pl.ds(prev, L), :].astype(F32)
        pv = jnp.where(c > 0, pv, 0.0)
        y = u * w_ref[CONV_WIDTH - 1:CONV_WIDTH, :] + b_ref[...]
        for k in range(1, CONV_WIDTH):
            shifted = jnp.where(row_d < k, pltpu.roll(pv, k, axis=0), pltpu.roll(u, k, axis=0))
            y = y + shifted * w_ref[CONV_WIDTH - 1 - k:CONV_WIDTH - k, :]
        return y * jax.nn.sigmoid(y)

    qc = conv_silu(mq_ref, cwq_ref, cbq_ref)
    kc = conv_silu(mk_ref, cwk_ref, cbk_ref) * (d ** -0.5)
    qb = qc.astype(BF16)
    kb = kc.astype(BF16)
    vb = mv_ref[...]

    ib = gb_ref[0, hh]
    fb = gb_ref[1, hh]
    gcol = gc_ref[...]
    grow = gr_ref[...]
    ig_col = gcol[:, 0:1] + ib
    ig_row = grow[0:1, :] + ib
    lf_col = _log_sigmoid(gcol[:, 1:2] + fb)
    lf_row = _log_sigmoid(grow[1:2, :] + fb)

    r = lax.broadcasted_iota(jnp.int32, (L, L), 0)
    s = lax.broadcasted_iota(jnp.int32, (L, L), 1)
    causal = s <= r
    tri_l = jnp.where(causal, 1.0, 0.0).astype(BF16)
    tri_u = jnp.where(r <= s, 1.0, 0.0).astype(BF16)
    lfc_hi, lfc_lo = _split2(jnp.broadcast_to(lf_col, (L, L)))
    lfr_hi, lfr_lo = _split2(jnp.broadcast_to(lf_row, (L, L)))
    b_t = _dot(tri_l, lfc_hi) + _dot(tri_l, lfc_lo)
    b_s = _dot(lfr_hi, tri_u) + _dot(lfr_lo, tri_u)

    m_prev = m_ref[...]
    b_col = b_t[:, 0:1]
    log_d = jnp.where(causal, b_t - b_s + ig_row, NEG)
    m_inter = b_col + m_prev
    m_t = jnp.maximum(m_inter, jnp.max(log_d, axis=1, keepdims=True))
    a = jnp.exp(log_d - m_t) * _dot_nt(qb, kb)
    w_inter = jnp.exp(m_inter - m_t)
    c_prev = c_ref[...]
    n_prev = n_ref[...]
    num = _dot(a.astype(BF16), vb) + w_inter * _dot(qb, c_prev.astype(BF16))
    den = jnp.sum(a, axis=1, keepdims=True) + w_inter * jnp.sum(qc * n_prev, axis=1, keepdims=True)
    h = num / jnp.maximum(jnp.abs(den), jnp.exp(-m_t))

    m_new = m_t[L - 1:L, :]
    b_last = b_col[L - 1:L, :]
    ws_col = jnp.exp(b_last - b_col + ig_col - m_new)
    decay = jnp.exp(b_last + m_prev - m_new)
    c_ref[...] = decay * c_prev + _dot_tn(kb, (ws_col * vb.astype(F32)).astype(BF16))
    n_ref[...] = decay * n_prev + jnp.sum(ws_col * kc, axis=0, keepdims=True)
    m_ref[...] = m_new

    hn = h * lax.rsqrt(jnp.mean(h * h, axis=-1, keepdims=True) + EPS) * nw_ref[...]
    o_ref[...] = (jax.nn.sigmoid(mo_ref[...].astype(F32)) * hn).astype(o_ref.dtype)


def _mlstm(mq, mk, mv, mo, g_col, g_row, gate_b, conv_wq, conv_wk, conv_bq, conv_bk, norm_w):
    b, s, w = mq.shape
    L = MLSTM_CHUNK
    d = MLSTM_HEAD_DIM
    assert s % L == 0
    seq = pl.BlockSpec((None, s, d), lambda bi, h, c: (bi, 0, h))
    chunk = pl.BlockSpec((None, L, d), lambda bi, h, c: (bi, c, h))
    per_head = lambda rows: pl.BlockSpec((rows, d), lambda bi, h, c: (0, h))
    return pl.pallas_call(
        _mlstm_kernel,
        grid=(b, w // d, s // L),
        in_specs=[pl.BlockSpec(memory_space=pltpu.SMEM), seq, seq, chunk, chunk,
                  pl.BlockSpec((None, None, L, 2), lambda bi, h, c: (bi, h, c, 0)),
                  pl.BlockSpec((None, None, 2, L), lambda bi, h, c: (bi, h, 0, c)),
                  per_head(CONV_WIDTH), per_head(CONV_WIDTH), per_head(1), per_head(1), per_head(1)],
        out_specs=chunk,
        out_shape=jax.ShapeDtypeStruct((b, s, w), BF16),
        scratch_shapes=[pltpu.VMEM((d, d), F32), pltpu.VMEM((1, d), F32), pltpu.VMEM((1, 1), F32)],
        compiler_params=_params(("parallel", "parallel", "arbitrary")),
        name="mlstm",
    )(gate_b, mq, mk, mv, mo, g_col, g_row, conv_wq, conv_wk, conv_bq, conv_bk, norm_w)


def _merge_kernel(x_ref, attn_ref, hm_ref, ga_ref, gm_ref, wab_ref, wmb_ref, mb_ref, wout_ref,
                  fnw_ref, wq_ref, x2_ref, h2_ref, q_ref):
    ya = _dot(attn_ref[...], wab_ref[...])
    ym = _dot(hm_ref[...], wmb_ref[...])
    y = (jax.nn.sigmoid(ga_ref[...].astype(F32) + mb_ref[0:1, :]) * ya
         + jax.nn.sigmoid(gm_ref[...].astype(F32) + mb_ref[1:2, :]) * ym)
    x2 = x_ref[...] + _dot(y.astype(BF16), wout_ref[...])
    x2_ref[...] = x2
    h2 = x2 * lax.rsqrt(jnp.mean(x2 * x2, axis=-1, keepdims=True) + EPS) * fnw_ref[...]
    h2b = h2.astype(BF16)
    h2_ref[...] = h2b
    q_ref[...] = _dot(h2b, wq_ref[...])


def _merge(x2d, attn, hm, ga, gm, wab, wmb, merge_b, wout, fnw, wq):
    t, d = x2d.shape
    tm = MERGE_TM
    nq = wq.shape[1]
    row = lambda w: pl.BlockSpec((tm, w), lambda i: (i, 0))
    const = lambda a: pl.BlockSpec(a.shape, lambda i: (0, 0))
    return pl.pallas_call(
        _merge_kernel,
        grid=(t // tm,),
        in_specs=[row(d), row(ATTN_W), row(MLSTM_W), row(d), row(d), const(wab), const(wmb),
                  const(merge_b), const(wout), const(fnw), const(wq)],
        out_specs=[row(d), row(d), row(nq)],
        out_shape=[jax.ShapeDtypeStruct((t, d), F32), jax.ShapeDtypeStruct((t, d), BF16),
                   jax.ShapeDtypeStruct((t, nq), F32)],
        compiler_params=_params(("parallel",)),
        name="merge",
    )(x2d, attn, hm, ga, gm, wab, wmb, merge_b, wout, fnw, wq)


_CAND_ROWS = [16] + [8] * (PEER_TOPK - 1)


def _top16(s, key):
    n, tn = s.shape
    row16 = lax.broadcasted_iota(jnp.int32, (PEER_TOPK, tn), 0)

    def body(it, carry):
        s_cur, rank, vals = carry
        m = jnp.max(s_cur, axis=0, keepdims=True)
        idx = jnp.min(jnp.where(s_cur == m, key, float(n)), axis=0, keepdims=True)
        hit = key == idx
        itf = it.astype(F32)
        rank = jnp.where(hit, itf, rank)
        vals = jnp.where(row16 == it, m, vals)
        return jnp.where(hit, -jnp.inf, s_cur), rank, vals

    init = (s, jnp.full((n, tn), float(PEER_TOPK), F32), jnp.zeros((PEER_TOPK, tn), F32))
    _, rank, vals = lax.fori_loop(0, PEER_TOPK, body, init)
    return vals, rank


def _select_kernel(q1_ref, q2_ref, k1_ref, k2_ref, e1_ref, na_ref, e2_ref, r2_ref):
    tn = q1_ref.shape[0]
    nk = PEER_N_KEYS

    def scores(k_ref, q_ref):
        kh, kl = _split2(k_ref[...])
        qh, ql = _split2(q_ref[...])
        return _dot_nt(kh, qh) + _dot_nt(kh, ql) + _dot_nt(kl, qh)

    s1 = scores(k1_ref, q1_ref)
    s2 = scores(k2_ref, q2_ref)
    key = lax.broadcasted_iota(jnp.int32, (nk, tn), 0).astype(F32)
    v1, r1 = _top16(s1, key)
    v2, r2 = _top16(s2, key)

    blocks, flats = [], []
    for i, rows in enumerate(_CAND_ROWS):
        j = lax.broadcasted_iota(jnp.int32, (rows, tn), 0)
        cand = v1[i:i + 1, :] + v2[0:rows, :]
        blocks.append(jnp.where((i + 1) * (j + 1) <= PEER_TOPK, cand, -jnp.inf))
        flats.append((j + i * PEER_TOPK).astype(F32))
    cand = jnp.concatenate(blocks, axis=0)
    flat = jnp.concatenate(flats, axis=0)

    def pick(_, carry):
        c_cur, chosen = carry
        m = jnp.max(c_cur, axis=0, keepdims=True)
        idx = jnp.min(jnp.where(c_cur == m, flat, 1e9), axis=0, keepdims=True)
        hit = flat == idx
        return jnp.where(hit, -jnp.inf, c_cur), jnp.where(hit, 1.0, chosen)

    _, chosen = lax.fori_loop(0, PEER_TOPK, pick, (cand, jnp.zeros_like(cand)))

    ev1 = jnp.exp(v1 - v1[0:1, :])
    ev2 = jnp.exp(v2 - v2[0:1, :])
    z = jnp.zeros((1, tn), F32)
    counts = []
    off = 0
    for i, rows in enumerate(_CAND_ROWS):
        ch = chosen[off:off + rows, :]
        counts.append(jnp.sum(ch, axis=0, keepdims=True))
        z = z + jnp.sum(ch * (ev1[i:i + 1, :] * ev2[0:rows, :]), axis=0, keepdims=True)
        off += rows

    n_a = jnp.zeros((nk, tn), F32)
    for i in range(PEER_TOPK):
        n_a = jnp.where(r1 == float(i), counts[i], n_a)
    in1 = r1 < float(PEER_TOPK)
    in2 = r2 < float(PEER_TOPK)
    e1 = jnp.where(in1, jnp.exp(s1 - v1[0:1, :]) / z, 0.0)
    e2 = jnp.where(in2, jnp.exp(s2 - v2[0:1, :]), 0.0)
    e1_ref[...] = _pack_pairs(e1, e1)
    na_ref[...] = _pack_pairs(n_a, n_a)
    e2_ref[...] = _pack_pairs(e2[:nk // 2], e2[nk // 2:])
    r2_ref[...] = _pack_pairs(r2[:nk // 2], r2[nk // 2:])


def _select(q, keys_1, keys_2):
    t = q.shape[0]
    tn = SELECT_TN
    assert t % tn == 0
    nk = PEER_N_KEYS
    q1 = pl.BlockSpec((tn, PEER_HALF), lambda i, h: (i, 2 * h))
    q2 = pl.BlockSpec((tn, PEER_HALF), lambda i, h: (i, 2 * h + 1))
    kspec = pl.BlockSpec((None, nk, PEER_HALF), lambda i, h: (h, 0, 0))
    ospec = lambda rows: pl.BlockSpec((None, rows, tn), lambda i, h: (h, 0, i))
    oshape = lambda rows: jax.ShapeDtypeStruct((PEER_HEADS, rows, t), jnp.int32)
    return pl.pallas_call(
        _select_kernel,
        grid=(t // tn, PEER_HEADS),
        in_specs=[q1, q2, kspec, kspec],
        out_specs=[ospec(nk), ospec(nk), ospec(nk // 2), ospec(nk // 2)],
        out_shape=[oshape(nk), oshape(nk), oshape(nk // 2), oshape(nk // 2)],
        compiler_params=_params(("parallel", "parallel")),
        name="peer_select",
    )(q, q, keys_1, keys_2)


def _experts_kernel(h2_ref, wd_ref, wut_ref, e1_in, na_in, e2_in, r2_in, x2_ref, o_ref,
                    acc_ref, z0_ref, z1_ref, act0_ref, act1_ref, e1_ref, na_ref, e2_ref, r2_ref):
    s = pl.program_id(1)
    n_chunks = pl.num_programs(1) - 2
    nk = PEER_N_KEYS
    tn = z0_ref.shape[1]
    rows_per_chunk = EXPERT_CHUNK // nk

    @pl.when(s == 0)
    def _():
        for ref in (acc_ref, z0_ref, z1_ref, act0_ref, act1_ref):
            ref[...] = jnp.zeros_like(ref)
        e2_ref[...] = e2_in[...]
        r2_ref[...] = r2_in[...]

    e1_ref[...] = e1_in[...]
    na_ref[...] = na_in[...]

    def gate_tile(lt, z_r, act_w):
        half = nk // 2
        groups = half // SUBLANES
        cols = slice(lt * LANES, (lt + 1) * LANES)
        g = [[jnp.zeros((2 * SUBLANES, LANES), BF16) for _ in range(groups)]
             for _ in range(rows_per_chunk)]
        for h in range(PEER_HEADS):
            as_pair = lambda ref, aa: _as_bf16_pairs(
                jnp.broadcast_to(ref[h, aa:aa + 1, cols], (SUBLANES, LANES)))
            e1 = [as_pair(e1_ref, aa) for aa in range(rows_per_chunk)]
            n_a = [as_pair(na_ref, aa) for aa in range(rows_per_chunk)]
            for k in range(groups):
                grp = slice(k * SUBLANES, (k + 1) * SUBLANES)
                r2 = _as_bf16_pairs(r2_ref[h, grp, cols])
                e2 = _as_bf16_pairs(e2_ref[h, grp, cols])
                for aa in range(rows_per_chunk):
                    g[aa][k] = g[aa][k] + jnp.where(r2 < n_a[aa], e2, jnp.zeros_like(e2)) * e1[aa]
        for aa in range(rows_per_chunk):
            for k in range(groups):
                g_lo, g_hi = _unpack_pairs(g[aa][k])
                for g_half, base in ((g_lo, aa * nk), (g_hi, aa * nk + half)):
                    rows = slice(base + k * SUBLANES, base + (k + 1) * SUBLANES)
                    z = z_r[rows, cols]
                    act = 0.5 * z * (1.0 + lax.erf(z * (2.0 ** -0.5)))
                    act_w[rows, cols] = (act * g_half).astype(BF16)

    def step(z_w, z_r, act_w, act_r):
        tiles_per_piece = tn // LANES // MXU_PIECES
        for p in range(MXU_PIECES):
            pc = slice(p * tn // MXU_PIECES, (p + 1) * tn // MXU_PIECES)
            z_w[:, pc] = _dot_nt(wd_ref[...], h2_ref[pc, :])
            for lt in range(p * tiles_per_piece, p * tiles_per_piece + tiles_per_piece // 2):
                gate_tile(lt, z_r, act_w)
            acc_ref[:, pc] += _dot(wut_ref[...], act_r[:, pc])
            for lt in range(p * tiles_per_piece + tiles_per_piece // 2, (p + 1) * tiles_per_piece):
                gate_tile(lt, z_r, act_w)

    @pl.when(s % 2 == 0)
    def _():
        step(z0_ref, z1_ref, act1_ref, act0_ref)

    @pl.when(s % 2 == 1)
    def _():
        step(z1_ref, z0_ref, act0_ref, act1_ref)

    @pl.when(s == pl.num_programs(1) - 1)
    def _():
        o_ref[...] = x2_ref[...] + acc_ref[...].T


def _experts(h2, wd, wut, e1, na, e2, r2, x2):
    t, d = h2.shape
    tn = EXPERT_TN
    ne = wd.shape[0]
    nk = PEER_N_KEYS
    assert t % tn == 0 and ne % EXPERT_CHUNK == 0 and EXPERT_CHUNK % nk == 0
    n_chunks = ne // EXPERT_CHUNK
    assert n_chunks % 2 == 0
    tok = pl.BlockSpec((tn, d), lambda i, s: (i, 0))
    gate = pl.BlockSpec((PEER_HEADS, nk // 2, tn), lambda i, s: (0, 0, i))
    rpc = EXPERT_CHUNK // nk
    row = pl.BlockSpec((PEER_HEADS, None, rpc, tn),
                       lambda i, s: (0, jnp.clip(s - 1, 0, n_chunks - 1), 0, i))
    e1 = e1.reshape(PEER_HEADS, nk // rpc, rpc, t)
    na = na.reshape(PEER_HEADS, nk // rpc, rpc, t)
    return pl.pallas_call(
        _experts_kernel,
        grid=(t // tn, n_chunks + 2),
        in_specs=[tok,
                  pl.BlockSpec((EXPERT_CHUNK, d), lambda i, s: (jnp.minimum(s, n_chunks - 1), 0)),
                  pl.BlockSpec((None, d, EXPERT_CHUNK), lambda i, s: (jnp.clip(s - 2, 0, n_chunks - 1), 0, 0)),
                  row, row, gate, gate, tok],
        out_specs=tok,
        out_shape=jax.ShapeDtypeStruct((t, d), F32),
        scratch_shapes=[pltpu.VMEM((d, tn), F32),
                        pltpu.VMEM((EXPERT_CHUNK, tn), F32), pltpu.VMEM((EXPERT_CHUNK, tn), F32),
                        pltpu.VMEM((EXPERT_CHUNK, tn), BF16), pltpu.VMEM((EXPERT_CHUNK, tn), BF16),
                        pltpu.VMEM((PEER_HEADS, rpc, tn), jnp.int32), pltpu.VMEM((PEER_HEADS, rpc, tn), jnp.int32),
                        pltpu.VMEM((PEER_HEADS, nk // 2, tn), jnp.int32),
                        pltpu.VMEM((PEER_HEADS, nk // 2, tn), jnp.int32)],
        compiler_params=_params(("parallel", "arbitrary")),
        name="peer_experts",
    )(h2, wd, wut, e1, na, e2, r2, x2)


def _rope_tables(seq):
    pos = jnp.arange(seq, dtype=F32)
    inv_freq = ROPE_THETA ** (-jnp.arange(0, ROT_DIM, 2, dtype=F32) / ROT_DIM)
    ang = pos[:, None] * inv_freq[None, :]
    cos, sin = jnp.cos(ang), jnp.sin(ang)
    rest = ATTN_HEAD_DIM - ROT_DIM
    cos_h = jnp.concatenate([cos, cos, jnp.ones((seq, rest), F32)], axis=-1)
    sin_h = jnp.concatenate([-sin, sin, jnp.zeros((seq, rest), F32)], axis=-1)
    return jnp.tile(cos_h, (1, ATTN_HEADS)), jnp.tile(sin_h, (1, ATTN_HEADS))


def kernel(x, mix_norm_w, w_in, q_norm_w, k_norm_w, conv_w, conv_b, igate_b, fgate_b, mlstm_norm_w,
           w_attn_branch, w_mlstm_branch, merge_b, w_out, ffn_norm_w, peer_w_query, peer_keys_1,
           peer_keys_2, peer_w_down, peer_w_up):
    b, s, d = x.shape
    t = b * s
    x2d = x.reshape(t, d)

    n_main = 3 * ATTN_W + 4 * MLSTM_W
    n_gate = 2 * MLSTM_HEADS
    w_cat = jnp.concatenate(
        [w_in[:, :n_main], w_in[:, n_main + n_gate:], w_in[:, n_main:n_main + n_gate],
         jnp.zeros((d, LANES - n_gate), w_in.dtype)], axis=1).astype(BF16)
    cos_t, sin_t = _rope_tables(s)
    head_of = jnp.arange(ATTN_W) // ATTN_HEAD_DIM
    bd = (head_of[:, None] == head_of[None, :]).astype(BF16)
    qnw = jnp.tile(q_norm_w, ATTN_HEADS).reshape(1, ATTN_W)
    knw = jnp.tile(k_norm_w, ATTN_HEADS).reshape(1, ATTN_W)

    aq, ak, av, mq, mk, mv, mo, ga, gm, gt = _inproj(
        x2d, mix_norm_w.reshape(1, d), w_cat, cos_t, sin_t, bd, qnw, knw, s)

    attn = _moba(aq.reshape(b, s, ATTN_W), ak.reshape(b, s, ATTN_W), av.reshape(b, s, ATTN_W))

    gates = gt[:, :n_gate].reshape(b, s, 2, MLSTM_HEADS)
    g_col = gates.transpose(0, 3, 1, 2)
    g_row = gates.transpose(0, 3, 2, 1)
    gate_b = jnp.stack([igate_b, fgate_b]).astype(F32)
    r3 = lambda a: a.reshape(b, s, MLSTM_W)
    hm = _mlstm(r3(mq), r3(mk), r3(mv), r3(mo), g_col, g_row, gate_b,
                conv_w[:, :MLSTM_W], conv_w[:, MLSTM_W:], conv_b[:MLSTM_W].reshape(1, MLSTM_W),
                conv_b[MLSTM_W:].reshape(1, MLSTM_W), mlstm_norm_w.reshape(1, MLSTM_W))

    x2, h2, q = _merge(x2d, attn.reshape(t, ATTN_W), hm.reshape(t, MLSTM_W), ga, gm,
                       w_attn_branch.astype(BF16), w_mlstm_branch.astype(BF16), merge_b,
                       w_out.astype(BF16), ffn_norm_w.reshape(1, d), peer_w_query.astype(BF16))

    e1, na, e2, r2 = _select(q, peer_keys_1, peer_keys_2)
    wut = peer_w_up.astype(BF16).reshape(-1, EXPERT_CHUNK, d).transpose(0, 2, 1)
    out = _experts(h2, peer_w_down.astype(BF16), wut, e1, na, e2, r2, x2)
    return out.reshape(b, s, d)
```

```python
import functools

import jax
import jax.numpy as jnp
from jax import lax
from jax.experimental import pallas as pl
from jax.experimental.pallas import tpu as pltpu

F32 = jnp.float32
BF16 = jnp.bfloat16

EPS = 1e-6
ATTN_HEADS = 8
ATTN_HEAD_DIM = 64
ATTN_W = ATTN_HEADS * ATTN_HEAD_DIM
ROT_DIM = ATTN_HEAD_DIM // 4
ROPE_THETA = 500000.0
MOBA_BLOCK = 256
MOBA_TOPK = 3
MLSTM_HEADS = 4
MLSTM_HEAD_DIM = 128
MLSTM_W = MLSTM_HEADS * MLSTM_HEAD_DIM
CONV_WIDTH = 4
PEER_HEADS = 8
PEER_N_KEYS = 128
PEER_HALF = 128
PEER_TOPK = 16

LANES = 128
SUBLANES = 8
NEG = -1e30
VMEM_LIMIT = 56 * 1024 * 1024

INPROJ_TM = 512
MERGE_TM = 512
MLSTM_CHUNK = 256
SELECT_TN = 512
EXPERT_TN = 512
EXPERT_CHUNK = 256
MXU_PIECES = 2


def _dot(a, b):
    return jnp.dot(a, b, preferred_element_type=F32)


def _dot_nt(a, b):
    return lax.dot_general(a, b, (((1,), (1,)), ((), ())), preferred_element_type=F32)


def _dot_tn(a, b):
    return lax.dot_general(a, b, (((0,), (0,)), ((), ())), preferred_element_type=F32)


def _bf16_bits(x):
    return lax.bitcast_convert_type(x.astype(BF16).astype(F32), jnp.int32)


def _pack_pairs(lo, hi):
    return lax.shift_right_logical(_bf16_bits(lo), jnp.int32(16)) | _bf16_bits(hi)


def _as_bf16_pairs(words):
    return pltpu.bitcast(words, BF16)


def _unpack_pairs(pairs):
    words = pltpu.bitcast(pairs, jnp.int32)
    lo = lax.bitcast_convert_type(lax.shift_left(words, jnp.int32(16)), F32)
    hi = lax.bitcast_convert_type(words & jnp.int32(-65536), F32)
    return lo, hi


def _split2(x):
    hi = x.astype(BF16)
    lo = (x - hi.astype(F32)).astype(BF16)
    return hi, lo


def _params(sem):
    return pltpu.CompilerParams(dimension_semantics=sem, vmem_limit_bytes=VMEM_LIMIT)


def _inproj_kernel(x_ref, nw_ref, w_ref, cos_ref, sin_ref, bd_ref, qnw_ref, knw_ref,
                   aq_ref, ak_ref, av_ref, mq_ref, mk_ref, mv_ref, mo_ref, ga_ref, gm_ref, gt_ref):
    x = x_ref[...]
    h = x * lax.rsqrt(jnp.mean(x * x, axis=-1, keepdims=True) + EPS) * nw_ref[...]
    hb = h.astype(BF16)

    def proj(lo, width):
        return _dot(hb, w_ref[:, lo:lo + width])

    lane = lax.broadcasted_iota(jnp.int32, (x.shape[0], ATTN_W), 1) % ATTN_HEAD_DIM
    half = ROT_DIM // 2

    def qk_norm_rope(t, nw):
        hi, lo = _split2(t * t)
        ms = (_dot(hi, bd_ref[...]) + _dot(lo, bd_ref[...])) * (1.0 / ATTN_HEAD_DIM)
        tn = t * lax.rsqrt(ms + EPS) * nw
        fwd = pltpu.roll(tn, ATTN_W - half, axis=1)
        bwd = pltpu.roll(tn, half, axis=1)
        swapped = jnp.where(lane < half, fwd, jnp.where(lane < ROT_DIM, bwd, 0.0))
        return tn * cos_ref[...] + swapped * sin_ref[...]

    aq_ref[...] = qk_norm_rope(proj(0, ATTN_W), qnw_ref[...]).astype(BF16)
    ak_ref[...] = qk_norm_rope(proj(ATTN_W, ATTN_W), knw_ref[...]).astype(BF16)
    av_ref[...] = proj(2 * ATTN_W, ATTN_W).astype(BF16)
    base = 3 * ATTN_W
    mq_ref[...] = proj(base, MLSTM_W).astype(BF16)
    mk_ref[...] = proj(base + MLSTM_W, MLSTM_W).astype(BF16)
    mv_ref[...] = proj(base + 2 * MLSTM_W, MLSTM_W).astype(BF16)
    mo_ref[...] = proj(base + 3 * MLSTM_W, MLSTM_W).astype(BF16)
    base = base + 4 * MLSTM_W
    d = x.shape[1]
    ga_ref[...] = proj(base, d).astype(BF16)
    gm_ref[...] = proj(base + d, d).astype(BF16)
    gt_ref[...] = proj(base + 2 * d, LANES)


def _inproj(x2d, mix_norm_w, w_cat, cos_t, sin_t, bd, qnw, knw, seq):
    t, d = x2d.shape
    tm = INPROJ_TM
    assert t % tm == 0 and seq % tm == 0
    tiles_per_seq = seq // tm
    ncols = w_cat.shape[1]
    row = lambda w: pl.BlockSpec((tm, w), lambda i: (i, 0))
    const = lambda shape: pl.BlockSpec(shape, lambda i: (0, 0))
    pos = pl.BlockSpec((tm, ATTN_W), lambda i: (i % tiles_per_seq, 0))
    outs = [ATTN_W] * 3 + [MLSTM_W] * 4 + [d, d]
    return pl.pallas_call(
        _inproj_kernel,
        grid=(t // tm,),
        in_specs=[row(d), const((1, d)), const((d, ncols)), pos, pos, const((ATTN_W, ATTN_W)),
                  const((1, ATTN_W)), const((1, ATTN_W))],
        out_specs=[row(w) for w in outs] + [row(LANES)],
        out_shape=[jax.ShapeDtypeStruct((t, w), BF16) for w in outs]
        + [jax.ShapeDtypeStruct((t, LANES), F32)],
        compiler_params=_params(("parallel",)),
        name="inproj",
    )(x2d, mix_norm_w, w_cat, cos_t, sin_t, bd, qnw, knw)


def _moba_kernel(q_ref, k_ref, v_ref, o_ref, kmean_ref, kext_ref, vext_ref, s_ref, *, n_blocks):
    i = pl.program_id(2)
    blk = MOBA_BLOCK
    hd = ATTN_HEAD_DIM
    heads = (0, 1)
    lane = lax.broadcasted_iota(jnp.int32, (blk, LANES), 1)
    in_head = [lane < hd, lane >= hd]
    spare0 = [hd, 0]

    @pl.when(i == 0)
    def _():
        kmean_ref[...] = jnp.zeros_like(kmean_ref)
        for n in range(n_blocks):
            rows = slice(n * blk, (n + 1) * blk)
            kb = k_ref[rows, :]
            vb = v_ref[rows, :]
            kmean_ref[n:n + 1, :] = jnp.mean(kb.astype(F32), axis=0, keepdims=True)
            for h in heads:
                onehot = jnp.where(lane == spare0[h] + n, 1.0, 0.0).astype(BF16)
                kext_ref[h, rows, :] = jnp.where(in_head[h], kb, onehot)
                vext_ref[h, rows, :] = jnp.where(in_head[h], vb, jnp.ones_like(vb))

    q = q_ref[...]
    kmean_hi, kmean_lo = _split2(kmean_ref[...])
    brow = lax.broadcasted_iota(jnp.int32, (16, blk), 0)
    qext = []
    for h in heads:
        qh = jnp.where(in_head[h], q, jnp.zeros_like(q)) * jnp.asarray(hd ** -0.5, BF16)
        g_t = (_dot_nt(kmean_hi, qh) + _dot_nt(kmean_lo, qh))[0:16, :]
        rank = jnp.zeros((16, blk), F32)
        for m in range(min(n_blocks, 16)):
            gm = g_t[m:m + 1, :]
            beats = jnp.where(gm > g_t, 1.0, jnp.where(gm == g_t, jnp.where(brow > m, 1.0, 0.0), 0.0))
            rank = rank + jnp.where(m < i, beats, 0.0)
        bias_t = jnp.where(brow < i, jnp.where(rank < MOBA_TOPK, 0.0, NEG), 0.0)
        bias = jnp.concatenate([bias_t, jnp.zeros((LANES - 16, blk), F32)], axis=0).T
        if spare0[h]:
            bias = pltpu.roll(bias, spare0[h], axis=1)
        qext.append(qh + bias.astype(BF16))

    own = pl.multiple_of(i * blk, blk)
    causal = (lax.broadcasted_iota(jnp.int32, (blk, blk), 1)
              <= lax.broadcasted_iota(jnp.int32, (blk, blk), 0))
    m_own = []
    for h in heads:
        s = jnp.where(causal, _dot_nt(qext[h], kext_ref[h, pl.ds(own, blk), :]), NEG)
        s_ref[h, i] = s
        m_own.append(jnp.max(s, axis=1, keepdims=True))

    def scores(n, m_run):
        start = pl.multiple_of(n * blk, blk)
        m_new = []
        for h in heads:
            s = _dot_nt(qext[h], kext_ref[h, pl.ds(start, blk), :])
            s_ref[h, n] = s
            m_new.append(jnp.maximum(m_run[h], jnp.max(s, axis=1, keepdims=True)))
        return tuple(m_new)

    m_fin = lax.fori_loop(0, i, scores, tuple(m_own))

    def values(n, acc):
        start = pl.multiple_of(n * blk, blk)
        out = []
        for h in heads:
            p = jnp.exp(s_ref[h, n] - m_fin[h]).astype(BF16)
            out.append(acc[h] + _dot(p, vext_ref[h, pl.ds(start, blk), :]))
        return tuple(out)

    acc = lax.fori_loop(0, i + 1, values, tuple(jnp.zeros((blk, LANES), F32) for _ in heads))
    outs = [a / pltpu.roll(a, hd, axis=1) for a in acc]
    o_ref[...] = jnp.where(in_head[0], outs[0], outs[1]).astype(o_ref.dtype)


def _moba(q, k, v):
    b, s, w = q.shape
    assert s % MOBA_BLOCK == 0 and w % LANES == 0
    n_blocks = s // MOBA_BLOCK
    assert n_blocks <= 16
    qspec = pl.BlockSpec((None, MOBA_BLOCK, LANES), lambda bi, hp, i: (bi, i, hp))
    kvspec = pl.BlockSpec((None, s, LANES), lambda bi, hp, i: (bi, 0, hp))
    return pl.pallas_call(
        functools.partial(_moba_kernel, n_blocks=n_blocks),
        grid=(b, w // LANES, n_blocks),
        in_specs=[qspec, kvspec, kvspec],
        out_specs=qspec,
        out_shape=jax.ShapeDtypeStruct((b, s, w), BF16),
        scratch_shapes=[pltpu.VMEM((LANES, LANES), F32),
                        pltpu.VMEM((2, s, LANES), BF16), pltpu.VMEM((2, s, LANES), BF16),
                        pltpu.VMEM((2, n_blocks, MOBA_BLOCK, MOBA_BLOCK), F32)],
        compiler_params=_params(("parallel", "parallel", "arbitrary")),
        name="moba",
    )(q, k, v)


def _log_sigmoid(f):
    return jnp.minimum(f, 0.0) - jnp.log(1.0 + jnp.exp(-jnp.abs(f)))


def _mlstm_kernel(gb_ref, mq_ref, mk_ref, mv_ref, mo_ref, gc_ref, gr_ref, cwq_ref, cwk_ref,
                  cbq_ref, cbk_ref, nw_ref, o_ref, c_ref, n_ref, m_ref):
    hh = pl.program_id(1)
    c = pl.program_id(2)
    L = MLSTM_CHUNK
    d = MLSTM_HEAD_DIM

    @pl.when(c == 0)
    def _():
        c_ref[...] = jnp.zeros_like(c_ref)
        n_ref[...] = jnp.zeros_like(n_ref)
        m_ref[...] = jnp.zeros_like(m_ref)

    start = pl.multiple_of(c * L, L)
    prev = pl.multiple_of(jnp.maximum(c - 1, 0) * L, L)
    row_d = lax.broadcasted_iota(jnp.int32, (L, d), 0)

    def conv_silu(u_ref, w_ref, b_ref):
        u = u_ref[pl.ds(start, L), :].astype(F32)
        pv = u_ref[pl.ds(prev, L), :].astype(F32)
        pv = jnp.where(c > 0, pv, 0.0)
        y = u * w_ref[CONV_WIDTH - 1:CONV_WIDTH, :] + b_ref[...]
        for k in range(1, CONV_WIDTH):
            shifted = jnp.where(row_d < k, pltpu.roll(pv, k, axis=0), pltpu.roll(u, k, axis=0))
            y = y + shifted * w_ref[CONV_WIDTH - 1 - k:CONV_WIDTH - k, :]
        return y * jax.nn.sigmoid(y)

    qc = conv_silu(mq_ref, cwq_ref, cbq_ref)
    kc = conv_silu(mk_ref, cwk_ref, cbk_ref) * (d ** -0.5)
    qb = qc.astype(BF16)
    kb = kc.astype(BF16)
    vb = mv_ref[...]

    ib = gb_ref[0, hh]
    fb = gb_ref[1, hh]
    gcol = gc_ref[...]
    grow = gr_ref[...]
    ig_col = gcol[:, 0:1] + ib
    ig_row = grow[0:1, :] + ib
    lf_col = _log_sigmoid(gcol[:, 1:2] + fb)
    lf_row = _log_sigmoid(grow[1:2, :] + fb)

    r = lax.broadcasted_iota(jnp.int32, (L, L), 0)
    s = lax.broadcasted_iota(jnp.int32, (L, L), 1)
    causal = s <= r
    tri_l = jnp.where(causal, 1.0, 0.0).astype(BF16)
    tri_u = jnp.where(r <= s, 1.0, 0.0).astype(BF16)
    lfc_hi, lfc_lo = _split2(jnp.broadcast_to(lf_col, (L, L)))
    lfr_hi, lfr_lo = _split2(jnp.broadcast_to(lf_row, (L, L)))
    b_t = _dot(tri_l, lfc_hi) + _dot(tri_l, lfc_lo)
    b_s = _dot(lfr_hi, tri_u) + _dot(lfr_lo, tri_u)

    m_prev = m_ref[...]
    b_col = b_t[:, 0:1]
    log_d = jnp.where(causal, b_t - b_s + ig_row, NEG)
    m_inter = b_col + m_prev
    m_t = jnp.maximum(m_inter, jnp.max(log_d, axis=1, keepdims=True))
    a = jnp.exp(log_d - m_t) * _dot_nt(qb, kb)
    w_inter = jnp.exp(m_inter - m_t)
    c_prev = c_ref[...]
    n_prev = n_ref[...]
    num = _dot(a.astype(BF16), vb) + w_inter * _dot(qb, c_prev.astype(BF16))
    den = jnp.sum(a, axis=1, keepdims=True) + w_inter * jnp.sum(qc * n_prev, axis=1, keepdims=True)
    h = num / jnp.maximum(jnp.abs(den), jnp.exp(-m_t))

    m_new = m_t[L - 1:L, :]
    b_last = b_col[L - 1:L, :]
    ws_col = jnp.exp(b_last - b_col + ig_col - m_new)
    decay = jnp.exp(b_last + m_prev - m_new)
    c_ref[...] = decay * c_prev + _dot_tn(kb, (ws_col * vb.astype(F32)).astype(BF16))
    n_ref[...] = decay * n_prev + jnp.sum(ws_col * kc, axis=0, keepdims=True)
    m_ref[...] = m_new

    hn = h * lax.rsqrt(jnp.mean(h * h, axis=-1, keepdims=True) + EPS) * nw_ref[...]
    o_ref[...] = (jax.nn.sigmoid(mo_ref[...].astype(F32)) * hn).astype(o_ref.dtype)


def _mlstm(mq, mk, mv, mo, g_col, g_row, gate_b, conv_wq, conv_wk, conv_bq, conv_bk, norm_w):
    b, s, w = mq.shape
    L = MLSTM_CHUNK
    d = MLSTM_HEAD_DIM
    assert s % L == 0
    seq = pl.BlockSpec((None, s, d), lambda bi, h, c: (bi, 0, h))
    chunk = pl.BlockSpec((None, L, d), lambda bi, h, c: (bi, c, h))
    per_head = lambda rows: pl.BlockSpec((rows, d), lambda bi, h, c: (0, h))
    return pl.pallas_call(
        _mlstm_kernel,
        grid=(b, w // d, s // L),
        in_specs=[pl.BlockSpec(memory_space=pltpu.SMEM), seq, seq, chunk, chunk,
                  pl.BlockSpec((None, None, L, 2), lambda bi, h, c: (bi, h, c, 0)),
                  pl.BlockSpec((None, None, 2, L), lambda bi, h, c: (bi, h, 0, c)),
                  per_head(CONV_WIDTH), per_head(CONV_WIDTH), per_head(1), per_head(1), per_head(1)],
        out_specs=chunk,
        out_shape=jax.ShapeDtypeStruct((b, s, w), BF16),
        scratch_shapes=[pltpu.VMEM((d, d), F32), pltpu.VMEM((1, d), F32), pltpu.VMEM((1, 1), F32)],
        compiler_params=_params(("parallel", "parallel", "arbitrary")),
        name="mlstm",
    )(gate_b, mq, mk, mv, mo, g_col, g_row, conv_wq, conv_wk, conv_bq, conv_bk, norm_w)


def _merge_kernel(x_ref, attn_ref, hm_ref, ga_ref, gm_ref, wab_ref, wmb_ref, mb_ref, wout_ref,
                  fnw_ref, wq_ref, x2_ref, h2_ref, q_ref):
    ya = _dot(attn_ref[...], wab_ref[...])
    ym = _dot(hm_ref[...], wmb_ref[...])
    y = (jax.nn.sigmoid(ga_ref[...].astype(F32) + mb_ref[0:1, :]) * ya
         + jax.nn.sigmoid(gm_ref[...].astype(F32) + mb_ref[1:2, :]) * ym)
    x2 = x_ref[...] + _dot(y.astype(BF16), wout_ref[...])
    x2_ref[...] = x2
    h2 = x2 * lax.rsqrt(jnp.mean(x2 * x2, axis=-1, keepdims=True) + EPS) * fnw_ref[...]
    h2b = h2.astype(BF16)
    h2_ref[...] = h2b
    q_ref[...] = _dot(h2b, wq_ref[...])


def _merge(x2d, attn, hm, ga, gm, wab, wmb, merge_b, wout, fnw, wq):
    t, d = x2d.shape
    tm = MERGE_TM
    nq = wq.shape[1]
    row = lambda w: pl.BlockSpec((tm, w), lambda i: (i, 0))
    const = lambda a: pl.BlockSpec(a.shape, lambda i: (0, 0))
    return pl.pallas_call(
        _merge_kernel,
        grid=(t // tm,),
        in_specs=[row(d), row(ATTN_W), row(MLSTM_W), row(d), row(d), const(wab), const(wmb),
                  const(merge_b), const(wout), const(fnw), const(wq)],
        out_specs=[row(d), row(d), row(nq)],
        out_shape=[jax.ShapeDtypeStruct((t, d), F32), jax.ShapeDtypeStruct((t, d), BF16),
                   jax.ShapeDtypeStruct((t, nq), F32)],
        compiler_params=_params(("parallel",)),
        name="merge",
    )(x2d, attn, hm, ga, gm, wab, wmb, merge_b, wout, fnw, wq)


_CAND_ROWS = [16] + [8] * (PEER_TOPK - 1)


def _select_kernel(q1_ref, q2_ref, k1_ref, k2_ref, e1_ref, na_ref, e2_ref, r2_ref,
                   s1_ref, s2_ref, w1_ref, w2_ref, c_ref):
    tn = q1_ref.shape[0]
    nk = PEER_N_KEYS
    topk = PEER_TOPK
    n_lt = tn // LANES
    lane_tiles = [slice(lt * LANES, (lt + 1) * LANES) for lt in range(n_lt)]

    def scores(k_ref, q_ref):
        kh, kl = _split2(k_ref[...])
        qh, ql = _split2(q_ref[...])
        return _dot_nt(kh, qh) + _dot_nt(kh, ql) + _dot_nt(kl, qh)

    for s_ref, w_ref, k_ref, q_ref in ((s1_ref, w1_ref, k1_ref, q1_ref), (s2_ref, w2_ref, k2_ref, q2_ref)):
        sc = scores(k_ref, q_ref)
        s_ref[...] = sc
        w_ref[...] = sc

    key = lax.broadcasted_iota(jnp.int32, (nk, LANES), 0).astype(F32)

    vals = [[[] for _ in lane_tiles] for _ in range(2)]
    idxs = [[[] for _ in lane_tiles] for _ in range(2)]
    for _ in range(topk):
        for half, w_ref in enumerate((w1_ref, w2_ref)):
            for lt, cols in enumerate(lane_tiles):
                s_cur = w_ref[:, cols]
                m = jnp.max(s_cur, axis=0, keepdims=True)
                idx = jnp.min(jnp.where(s_cur == m, key, float(nk)), axis=0, keepdims=True)
                w_ref[:, cols] = jnp.where(key == idx, -jnp.inf, s_cur)
                vals[half][lt].append(m)
                idxs[half][lt].append(idx)

    row16 = lax.broadcasted_iota(jnp.int32, (topk, LANES), 0)
    flats, valid = [], []
    for i, rows in enumerate(_CAND_ROWS):
        j = lax.broadcasted_iota(jnp.int32, (rows, LANES), 0)
        flats.append((j + i * topk).astype(F32))
        valid.append((i + 1) * (j + 1) <= topk)
    flat = jnp.concatenate(flats, axis=0)
    offs = [sum(_CAND_ROWS[:i]) for i in range(topk)]
    v2_dense = []
    for lt, cols in enumerate(lane_tiles):
        v2d = jnp.zeros((topk, LANES), F32)
        for j in range(topk):
            v2d = jnp.where(row16 == j, vals[1][lt][j], v2d)
        v2_dense.append(v2d)
        for i, rows in enumerate(_CAND_ROWS):
            c_ref[offs[i]:offs[i] + rows, cols] = jnp.where(valid[i], vals[0][lt][i] + v2d[0:rows, :], -jnp.inf)

    for _ in range(topk):
        for lt, cols in enumerate(lane_tiles):
            c_cur = c_ref[:, cols]
            m = jnp.max(c_cur, axis=0, keepdims=True)
            idx = jnp.min(jnp.where(c_cur == m, flat, 1e9), axis=0, keepdims=True)
            c_ref[:, cols] = jnp.where(flat == idx, -jnp.inf, c_cur)

    for lt, cols in enumerate(lane_tiles):
        v1, v2d = vals[0][lt], v2_dense[lt]
        ev2 = jnp.exp(v2d - v2d[0:1, :])
        z = jnp.zeros((1, LANES), F32)
        counts = []
        for i, rows in enumerate(_CAND_ROWS):
            ch = jnp.where(valid[i], jnp.where(c_ref[offs[i]:offs[i] + rows, cols] == -jnp.inf, 1.0, 0.0), 0.0)
            counts.append(jnp.sum(ch, axis=0, keepdims=True))
            ev1_i = jnp.exp(v1[i] - v1[0])
            z = z + jnp.sum(ch * (ev1_i * ev2[0:rows, :]), axis=0, keepdims=True)

        n_a = jnp.zeros((nk, LANES), F32)
        r2 = jnp.full((nk, LANES), float(topk), F32)
        for i in range(topk):
            n_a = jnp.where(key == idxs[0][lt][i], counts[i], n_a)
            r2 = jnp.where(key == idxs[1][lt][i], float(i), r2)
        in1 = w1_ref[:, cols] == -jnp.inf
        in2 = w2_ref[:, cols] == -jnp.inf
        e1 = jnp.where(in1, jnp.exp(s1_ref[:, cols] - v1[0]) / z, 0.0)
        e2 = jnp.where(in2, jnp.exp(s2_ref[:, cols] - v2d[0:1, :]), 0.0)
        e1_ref[:, cols] = _pack_pairs(e1, e1)
        na_ref[:, cols] = _pack_pairs(n_a, n_a)
        e2_ref[:, cols] = _pack_pairs(e2[:nk // 2], e2[nk // 2:])
        r2_ref[:, cols] = _pack_pairs(r2[:nk // 2], r2[nk // 2:])


def _select(q, keys_1, keys_2):
    t = q.shape[0]
    tn = SELECT_TN
    assert t % tn == 0
    nk = PEER_N_KEYS
    q1 = pl.BlockSpec((tn, PEER_HALF), lambda i, h: (i, 2 * h))
    q2 = pl.BlockSpec((tn, PEER_HALF), lambda i, h: (i, 2 * h + 1))
    kspec = pl.BlockSpec((None, nk, PEER_HALF), lambda i, h: (h, 0, 0))
    ospec = lambda rows: pl.BlockSpec((None, rows, tn), lambda i, h: (h, 0, i))
    oshape = lambda rows: jax.ShapeDtypeStruct((PEER_HEADS, rows, t), jnp.int32)
    keys_buf = pltpu.VMEM((nk, tn), F32)
    return pl.pallas_call(
        _select_kernel,
        grid=(t // tn, PEER_HEADS),
        in_specs=[q1, q2, kspec, kspec],
        out_specs=[ospec(nk), ospec(nk), ospec(nk // 2), ospec(nk // 2)],
        out_shape=[oshape(nk), oshape(nk), oshape(nk // 2), oshape(nk // 2)],
        scratch_shapes=[keys_buf, keys_buf, keys_buf, keys_buf, pltpu.VMEM((sum(_CAND_ROWS), tn), F32)],
        compiler_params=_params(("parallel", "parallel")),
        name="peer_select",
    )(q, q, keys_1, keys_2)


def _experts_kernel(h2_ref, wd_ref, wut_ref, e1_in, na_in, e2_in, r2_in, x2_ref, o_ref,
                    acc_ref, z0_ref, z1_ref, act0_ref, act1_ref, e1_ref, na_ref, e2_ref, r2_ref):
    s = pl.program_id(1)
    n_chunks = pl.num_programs(1) - 2
    nk = PEER_N_KEYS
    tn = z0_ref.shape[1]
    rows_per_chunk = EXPERT_CHUNK // nk

    @pl.when(s == 0)
    def _():
        for ref in (acc_ref, z0_ref, z1_ref, act0_ref, act1_ref):
            ref[...] = jnp.zeros_like(ref)
        e2_ref[...] = e2_in[...]
        r2_ref[...] = r2_in[...]

    chunk = jnp.clip(s - 1, 0, n_chunks - 1)
    r0 = (chunk % (SUBLANES // rows_per_chunk)) * rows_per_chunk
    for h in range(PEER_HEADS):
        for aa in range(rows_per_chunk):
            e1_ref[h, aa:aa + 1, :] = e1_in[h, pl.ds(r0 + aa, 1), :]
            na_ref[h, aa:aa + 1, :] = na_in[h, pl.ds(r0 + aa, 1), :]

    def gate_tile(lt, z_r, act_w):
        half = nk // 2
        groups = half // SUBLANES
        cols = slice(lt * LANES, (lt + 1) * LANES)
        g = [[jnp.zeros((2 * SUBLANES, LANES), BF16) for _ in range(groups)]
             for _ in range(rows_per_chunk)]
        for h in range(PEER_HEADS):
            as_pair = lambda ref, aa: _as_bf16_pairs(
                jnp.broadcast_to(ref[h, aa:aa + 1, cols], (SUBLANES, LANES)))
            e1 = [as_pair(e1_ref, aa) for aa in range(rows_per_chunk)]
            n_a = [as_pair(na_ref, aa) for aa in range(rows_per_chunk)]
            for k in range(groups):
                grp = slice(k * SUBLANES, (k + 1) * SUBLANES)
                r2 = _as_bf16_pairs(r2_ref[h, grp, cols])
                e2 = _as_bf16_pairs(e2_ref[h, grp, cols])
                for aa in range(rows_per_chunk):
                    g[aa][k] = g[aa][k] + jnp.where(r2 < n_a[aa], e2, jnp.zeros_like(e2)) * e1[aa]
        for aa in range(rows_per_chunk):
            for k in range(groups):
                g_lo, g_hi = _unpack_pairs(g[aa][k])
                for g_half, base in ((g_lo, aa * nk), (g_hi, aa * nk + half)):
                    rows = slice(base + k * SUBLANES, base + (k + 1) * SUBLANES)
                    z = z_r[rows, cols]
                    act = 0.5 * z * (1.0 + lax.erf(z * (2.0 ** -0.5)))
                    act_w[rows, cols] = (act * g_half).astype(BF16)

    def step(z_w, z_r, act_w, act_r):
        tiles_per_piece = tn // LANES // MXU_PIECES
        for p in range(MXU_PIECES):
            pc = slice(p * tn // MXU_PIECES, (p + 1) * tn // MXU_PIECES)
            z_w[:, pc] = _dot_nt(wd_ref[...], h2_ref[pc, :])
            for lt in range(p * tiles_per_piece, p * tiles_per_piece + tiles_per_piece // 2):
                gate_tile(lt, z_r, act_w)
            acc_ref[:, pc] += _dot(wut_ref[...], act_r[:, pc])
            for lt in range(p * tiles_per_piece + tiles_per_piece // 2, (p + 1) * tiles_per_piece):
                gate_tile(lt, z_r, act_w)

    @pl.when(s % 2 == 0)
    def _():
        step(z0_ref, z1_ref, act1_ref, act0_ref)

    @pl.when(s % 2 == 1)
    def _():
        step(z1_ref, z0_ref, act0_ref, act1_ref)

    @pl.when(s == pl.num_programs(1) - 1)
    def _():
        o_ref[...] = x2_ref[...] + acc_ref[...].T


def _experts(h2, wd, wut, e1, na, e2, r2, x2):
    t, d = h2.shape
    tn = EXPERT_TN
    ne = wd.shape[0]
    nk = PEER_N_KEYS
    assert t % tn == 0 and ne % EXPERT_CHUNK == 0 and EXPERT_CHUNK % nk == 0
    n_chunks = ne // EXPERT_CHUNK
    assert n_chunks % 2 == 0
    tok = pl.BlockSpec((tn, d), lambda i, s: (i, 0))
    gate = pl.BlockSpec((PEER_HEADS, nk // 2, tn), lambda i, s: (0, 0, i))
    rpc = EXPERT_CHUNK // nk
    assert SUBLANES % rpc == 0
    chunks_per_group = SUBLANES // rpc
    row = pl.BlockSpec((PEER_HEADS, None, SUBLANES, tn),
                       lambda i, s: (0, jnp.clip(s - 1, 0, n_chunks - 1) // chunks_per_group, 0, i))
    e1 = e1.reshape(PEER_HEADS, nk // SUBLANES, SUBLANES, t)
    na = na.reshape(PEER_HEADS, nk // SUBLANES, SUBLANES, t)
    return pl.pallas_call(
        _experts_kernel,
        grid=(t // tn, n_chunks + 2),
        in_specs=[tok,
                  pl.BlockSpec((EXPERT_CHUNK, d), lambda i, s: (jnp.minimum(s, n_chunks - 1), 0)),
                  pl.BlockSpec((None, d, EXPERT_CHUNK), lambda i, s: (jnp.clip(s - 2, 0, n_chunks - 1), 0, 0)),
                  row, row, gate, gate, tok],
        out_specs=tok,
        out_shape=jax.ShapeDtypeStruct((t, d), F32),
        scratch_shapes=[pltpu.VMEM((d, tn), F32),
                        pltpu.VMEM((EXPERT_CHUNK, tn), F32), pltpu.VMEM((EXPERT_CHUNK, tn), F32),
                        pltpu.VMEM((EXPERT_CHUNK, tn), BF16), pltpu.VMEM((EXPERT_CHUNK, tn), BF16),
                        pltpu.VMEM((PEER_HEADS, rpc, tn), jnp.int32), pltpu.VMEM((PEER_HEADS, rpc, tn), jnp.int32),
                        pltpu.VMEM((PEER_HEADS, nk // 2, tn), jnp.int32),
                        pltpu.VMEM((PEER_HEADS, nk // 2, tn), jnp.int32)],
        compiler_params=_params(("parallel", "arbitrary")),
        name="peer_experts",
    )(h2, wd, wut, e1, na, e2, r2, x2)


def _rope_tables(seq):
    pos = jnp.arange(seq, dtype=F32)
    inv_freq = ROPE_THETA ** (-jnp.arange(0, ROT_DIM, 2, dtype=F32) / ROT_DIM)
    ang = pos[:, None] * inv_freq[None, :]
    cos, sin = jnp.cos(ang), jnp.sin(ang)
    rest = ATTN_HEAD_DIM - ROT_DIM
    cos_h = jnp.concatenate([cos, cos, jnp.ones((seq, rest), F32)], axis=-1)
    sin_h = jnp.concatenate([-sin, sin, jnp.zeros((seq, rest), F32)], axis=-1)
    return jnp.tile(cos_h, (1, ATTN_HEADS)), jnp.tile(sin_h, (1, ATTN_HEADS))


def kernel(x, mix_norm_w, w_in, q_norm_w, k_norm_w, conv_w, conv_b, igate_b, fgate_b, mlstm_norm_w,
           w_attn_branch, w_mlstm_branch, merge_b, w_out, ffn_norm_w, peer_w_query, peer_keys_1,
           peer_keys_2, peer_w_down, peer_w_up):
    b, s, d = x.shape
    t = b * s
    x2d = x.reshape(t, d)

    n_main = 3 * ATTN_W + 4 * MLSTM_W
    n_gate = 2 * MLSTM_HEADS
    w_cat = jnp.concatenate(
        [w_in[:, :n_main], w_in[:, n_main + n_gate:], w_in[:, n_main:n_main + n_gate],
         jnp.zeros((d, LANES - n_gate), w_in.dtype)], axis=1).astype(BF16)
    cos_t, sin_t = _rope_tables(s)
    head_of = jnp.arange(ATTN_W) // ATTN_HEAD_DIM
    bd = (head_of[:, None] == head_of[None, :]).astype(BF16)
    qnw = jnp.tile(q_norm_w, ATTN_HEADS).reshape(1, ATTN_W)
    knw = jnp.tile(k_norm_w, ATTN_HEADS).reshape(1, ATTN_W)

    aq, ak, av, mq, mk, mv, mo, ga, gm, gt = _inproj(
        x2d, mix_norm_w.reshape(1, d), w_cat, cos_t, sin_t, bd, qnw, knw, s)

    attn = _moba(aq.reshape(b, s, ATTN_W), ak.reshape(b, s, ATTN_W), av.reshape(b, s, ATTN_W))

    gates = gt[:, :n_gate].reshape(b, s, 2, MLSTM_HEADS)
    g_col = gates.transpose(0, 3, 1, 2)
    g_row = gates.transpose(0, 3, 2, 1)
    gate_b = jnp.stack([igate_b, fgate_b]).astype(F32)
    r3 = lambda a: a.reshape(b, s, MLSTM_W)
    hm = _mlstm(r3(mq), r3(mk), r3(mv), r3(mo), g_col, g_row, gate_b,
                conv_w[:, :MLSTM_W], conv_w[:, MLSTM_W:], conv_b[:MLSTM_W].reshape(1, MLSTM_W),
                conv_b[MLSTM_W:].reshape(1, MLSTM_W), mlstm_norm_w.reshape(1, MLSTM_W))

    x2, h2, q = _merge(x2d, attn.reshape(t, ATTN_W), hm.reshape(t, MLSTM_W), ga, gm,
                       w_attn_branch.astype(BF16), w_mlstm_branch.astype(BF16), merge_b,
                       w_out.astype(BF16), ffn_norm_w.reshape(1, d), peer_w_query.astype(BF16))

    e1, na, e2, r2 = _select(q, peer_keys_1, peer_keys_2)
    wut = peer_w_up.astype(BF16).reshape(-1, EXPERT_CHUNK, d).transpose(0, 2, 1)
    out = _experts(h2, peer_w_down.astype(BF16), wut, e1, na, e2, r2, x2)
    return out.reshape(b, s, d)
```

```python
import functools

import jax
import jax.numpy as jnp
from jax import lax
from jax.experimental import pallas as pl
from jax.experimental.pallas import tpu as pltpu

F32 = jnp.float32
BF16 = jnp.bfloat16

EPS = 1e-6
ATTN_HEADS = 8
ATTN_HEAD_DIM = 64
ATTN_W = ATTN_HEADS * ATTN_HEAD_DIM
ROT_DIM = ATTN_HEAD_DIM // 4
ROPE_THETA = 500000.0
MOBA_BLOCK = 256
MOBA_TOPK = 3
MLSTM_HEADS = 4
MLSTM_HEAD_DIM = 128
MLSTM_W = MLSTM_HEADS * MLSTM_HEAD_DIM
CONV_WIDTH = 4
PEER_HEADS = 8
PEER_N_KEYS = 128
PEER_HALF = 128
PEER_TOPK = 16

LANES = 128
SUBLANES = 8
NEG = -1e30
VMEM_LIMIT = 56 * 1024 * 1024

INPROJ_TM = 512
MERGE_TM = 512
MLSTM_CHUNK = 256
SELECT_TN = 512
EXPERT_TN = 512
EXPERT_CHUNK = 256
MXU_PIECES = 2
UP_ROW_PIECES = 1
GATE_GROUPS = 8


def _dot(a, b):
    return jnp.dot(a, b, preferred_element_type=F32)


def _dot_nt(a, b):
    return lax.dot_general(a, b, (((1,), (1,)), ((), ())), preferred_element_type=F32)


def _dot_tn(a, b):
    return lax.dot_general(a, b, (((0,), (0,)), ((), ())), preferred_element_type=F32)


def _bf16_bits(x):
    return lax.bitcast_convert_type(x.astype(BF16).astype(F32), jnp.int32)


def _pack_pairs(lo, hi):
    return lax.shift_right_logical(_bf16_bits(lo), jnp.int32(16)) | _bf16_bits(hi)


def _as_bf16_pairs(words):
    return pltpu.bitcast(words, BF16)


def _unpack_pairs(pairs):
    words = pltpu.bitcast(pairs, jnp.int32)
    lo = lax.bitcast_convert_type(lax.shift_left(words, jnp.int32(16)), F32)
    hi = lax.bitcast_convert_type(words & jnp.int32(-65536), F32)
    return lo, hi


def _pack_rows(x):
    *lead, m, n = x.shape
    pairs = jnp.swapaxes(x.reshape(*lead, m // 2, 2, n), -1, -2)
    return lax.bitcast_convert_type(pairs, jnp.int32)


def _rows_bf16(words):
    return pltpu.bitcast(words, BF16)


def _split2(x):
    hi = x.astype(BF16)
    lo = (x - hi.astype(F32)).astype(BF16)
    return hi, lo


def _params(sem, flags=None):
    return pltpu.CompilerParams(dimension_semantics=sem, vmem_limit_bytes=VMEM_LIMIT, flags=flags)


def _inproj_kernel(x_ref, nw_ref, w_ref, cos_ref, sin_ref, bd_ref, qnw_ref, knw_ref,
                   aq_ref, ak_ref, av_ref, mq_ref, mk_ref, mv_ref, mo_ref, ga_ref, gm_ref, gt_ref):
    x = x_ref[...]
    h = x * lax.rsqrt(jnp.mean(x * x, axis=-1, keepdims=True) + EPS) * nw_ref[...]
    hb = h.astype(BF16)

    def proj(lo, width):
        return _dot(hb, w_ref[:, lo:lo + width])

    lane = lax.broadcasted_iota(jnp.int32, (x.shape[0], ATTN_W), 1) % ATTN_HEAD_DIM
    half = ROT_DIM // 2

    def qk_norm_rope(t, nw):
        hi, lo = _split2(t * t)
        ms = (_dot(hi, bd_ref[...]) + _dot(lo, bd_ref[...])) * (1.0 / ATTN_HEAD_DIM)
        tn = t * lax.rsqrt(ms + EPS) * nw
        fwd = pltpu.roll(tn, ATTN_W - half, axis=1)
        bwd = pltpu.roll(tn, half, axis=1)
        swapped = jnp.where(lane < half, fwd, jnp.where(lane < ROT_DIM, bwd, 0.0))
        return tn * cos_ref[...] + swapped * sin_ref[...]

    aq_ref[...] = qk_norm_rope(proj(0, ATTN_W), qnw_ref[...]).astype(BF16)
    ak_ref[...] = qk_norm_rope(proj(ATTN_W, ATTN_W), knw_ref[...]).astype(BF16)
    av_ref[...] = proj(2 * ATTN_W, ATTN_W).astype(BF16)
    base = 3 * ATTN_W
    mq_ref[...] = proj(base, MLSTM_W).astype(BF16)
    mk_ref[...] = proj(base + MLSTM_W, MLSTM_W).astype(BF16)
    mv_ref[...] = proj(base + 2 * MLSTM_W, MLSTM_W).astype(BF16)
    mo_ref[...] = proj(base + 3 * MLSTM_W, MLSTM_W).astype(BF16)
    base = base + 4 * MLSTM_W
    d = x.shape[1]
    ga_ref[...] = proj(base, d).astype(BF16)
    gm_ref[...] = proj(base + d, d).astype(BF16)
    gt_ref[...] = proj(base + 2 * d, LANES)


def _inproj(x2d, mix_norm_w, w_cat, cos_t, sin_t, bd, qnw, knw, seq):
    t, d = x2d.shape
    tm = INPROJ_TM
    assert t % tm == 0 and seq % tm == 0
    tiles_per_seq = seq // tm
    ncols = w_cat.shape[1]
    row = lambda w: pl.BlockSpec((tm, w), lambda i: (i, 0))
    const = lambda shape: pl.BlockSpec(shape, lambda i: (0, 0))
    pos = pl.BlockSpec((tm, ATTN_W), lambda i: (i % tiles_per_seq, 0))
    outs = [ATTN_W] * 3 + [MLSTM_W] * 4 + [d, d]
    return pl.pallas_call(
        _inproj_kernel,
        grid=(t // tm,),
        in_specs=[row(d), const((1, d)), const((d, ncols)), pos, pos, const((ATTN_W, ATTN_W)),
                  const((1, ATTN_W)), const((1, ATTN_W))],
        out_specs=[row(w) for w in outs] + [row(LANES)],
        out_shape=[jax.ShapeDtypeStruct((t, w), BF16) for w in outs]
        + [jax.ShapeDtypeStruct((t, LANES), F32)],
        compiler_params=_params(("parallel",)),
        name="inproj",
    )(x2d, mix_norm_w, w_cat, cos_t, sin_t, bd, qnw, knw)


def _moba_kernel(q_ref, k_ref, v_ref, o_ref, kmean_ref, kext_ref, vext_ref, s_ref, *, n_blocks):
    i = pl.program_id(2)
    blk = MOBA_BLOCK
    hd = ATTN_HEAD_DIM
    heads = (0, 1)
    lane = lax.broadcasted_iota(jnp.int32, (blk, LANES), 1)
    in_head = [lane < hd, lane >= hd]
    spare0 = [hd, 0]

    @pl.when(i == 0)
    def _():
        kmean_ref[...] = jnp.zeros_like(kmean_ref)
        for n in range(n_blocks):
            rows = slice(n * blk, (n + 1) * blk)
            kb = k_ref[rows, :]
            vb = v_ref[rows, :]
            kmean_ref[n:n + 1, :] = jnp.mean(kb.astype(F32), axis=0, keepdims=True)
            for h in heads:
                onehot = jnp.where(lane == spare0[h] + n, 1.0, 0.0).astype(BF16)
                kext_ref[h, rows, :] = jnp.where(in_head[h], kb, onehot)
                vext_ref[h, rows, :] = jnp.where(in_head[h], vb, jnp.ones_like(vb))

    q = q_ref[...]
    kmean_hi, kmean_lo = _split2(kmean_ref[...])
    brow = lax.broadcasted_iota(jnp.int32, (16, blk), 0)
    qext = []
    for h in heads:
        qh = jnp.where(in_head[h], q, jnp.zeros_like(q)) * jnp.asarray(hd ** -0.5, BF16)
        g_t = (_dot_nt(kmean_hi, qh) + _dot_nt(kmean_lo, qh))[0:16, :]
        rank = jnp.zeros((16, blk), F32)
        for m in range(min(n_blocks, 16)):
            gm = g_t[m:m + 1, :]
            beats = jnp.where(gm > g_t, 1.0, jnp.where(gm == g_t, jnp.where(brow > m, 1.0, 0.0), 0.0))
            rank = rank + jnp.where(m < i, beats, 0.0)
        bias_t = jnp.where(brow < i, jnp.where(rank < MOBA_TOPK, 0.0, NEG), 0.0)
        bias = jnp.concatenate([bias_t, jnp.zeros((LANES - 16, blk), F32)], axis=0).T
        if spare0[h]:
            bias = pltpu.roll(bias, spare0[h], axis=1)
        qext.append(qh + bias.astype(BF16))

    own = pl.multiple_of(i * blk, blk)
    causal = (lax.broadcasted_iota(jnp.int32, (blk, blk), 1)
              <= lax.broadcasted_iota(jnp.int32, (blk, blk), 0))
    def scores(t, m_run):
        m_new = list(m_run)
        for u in range(2):
            n = 2 * t + u
            start = pl.multiple_of(n * blk, blk)
            for h in heads:
                s = _dot_nt(qext[h], kext_ref[h, pl.ds(start, blk), :])
                s_ref[h, n] = s
                m_cand = jnp.maximum(m_new[h], jnp.max(s, axis=1, keepdims=True))
                m_new[h] = m_cand if u == 0 else jnp.where(n < i, m_cand, m_new[h])
        return tuple(m_new)

    m_past = lax.fori_loop(0, (i + 1) // 2, scores, tuple(jnp.full((blk, 1), NEG, F32) for _ in heads))
    m_fin = []
    for h in heads:
        s = jnp.where(causal, _dot_nt(qext[h], kext_ref[h, pl.ds(own, blk), :]), NEG)
        s_ref[h, i] = s
        m_fin.append(jnp.maximum(m_past[h], jnp.max(s, axis=1, keepdims=True)))

    def values(t, acc):
        out = list(acc)
        for u in range(2):
            n = 2 * t + u
            n_read = jnp.minimum(n, i)
            start = pl.multiple_of(n_read * blk, blk)
            for h in heads:
                m_eff = m_fin[h] if u == 0 else jnp.where(n <= i, m_fin[h], -NEG)
                p = jnp.exp(s_ref[h, n_read] - m_eff).astype(BF16)
                out[h] = out[h] + _dot(p, vext_ref[h, pl.ds(start, blk), :])
        return tuple(out)

    acc = lax.fori_loop(0, (i + 2) // 2, values, tuple(jnp.zeros((blk, LANES), F32) for _ in heads))
    outs = [a / pltpu.roll(a, hd, axis=1) for a in acc]
    o_ref[...] = jnp.where(in_head[0], outs[0], outs[1]).astype(o_ref.dtype)


def _moba(q, k, v):
    b, s, w = q.shape
    assert s % MOBA_BLOCK == 0 and w % LANES == 0
    n_blocks = s // MOBA_BLOCK
    assert n_blocks <= 16
    qspec = pl.BlockSpec((None, MOBA_BLOCK, LANES), lambda bi, hp, i: (bi, i, hp))
    kvspec = pl.BlockSpec((None, s, LANES), lambda bi, hp, i: (bi, 0, hp))
    return pl.pallas_call(
        functools.partial(_moba_kernel, n_blocks=n_blocks),
        grid=(b, w // LANES, n_blocks),
        in_specs=[qspec, kvspec, kvspec],
        out_specs=qspec,
        out_shape=jax.ShapeDtypeStruct((b, s, w), BF16),
        scratch_shapes=[pltpu.VMEM((LANES, LANES), F32),
                        pltpu.VMEM((2, s, LANES), BF16), pltpu.VMEM((2, s, LANES), BF16),
                        pltpu.VMEM((2, n_blocks, MOBA_BLOCK, MOBA_BLOCK), F32)],
        compiler_params=_params(("parallel", "parallel", "arbitrary")),
        name="moba",
    )(q, k, v)


def _log_sigmoid(f):
    return jnp.minimum(f, 0.0) - jnp.log(1.0 + jnp.exp(-jnp.abs(f)))


def _mlstm_kernel(gb_ref, mq_ref, mk_ref, mv_ref, mo_ref, gc_ref, gr_ref, cwq_ref, cwk_ref,
                  cbq_ref, cbk_ref, nw_ref, o_ref, c_ref, n_ref, m_ref):
    hh = pl.program_id(1)
    c = pl.program_id(2)
    L = MLSTM_CHUNK
    d = MLSTM_HEAD_DIM

    @pl.when(c == 0)
    def _():
        c_ref[...] = jnp.zeros_like(c_ref)
        n_ref[...] = jnp.zeros_like(n_ref)
        m_ref[...] = jnp.zeros_like(m_ref)

    start = pl.multiple_of(c * L, L)
    prev = pl.multiple_of(jnp.maximum(c - 1, 0) * L, L)
    row_d = lax.broadcasted_iota(jnp.int32, (L, d), 0)

    def conv_silu(u_ref, w_ref, b_ref):
        u = u_ref[pl.ds(start, L), :].astype(F32)
        pv = u_ref[pl.ds(prev, L), :].astype(F32)
        pv = jnp.where(c > 0, pv, 0.0)
        y = u * w_ref[CONV_WIDTH - 1:CONV_WIDTH, :] + b_ref[...]
        for k in range(1, CONV_WIDTH):
            shifted = jnp.where(row_d < k, pltpu.roll(pv, k, axis=0), pltpu.roll(u, k, axis=0))
            y = y + shifted * w_ref[CONV_WIDTH - 1 - k:CONV_WIDTH - k, :]
        return y * jax.nn.sigmoid(y)

    qc = conv_silu(mq_ref, cwq_ref, cbq_ref)
    kc = conv_silu(mk_ref, cwk_ref, cbk_ref) * (d ** -0.5)
    qb = qc.astype(BF16)
    kb = kc.astype(BF16)
    vb = mv_ref[...]

    ib = gb_ref[0, hh]
    fb = gb_ref[1, hh]
    gcol = gc_ref[...]
    grow = gr_ref[...]
    ig_col = gcol[:, 0:1] + ib
    ig_row = grow[0:1, :] + ib
    lf_col = _log_sigmoid(gcol[:, 1:2] + fb)
    lf_row = _log_sigmoid(grow[1:2, :] + fb)

    r = lax.broadcasted_iota(jnp.int32, (L, L), 0)
    s = lax.broadcasted_iota(jnp.int32, (L, L), 1)
    causal = s <= r
    tri_l = jnp.where(causal, 1.0, 0.0).astype(BF16)
    tri_u = jnp.where(r <= s, 1.0, 0.0).astype(BF16)
    lfc_hi, lfc_lo = _split2(jnp.broadcast_to(lf_col, (L, L)))
    lfr_hi, lfr_lo = _split2(jnp.broadcast_to(lf_row, (L, L)))
    b_t = _dot(tri_l, lfc_hi) + _dot(tri_l, lfc_lo)
    b_s = _dot(lfr_hi, tri_u) + _dot(lfr_lo, tri_u)

    m_prev = m_ref[...]
    b_col = b_t[:, 0:1]
    log_d = jnp.where(causal, b_t - b_s + ig_row, NEG)
    m_inter = b_col + m_prev
    m_t = jnp.maximum(m_inter, jnp.max(log_d, axis=1, keepdims=True))
    a = jnp.exp(log_d - m_t) * _dot_nt(qb, kb)
    w_inter = jnp.exp(m_inter - m_t)
    c_prev = c_ref[...]
    n_prev = n_ref[...]
    num = _dot(a.astype(BF16), vb) + w_inter * _dot(qb, c_prev.astype(BF16))
    den = jnp.sum(a, axis=1, keepdims=True) + w_inter * jnp.sum(qc * n_prev, axis=1, keepdims=True)
    h = num / jnp.maximum(jnp.abs(den), jnp.exp(-m_t))

    m_new = m_t[L - 1:L, :]
    b_last = b_col[L - 1:L, :]
    ws_col = jnp.exp(b_last - b_col + ig_col - m_new)
    decay = jnp.exp(b_last + m_prev - m_new)
    c_ref[...] = decay * c_prev + _dot_tn(kb, (ws_col * vb.astype(F32)).astype(BF16))
    n_ref[...] = decay * n_prev + jnp.sum(ws_col * kc, axis=0, keepdims=True)
    m_ref[...] = m_new

    hn = h * lax.rsqrt(jnp.mean(h * h, axis=-1, keepdims=True) + EPS) * nw_ref[...]
    o_ref[...] = (jax.nn.sigmoid(mo_ref[...].astype(F32)) * hn).astype(o_ref.dtype)


def _mlstm(mq, mk, mv, mo, g_col, g_row, gate_b, conv_wq, conv_wk, conv_bq, conv_bk, norm_w):
    b, s, w = mq.shape
    L = MLSTM_CHUNK
    d = MLSTM_HEAD_DIM
    assert s % L == 0
    seq = pl.BlockSpec((None, s, d), lambda bi, h, c: (bi, 0, h))
    chunk = pl.BlockSpec((None, L, d), lambda bi, h, c: (bi, c, h))
    per_head = lambda rows: pl.BlockSpec((rows, d), lambda bi, h, c: (0, h))
    return pl.pallas_call(
        _mlstm_kernel,
        grid=(b, w // d, s // L),
        in_specs=[pl.BlockSpec(memory_space=pltpu.SMEM), seq, seq, chunk, chunk,
                  pl.BlockSpec((None, None, L, 2), lambda bi, h, c: (bi, h, c, 0)),
                  pl.BlockSpec((None, None, 2, L), lambda bi, h, c: (bi, h, 0, c)),
                  per_head(CONV_WIDTH), per_head(CONV_WIDTH), per_head(1), per_head(1), per_head(1)],
        out_specs=chunk,
        out_shape=jax.ShapeDtypeStruct((b, s, w), BF16),
        scratch_shapes=[pltpu.VMEM((d, d), F32), pltpu.VMEM((1, d), F32), pltpu.VMEM((1, 1), F32)],
        compiler_params=_params(("parallel", "parallel", "arbitrary")),
        name="mlstm",
    )(gate_b, mq, mk, mv, mo, g_col, g_row, conv_wq, conv_wk, conv_bq, conv_bk, norm_w)


def _merge_kernel(x_ref, attn_ref, hm_ref, ga_ref, gm_ref, wab_ref, wmb_ref, mb_ref, wout_ref,
                  fnw_ref, wq_ref, x2_ref, h2_ref, q_ref):
    ya = _dot(attn_ref[...], wab_ref[...])
    ym = _dot(hm_ref[...], wmb_ref[...])
    y = (jax.nn.sigmoid(ga_ref[...].astype(F32) + mb_ref[0:1, :]) * ya
         + jax.nn.sigmoid(gm_ref[...].astype(F32) + mb_ref[1:2, :]) * ym)
    x2 = x_ref[...] + _dot(y.astype(BF16), wout_ref[...])
    x2_ref[...] = x2
    h2 = x2 * lax.rsqrt(jnp.mean(x2 * x2, axis=-1, keepdims=True) + EPS) * fnw_ref[...]
    h2b = h2.astype(BF16)
    h2_ref[...] = pltpu.bitcast(h2b, jnp.int32)
    q_ref[...] = _dot(h2b, wq_ref[...])


def _merge(x2d, attn, hm, ga, gm, wab, wmb, merge_b, wout, fnw, wq):
    t, d = x2d.shape
    tm = MERGE_TM
    nq = wq.shape[1]
    row = lambda w: pl.BlockSpec((tm, w), lambda i: (i, 0))
    const = lambda a: pl.BlockSpec(a.shape, lambda i: (0, 0))
    return pl.pallas_call(
        _merge_kernel,
        grid=(t // tm,),
        in_specs=[row(d), row(ATTN_W), row(MLSTM_W), row(d), row(d), const(wab), const(wmb),
                  const(merge_b), const(wout), const(fnw), const(wq)],
        out_specs=[row(d), pl.BlockSpec((tm // 2, d), lambda i: (i, 0)), row(nq)],
        out_shape=[jax.ShapeDtypeStruct((t, d), F32), jax.ShapeDtypeStruct((t // 2, d), jnp.int32),
                   jax.ShapeDtypeStruct((t, nq), F32)],
        compiler_params=_params(("parallel",)),
        name="merge",
    )(x2d, attn, hm, ga, gm, wab, wmb, merge_b, wout, fnw, wq)


_CAND_ROWS = [16] + [8] * (PEER_TOPK - 1)


def _select_kernel(q1_ref, q2_ref, k1_ref, k2_ref, e1_ref, na_ref, e2_ref, r2_ref,
                   s1_ref, s2_ref, w1_ref, w2_ref, c_ref):
    tn = q1_ref.shape[0]
    nk = PEER_N_KEYS
    topk = PEER_TOPK
    n_lt = tn // LANES
    lane_tiles = [slice(lt * LANES, (lt + 1) * LANES) for lt in range(n_lt)]

    def scores(k_ref, q_ref):
        kh, kl = _split2(k_ref[...])
        qh, ql = _split2(q_ref[...])
        return _dot_nt(kh, qh) + _dot_nt(kh, ql) + _dot_nt(kl, qh)

    for s_ref, w_ref, k_ref, q_ref in ((s1_ref, w1_ref, k1_ref, q1_ref), (s2_ref, w2_ref, k2_ref, q2_ref)):
        sc = scores(k_ref, q_ref)
        s_ref[...] = sc
        w_ref[...] = sc

    key = lax.broadcasted_iota(jnp.int32, (nk, LANES), 0).astype(F32)

    vals = [[[] for _ in lane_tiles] for _ in range(2)]
    idxs = [[[] for _ in lane_tiles] for _ in range(2)]
    for _ in range(topk):
        for half, w_ref in enumerate((w1_ref, w2_ref)):
            for lt, cols in enumerate(lane_tiles):
                s_cur = w_ref[:, cols]
                m = jnp.max(s_cur, axis=0, keepdims=True)
                idx = jnp.min(jnp.where(s_cur == m, key, float(nk)), axis=0, keepdims=True)
                w_ref[:, cols] = jnp.where(key == idx, -jnp.inf, s_cur)
                vals[half][lt].append(m)
                idxs[half][lt].append(idx)

    row16 = lax.broadcasted_iota(jnp.int32, (topk, LANES), 0)
    flats, valid = [], []
    for i, rows in enumerate(_CAND_ROWS):
        j = lax.broadcasted_iota(jnp.int32, (rows, LANES), 0)
        flats.append((j + i * topk).astype(F32))
        valid.append((i + 1) * (j + 1) <= topk)
    flat = jnp.concatenate(flats, axis=0)
    offs = [sum(_CAND_ROWS[:i]) for i in range(topk)]
    v2_dense = []
    for lt, cols in enumerate(lane_tiles):
        v2d = jnp.zeros((topk, LANES), F32)
        for j in range(topk):
            v2d = jnp.where(row16 == j, vals[1][lt][j], v2d)
        v2_dense.append(v2d)
        for i, rows in enumerate(_CAND_ROWS):
            c_ref[offs[i]:offs[i] + rows, cols] = jnp.where(valid[i], vals[0][lt][i] + v2d[0:rows, :], -jnp.inf)

    for _ in range(topk):
        for lt, cols in enumerate(lane_tiles):
            c_cur = c_ref[:, cols]
            m = jnp.max(c_cur, axis=0, keepdims=True)
            idx = jnp.min(jnp.where(c_cur == m, flat, 1e9), axis=0, keepdims=True)
            c_ref[:, cols] = jnp.where(flat == idx, -jnp.inf, c_cur)

    for lt, cols in enumerate(lane_tiles):
        v1, v2d = vals[0][lt], v2_dense[lt]
        ev2 = jnp.exp(v2d - v2d[0:1, :])
        z = jnp.zeros((1, LANES), F32)
        counts = []
        for i, rows in enumerate(_CAND_ROWS):
            ch = jnp.where(valid[i], jnp.where(c_ref[offs[i]:offs[i] + rows, cols] == -jnp.inf, 1.0, 0.0), 0.0)
            counts.append(jnp.sum(ch, axis=0, keepdims=True))
            ev1_i = jnp.exp(v1[i] - v1[0])
            z = z + jnp.sum(ch * (ev1_i * ev2[0:rows, :]), axis=0, keepdims=True)

        n_a = jnp.zeros((nk, LANES), F32)
        r2 = jnp.full((nk, LANES), float(topk), F32)
        for i in range(topk):
            n_a = jnp.where(key == idxs[0][lt][i], counts[i], n_a)
            r2 = jnp.where(key == idxs[1][lt][i], float(i), r2)
        in1 = w1_ref[:, cols] == -jnp.inf
        in2 = w2_ref[:, cols] == -jnp.inf
        e1 = jnp.where(in1, jnp.exp(s1_ref[:, cols] - v1[0]) / z, 0.0)
        e2 = jnp.where(in2, jnp.exp(s2_ref[:, cols] - v2d[0:1, :]), 0.0)
        e1_ref[:, cols] = _pack_pairs(e1, e1)
        na_ref[:, cols] = _pack_pairs(n_a, n_a)
        e2_ref[:, cols] = _pack_pairs(e2[:nk // 2], e2[nk // 2:])
        r2_ref[:, cols] = _pack_pairs(r2[:nk // 2], r2[nk // 2:])


def _select(q, keys_1, keys_2):
    t = q.shape[0]
    tn = SELECT_TN
    assert t % tn == 0
    nk = PEER_N_KEYS
    q1 = pl.BlockSpec((tn, PEER_HALF), lambda i, h: (i, 2 * h))
    q2 = pl.BlockSpec((tn, PEER_HALF), lambda i, h: (i, 2 * h + 1))
    kspec = pl.BlockSpec((None, nk, PEER_HALF), lambda i, h: (h, 0, 0))
    ospec = lambda rows: pl.BlockSpec((None, rows, tn), lambda i, h: (h, 0, i))
    oshape = lambda rows: jax.ShapeDtypeStruct((PEER_HEADS, rows, t), jnp.int32)
    keys_buf = pltpu.VMEM((nk, tn), F32)
    return pl.pallas_call(
        _select_kernel,
        grid=(t // tn, PEER_HEADS),
        in_specs=[q1, q2, kspec, kspec],
        out_specs=[ospec(nk), ospec(nk), ospec(nk // 2), ospec(nk // 2)],
        out_shape=[oshape(nk), oshape(nk), oshape(nk // 2), oshape(nk // 2)],
        scratch_shapes=[keys_buf, keys_buf, keys_buf, keys_buf, pltpu.VMEM((sum(_CAND_ROWS), tn), F32)],
        compiler_params=_params(("parallel", "parallel")),
        name="peer_select",
    )(q, q, keys_1, keys_2)


def _experts_kernel(h2_ref, wd_ref, wut_ref, e1_in, na_in, e2_in, r2_in, x2_ref, o_ref,
                    acc_ref, z0_ref, z1_ref, act0_ref, act1_ref, e1_ref, na_ref, e2_ref, r2_ref):
    s = pl.program_id(1)
    n_chunks = pl.num_programs(1) - 2
    nk = PEER_N_KEYS
    tn = z0_ref.shape[1]
    rows_per_chunk = EXPERT_CHUNK // nk

    @pl.when(s == 0)
    def _():
        for ref in (acc_ref, z0_ref, z1_ref, act0_ref, act1_ref):
            ref[...] = jnp.zeros_like(ref)
        e2_ref[...] = e2_in[...]
        r2_ref[...] = r2_in[...]

    chunk = jnp.clip(s - 1, 0, n_chunks - 1)
    r0 = (chunk % (SUBLANES // rows_per_chunk)) * rows_per_chunk
    for h in range(PEER_HEADS):
        for aa in range(rows_per_chunk):
            e1_ref[h, aa:aa + 1, :] = e1_in[h, pl.ds(r0 + aa, 1), :]
            na_ref[h, aa:aa + 1, :] = na_in[h, pl.ds(r0 + aa, 1), :]

    half = nk // 2
    groups = half // SUBLANES

    def gate_piece(lt, k_lo, k_hi, z_r, act_w):
        cols = slice(lt * LANES, (lt + 1) * LANES)
        g = [[jnp.zeros((2 * SUBLANES, LANES), BF16) for _ in range(k_lo, k_hi)]
             for _ in range(rows_per_chunk)]
        for h in range(PEER_HEADS):
            as_pair = lambda ref, aa: _as_bf16_pairs(
                jnp.broadcast_to(ref[h, aa:aa + 1, cols], (SUBLANES, LANES)))
            e1 = [as_pair(e1_ref, aa) for aa in range(rows_per_chunk)]
            n_a = [as_pair(na_ref, aa) for aa in range(rows_per_chunk)]
            for k in range(k_lo, k_hi):
                grp = slice(k * SUBLANES, (k + 1) * SUBLANES)
                r2 = _as_bf16_pairs(r2_ref[h, grp, cols])
                e2 = _as_bf16_pairs(e2_ref[h, grp, cols])
                for aa in range(rows_per_chunk):
                    g[aa][k - k_lo] = (g[aa][k - k_lo]
                                       + jnp.where(r2 < n_a[aa], e2, jnp.zeros_like(e2)) * e1[aa])
        for aa in range(rows_per_chunk):
            for k in range(k_lo, k_hi):
                g_lo, g_hi = _unpack_pairs(g[aa][k - k_lo])
                for g_half, base in ((g_lo, aa * nk), (g_hi, aa * nk + half)):
                    rows = slice(base + k * SUBLANES, base + (k + 1) * SUBLANES)
                    z = z_r[rows, cols]
                    act = 0.5 * z * (1.0 + lax.erf(z * (2.0 ** -0.5)))
                    act_w[rows, cols] = (act * g_half).astype(BF16)

    def step(z_w, z_r, act_w, act_r):
        d = acc_ref.shape[0]
        mxu = []
        for p in range(MXU_PIECES):
            pc = slice(p * tn // MXU_PIECES, (p + 1) * tn // MXU_PIECES)
            pw = slice(pc.start // 2, pc.stop // 2)

            def z_piece(pc=pc, pw=pw):
                z_w[:, pc] = _dot_nt(_rows_bf16(wd_ref[...]), _rows_bf16(h2_ref[pw, :]))

            mxu.append(z_piece)
            for r in range(UP_ROW_PIECES):
                rs = slice(r * d // UP_ROW_PIECES, (r + 1) * d // UP_ROW_PIECES)
                rw = slice(rs.start // 2, rs.stop // 2)

                def up_piece(pc=pc, rs=rs, rw=rw):
                    acc_ref[rs, pc] += _dot(_rows_bf16(wut_ref[rw, :]), act_r[:, pc])

                mxu.append(up_piece)
        gates = [(lt, k, k + GATE_GROUPS) for lt in range(tn // LANES)
                 for k in range(0, groups, GATE_GROUPS)]
        per_mxu = -(-len(gates) // len(mxu))
        for m, piece in enumerate(mxu):
            piece()
            for lt, k_lo, k_hi in gates[m * per_mxu:(m + 1) * per_mxu]:
                gate_piece(lt, k_lo, k_hi, z_r, act_w)
        for lt, k_lo, k_hi in gates[len(mxu) * per_mxu:]:
            gate_piece(lt, k_lo, k_hi, z_r, act_w)

    @pl.when(s % 2 == 0)
    def _():
        step(z0_ref, z1_ref, act1_ref, act0_ref)

    @pl.when(s % 2 == 1)
    def _():
        step(z1_ref, z0_ref, act0_ref, act1_ref)

    @pl.when(s == pl.num_programs(1) - 1)
    def _():
        o_ref[...] = x2_ref[...] + acc_ref[...].T


def _experts(h2, wd, wut, e1, na, e2, r2, x2):
    t, d = x2.shape
    tn = EXPERT_TN
    ne = 2 * wd.shape[0]
    nk = PEER_N_KEYS
    assert t % tn == 0 and ne % EXPERT_CHUNK == 0 and EXPERT_CHUNK % nk == 0
    n_chunks = ne // EXPERT_CHUNK
    assert n_chunks % 2 == 0
    tok = pl.BlockSpec((tn, d), lambda i, s: (i, 0))
    gate = pl.BlockSpec((PEER_HEADS, nk // 2, tn), lambda i, s: (0, 0, i))
    rpc = EXPERT_CHUNK // nk
    assert SUBLANES % rpc == 0
    chunks_per_group = SUBLANES // rpc
    row = pl.BlockSpec((PEER_HEADS, None, SUBLANES, tn),
                       lambda i, s: (0, jnp.clip(s - 1, 0, n_chunks - 1) // chunks_per_group, 0, i))
    e1 = e1.reshape(PEER_HEADS, nk // SUBLANES, SUBLANES, t)
    na = na.reshape(PEER_HEADS, nk // SUBLANES, SUBLANES, t)
    return pl.pallas_call(
        _experts_kernel,
        grid=(t // tn, n_chunks + 2),
        in_specs=[pl.BlockSpec((tn // 2, d), lambda i, s: (i, 0)),
                  pl.BlockSpec((EXPERT_CHUNK // 2, d), lambda i, s: (jnp.minimum(s, n_chunks - 1), 0)),
                  pl.BlockSpec((None, d // 2, EXPERT_CHUNK),
                               lambda i, s: (jnp.clip(s - 2, 0, n_chunks - 1), 0, 0)),
                  row, row, gate, gate, tok],
        out_specs=tok,
        out_shape=jax.ShapeDtypeStruct((t, d), F32),
        scratch_shapes=[pltpu.VMEM((d, tn), F32),
                        pltpu.VMEM((EXPERT_CHUNK, tn), F32), pltpu.VMEM((EXPERT_CHUNK, tn), F32),
                        pltpu.VMEM((EXPERT_CHUNK, tn), BF16), pltpu.VMEM((EXPERT_CHUNK, tn), BF16),
                        pltpu.VMEM((PEER_HEADS, rpc, tn), jnp.int32), pltpu.VMEM((PEER_HEADS, rpc, tn), jnp.int32),
                        pltpu.VMEM((PEER_HEADS, nk // 2, tn), jnp.int32),
                        pltpu.VMEM((PEER_HEADS, nk // 2, tn), jnp.int32)],
        compiler_params=_params(("parallel", "arbitrary")),
        name="peer_experts",
    )(h2, wd, wut, e1, na, e2, r2, x2)


def _rope_tables(seq):
    pos = jnp.arange(seq, dtype=F32)
    inv_freq = ROPE_THETA ** (-jnp.arange(0, ROT_DIM, 2, dtype=F32) / ROT_DIM)
    ang = pos[:, None] * inv_freq[None, :]
    cos, sin = jnp.cos(ang), jnp.sin(ang)
    rest = ATTN_HEAD_DIM - ROT_DIM
    cos_h = jnp.concatenate([cos, cos, jnp.ones((seq, rest), F32)], axis=-1)
    sin_h = jnp.concatenate([-sin, sin, jnp.zeros((seq, rest), F32)], axis=-1)
    return jnp.tile(cos_h, (1, ATTN_HEADS)), jnp.tile(sin_h, (1, ATTN_HEADS))


def kernel(x, mix_norm_w, w_in, q_norm_w, k_norm_w, conv_w, conv_b, igate_b, fgate_b, mlstm_norm_w,
           w_attn_branch, w_mlstm_branch, merge_b, w_out, ffn_norm_w, peer_w_query, peer_keys_1,
           peer_keys_2, peer_w_down, peer_w_up):
    b, s, d = x.shape
    t = b * s
    x2d = x.reshape(t, d)

    n_main = 3 * ATTN_W + 4 * MLSTM_W
    n_gate = 2 * MLSTM_HEADS
    w_cat = jnp.concatenate(
        [w_in[:, :n_main], w_in[:, n_main + n_gate:], w_in[:, n_main:n_main + n_gate],
         jnp.zeros((d, LANES - n_gate), w_in.dtype)], axis=1).astype(BF16)
    cos_t, sin_t = _rope_tables(s)
    head_of = jnp.arange(ATTN_W) // ATTN_HEAD_DIM
    bd = (head_of[:, None] == head_of[None, :]).astype(BF16)
    qnw = jnp.tile(q_norm_w, ATTN_HEADS).reshape(1, ATTN_W)
    knw = jnp.tile(k_norm_w, ATTN_HEADS).reshape(1, ATTN_W)

    aq, ak, av, mq, mk, mv, mo, ga, gm, gt = _inproj(
        x2d, mix_norm_w.reshape(1, d), w_cat, cos_t, sin_t, bd, qnw, knw, s)

    attn = _moba(aq.reshape(b, s, ATTN_W), ak.reshape(b, s, ATTN_W), av.reshape(b, s, ATTN_W))

    gates = gt[:, :n_gate].reshape(b, s, 2, MLSTM_HEADS)
    g_col = gates.transpose(0, 3, 1, 2)
    g_row = gates.transpose(0, 3, 2, 1)
    gate_b = jnp.stack([igate_b, fgate_b]).astype(F32)
    r3 = lambda a: a.reshape(b, s, MLSTM_W)
    hm = _mlstm(r3(mq), r3(mk), r3(mv), r3(mo), g_col, g_row, gate_b,
                conv_w[:, :MLSTM_W], conv_w[:, MLSTM_W:], conv_b[:MLSTM_W].reshape(1, MLSTM_W),
                conv_b[MLSTM_W:].reshape(1, MLSTM_W), mlstm_norm_w.reshape(1, MLSTM_W))

    x2, h2, q = _merge(x2d, attn.reshape(t, ATTN_W), hm.reshape(t, MLSTM_W), ga, gm,
                       w_attn_branch.astype(BF16), w_mlstm_branch.astype(BF16), merge_b,
                       w_out.astype(BF16), ffn_norm_w.reshape(1, d), peer_w_query.astype(BF16))

    e1, na, e2, r2 = _select(q, peer_keys_1, peer_keys_2)
    wut = peer_w_up.astype(BF16).reshape(-1, EXPERT_CHUNK, d).transpose(0, 2, 1)
    out = _experts(h2, _pack_rows(peer_w_down.astype(BF16)), _pack_rows(wut), e1, na, e2, r2, x2)
    return out.reshape(b, s, d)
```

```python
import functools

import jax
import jax.numpy as jnp
from jax import lax
from jax.experimental import pallas as pl
from jax.experimental.pallas import tpu as pltpu

F32 = jnp.float32
BF16 = jnp.bfloat16

EPS = 1e-6
ATTN_HEADS = 8
ATTN_HEAD_DIM = 64
ATTN_W = ATTN_HEADS * ATTN_HEAD_DIM
ROT_DIM = ATTN_HEAD_DIM // 4
ROPE_THETA = 500000.0
MOBA_BLOCK = 256
MOBA_TOPK = 3
MLSTM_HEADS = 4
MLSTM_HEAD_DIM = 128
MLSTM_W = MLSTM_HEADS * MLSTM_HEAD_DIM
CONV_WIDTH = 4
PEER_HEADS = 8
PEER_N_KEYS = 128
PEER_HALF = 128
PEER_TOPK = 16

LANES = 128
SUBLANES = 8
NEG = -1e30
VMEM_LIMIT = 56 * 1024 * 1024

INPROJ_TM = 512
MERGE_TM = 512
MLSTM_CHUNK = 256
SELECT_TN = 512
EXPERT_TN = 512
EXPERT_CHUNK = 512
MXU_PIECES = 2
UP_ROW_PIECES = 1
GATE_GROUPS = 8


def _dot(a, b):
    return jnp.dot(a, b, preferred_element_type=F32)


def _dot_nt(a, b):
    return lax.dot_general(a, b, (((1,), (1,)), ((), ())), preferred_element_type=F32)


def _dot_tn(a, b):
    return lax.dot_general(a, b, (((0,), (0,)), ((), ())), preferred_element_type=F32)


def _bf16_bits(x):
    return lax.bitcast_convert_type(x.astype(BF16).astype(F32), jnp.int32)


def _pack_pairs(lo, hi):
    return lax.shift_right_logical(_bf16_bits(lo), jnp.int32(16)) | _bf16_bits(hi)


def _as_bf16_pairs(words):
    return pltpu.bitcast(words, BF16)


def _unpack_pairs(pairs):
    words = pltpu.bitcast(pairs, jnp.int32)
    lo = lax.bitcast_convert_type(lax.shift_left(words, jnp.int32(16)), F32)
    hi = lax.bitcast_convert_type(words & jnp.int32(-65536), F32)
    return lo, hi


def _pack_rows(x):
    *lead, m, n = x.shape
    pairs = jnp.swapaxes(x.reshape(*lead, m // 2, 2, n), -1, -2)
    return lax.bitcast_convert_type(pairs, jnp.int32)


def _rows_bf16(words):
    return pltpu.bitcast(words, BF16)


def _split2(x):
    hi = x.astype(BF16)
    lo = (x - hi.astype(F32)).astype(BF16)
    return hi, lo


def _params(sem, flags=None):
    return pltpu.CompilerParams(dimension_semantics=sem, vmem_limit_bytes=VMEM_LIMIT, flags=flags)


def _inproj_kernel(x_ref, nw_ref, w_ref, cos_ref, sin_ref, bd_ref, qnw_ref, knw_ref,
                   aq_ref, ak_ref, av_ref, mq_ref, mk_ref, mv_ref, mo_ref, ga_ref, gm_ref, gt_ref):
    x = x_ref[...]
    h = x * lax.rsqrt(jnp.mean(x * x, axis=-1, keepdims=True) + EPS) * nw_ref[...]
    hb = h.astype(BF16)

    def proj(lo, width):
        return _dot(hb, w_ref[:, lo:lo + width])

    lane = lax.broadcasted_iota(jnp.int32, (x.shape[0], ATTN_W), 1) % ATTN_HEAD_DIM
    half = ROT_DIM // 2

    def qk_norm_rope(t, nw):
        hi, lo = _split2(t * t)
        ms = (_dot(hi, bd_ref[...]) + _dot(lo, bd_ref[...])) * (1.0 / ATTN_HEAD_DIM)
        tn = t * lax.rsqrt(ms + EPS) * nw
        fwd = pltpu.roll(tn, ATTN_W - half, axis=1)
        bwd = pltpu.roll(tn, half, axis=1)
        swapped = jnp.where(lane < half, fwd, jnp.where(lane < ROT_DIM, bwd, 0.0))
        return tn * cos_ref[...] + swapped * sin_ref[...]

    aq_ref[...] = qk_norm_rope(proj(0, ATTN_W), qnw_ref[...]).astype(BF16)
    ak_ref[...] = qk_norm_rope(proj(ATTN_W, ATTN_W), knw_ref[...]).astype(BF16)
    av_ref[...] = proj(2 * ATTN_W, ATTN_W).astype(BF16)
    base = 3 * ATTN_W
    mq_ref[...] = proj(base, MLSTM_W).astype(BF16)
    mk_ref[...] = proj(base + MLSTM_W, MLSTM_W).astype(BF16)
    mv_ref[...] = proj(base + 2 * MLSTM_W, MLSTM_W).astype(BF16)
    mo_ref[...] = proj(base + 3 * MLSTM_W, MLSTM_W).astype(BF16)
    base = base + 4 * MLSTM_W
    d = x.shape[1]
    ga_ref[...] = proj(base, d).astype(BF16)
    gm_ref[...] = proj(base + d, d).astype(BF16)
    gt_ref[...] = proj(base + 2 * d, LANES)


def _inproj(x2d, mix_norm_w, w_cat, cos_t, sin_t, bd, qnw, knw, seq):
    t, d = x2d.shape
    tm = INPROJ_TM
    assert t % tm == 0 and seq % tm == 0
    tiles_per_seq = seq // tm
    ncols = w_cat.shape[1]
    row = lambda w: pl.BlockSpec((tm, w), lambda i: (i, 0))
    const = lambda shape: pl.BlockSpec(shape, lambda i: (0, 0))
    pos = pl.BlockSpec((tm, ATTN_W), lambda i: (i % tiles_per_seq, 0))
    outs = [ATTN_W] * 3 + [MLSTM_W] * 4 + [d, d]
    return pl.pallas_call(
        _inproj_kernel,
        grid=(t // tm,),
        in_specs=[row(d), const((1, d)), const((d, ncols)), pos, pos, const((ATTN_W, ATTN_W)),
                  const((1, ATTN_W)), const((1, ATTN_W))],
        out_specs=[row(w) for w in outs] + [row(LANES)],
        out_shape=[jax.ShapeDtypeStruct((t, w), BF16) for w in outs]
        + [jax.ShapeDtypeStruct((t, LANES), F32)],
        compiler_params=_params(("parallel",)),
        name="inproj",
    )(x2d, mix_norm_w, w_cat, cos_t, sin_t, bd, qnw, knw)


def _moba_kernel(q_ref, k_ref, v_ref, o_ref, kmean_ref, kext_ref, vext_ref, s_ref, *, n_blocks):
    i = pl.program_id(2)
    blk = MOBA_BLOCK
    hd = ATTN_HEAD_DIM
    heads = (0, 1)
    lane = lax.broadcasted_iota(jnp.int32, (blk, LANES), 1)
    in_head = [lane < hd, lane >= hd]
    spare0 = [hd, 0]

    @pl.when(i == 0)
    def _():
        kmean_ref[...] = jnp.zeros_like(kmean_ref)
        for n in range(n_blocks):
            rows = slice(n * blk, (n + 1) * blk)
            kb = k_ref[rows, :]
            vb = v_ref[rows, :]
            kmean_ref[n:n + 1, :] = jnp.mean(kb.astype(F32), axis=0, keepdims=True)
            for h in heads:
                onehot = jnp.where(lane == spare0[h] + n, 1.0, 0.0).astype(BF16)
                kext_ref[h, rows, :] = jnp.where(in_head[h], kb, onehot)
                vext_ref[h, rows, :] = jnp.where(in_head[h], vb, jnp.ones_like(vb))

    q = q_ref[...]
    kmean_hi, kmean_lo = _split2(kmean_ref[...])
    brow = lax.broadcasted_iota(jnp.int32, (16, blk), 0)
    qext = []
    for h in heads:
        qh = jnp.where(in_head[h], q, jnp.zeros_like(q)) * jnp.asarray(hd ** -0.5, BF16)
        g_t = (_dot_nt(kmean_hi, qh) + _dot_nt(kmean_lo, qh))[0:16, :]
        rank = jnp.zeros((16, blk), F32)
        for m in range(min(n_blocks, 16)):
            gm = g_t[m:m + 1, :]
            beats = jnp.where(gm > g_t, 1.0, jnp.where(gm == g_t, jnp.where(brow > m, 1.0, 0.0), 0.0))
            rank = rank + jnp.where(m < i, beats, 0.0)
        bias_t = jnp.where(brow < i, jnp.where(rank < MOBA_TOPK, 0.0, NEG), 0.0)
        bias = jnp.concatenate([bias_t, jnp.zeros((LANES - 16, blk), F32)], axis=0).T
        if spare0[h]:
            bias = pltpu.roll(bias, spare0[h], axis=1)
        qext.append(qh + bias.astype(BF16))

    own = pl.multiple_of(i * blk, blk)
    causal = (lax.broadcasted_iota(jnp.int32, (blk, blk), 1)
              <= lax.broadcasted_iota(jnp.int32, (blk, blk), 0))
    def scores(t, m_run):
        m_new = list(m_run)
        for u in range(2):
            n = 2 * t + u
            start = pl.multiple_of(n * blk, blk)
            for h in heads:
                s = _dot_nt(qext[h], kext_ref[h, pl.ds(start, blk), :])
                s_ref[h, n] = s
                m_cand = jnp.maximum(m_new[h], jnp.max(s, axis=1, keepdims=True))
                m_new[h] = m_cand if u == 0 else jnp.where(n < i, m_cand, m_new[h])
        return tuple(m_new)

    m_past = lax.fori_loop(0, (i + 1) // 2, scores, tuple(jnp.full((blk, 1), NEG, F32) for _ in heads))
    m_fin = []
    for h in heads:
        s = jnp.where(causal, _dot_nt(qext[h], kext_ref[h, pl.ds(own, blk), :]), NEG)
        s_ref[h, i] = s
        m_fin.append(jnp.maximum(m_past[h], jnp.max(s, axis=1, keepdims=True)))

    def values(t, acc):
        out = list(acc)
        for u in range(2):
            n = 2 * t + u
            n_read = jnp.minimum(n, i)
            start = pl.multiple_of(n_read * blk, blk)
            for h in heads:
                m_eff = m_fin[h] if u == 0 else jnp.where(n <= i, m_fin[h], -NEG)
                p = jnp.exp(s_ref[h, n_read] - m_eff).astype(BF16)
                out[h] = out[h] + _dot(p, vext_ref[h, pl.ds(start, blk), :])
        return tuple(out)

    acc = lax.fori_loop(0, (i + 2) // 2, values, tuple(jnp.zeros((blk, LANES), F32) for _ in heads))
    outs = [a / pltpu.roll(a, hd, axis=1) for a in acc]
    o_ref[...] = jnp.where(in_head[0], outs[0], outs[1]).astype(o_ref.dtype)


def _moba(q, k, v):
    b, s, w = q.shape
    assert s % MOBA_BLOCK == 0 and w % LANES == 0
    n_blocks = s // MOBA_BLOCK
    assert n_blocks <= 16
    qspec = pl.BlockSpec((None, MOBA_BLOCK, LANES), lambda bi, hp, i: (bi, i, hp))
    kvspec = pl.BlockSpec((None, s, LANES), lambda bi, hp, i: (bi, 0, hp))
    return pl.pallas_call(
        functools.partial(_moba_kernel, n_blocks=n_blocks),
        grid=(b, w // LANES, n_blocks),
        in_specs=[qspec, kvspec, kvspec],
        out_specs=qspec,
        out_shape=jax.ShapeDtypeStruct((b, s, w), BF16),
        scratch_shapes=[pltpu.VMEM((LANES, LANES), F32),
                        pltpu.VMEM((2, s, LANES), BF16), pltpu.VMEM((2, s, LANES), BF16),
                        pltpu.VMEM((2, n_blocks, MOBA_BLOCK, MOBA_BLOCK), F32)],
        compiler_params=_params(("parallel", "parallel", "arbitrary")),
        name="moba",
    )(q, k, v)


def _log_sigmoid(f):
    return jnp.minimum(f, 0.0) - jnp.log(1.0 + jnp.exp(-jnp.abs(f)))


def _mlstm_kernel(gb_ref, mq_ref, mk_ref, mv_ref, mo_ref, gc_ref, gr_ref, cwq_ref, cwk_ref,
                  cbq_ref, cbk_ref, nw_ref, o_ref, c_ref, n_ref, m_ref):
    hh = pl.program_id(1)
    c = pl.program_id(2)
    L = MLSTM_CHUNK
    d = MLSTM_HEAD_DIM

    @pl.when(c == 0)
    def _():
        c_ref[...] = jnp.zeros_like(c_ref)
        n_ref[...] = jnp.zeros_like(n_ref)
        m_ref[...] = jnp.zeros_like(m_ref)

    start = pl.multiple_of(c * L, L)
    prev = pl.multiple_of(jnp.maximum(c - 1, 0) * L, L)
    row_d = lax.broadcasted_iota(jnp.int32, (L, d), 0)

    def conv_silu(u_ref, w_ref, b_ref):
        u = u_ref[pl.ds(start, L), :].astype(F32)
        pv = u_ref[pl.ds(prev, L), :].astype(F32)
        pv = jnp.where(c > 0, pv, 0.0)
        y = u * w_ref[CONV_WIDTH - 1:CONV_WIDTH, :] + b_ref[...]
        for k in range(1, CONV_WIDTH):
            shifted = jnp.where(row_d < k, pltpu.roll(pv, k, axis=0), pltpu.roll(u, k, axis=0))
            y = y + shifted * w_ref[CONV_WIDTH - 1 - k:CONV_WIDTH - k, :]
        return y * jax.nn.sigmoid(y)

    qc = conv_silu(mq_ref, cwq_ref, cbq_ref)
    kc = conv_silu(mk_ref, cwk_ref, cbk_ref) * (d ** -0.5)
    qb = qc.astype(BF16)
    kb = kc.astype(BF16)
    vb = mv_ref[...]

    ib = gb_ref[0, hh]
    fb = gb_ref[1, hh]
    gcol = gc_ref[...]
    grow = gr_ref[...]
    ig_col = gcol[:, 0:1] + ib
    ig_row = grow[0:1, :] + ib
    lf_col = _log_sigmoid(gcol[:, 1:2] + fb)
    lf_row = _log_sigmoid(grow[1:2, :] + fb)

    r = lax.broadcasted_iota(jnp.int32, (L, L), 0)
    s = lax.broadcasted_iota(jnp.int32, (L, L), 1)
    causal = s <= r
    tri_l = jnp.where(causal, 1.0, 0.0).astype(BF16)
    tri_u = jnp.where(r <= s, 1.0, 0.0).astype(BF16)
    lfc_hi, lfc_lo = _split2(jnp.broadcast_to(lf_col, (L, L)))
    lfr_hi, lfr_lo = _split2(jnp.broadcast_to(lf_row, (L, L)))
    b_t = _dot(tri_l, lfc_hi) + _dot(tri_l, lfc_lo)
    b_s = _dot(lfr_hi, tri_u) + _dot(lfr_lo, tri_u)

    m_prev = m_ref[...]
    b_col = b_t[:, 0:1]
    log_d = jnp.where(causal, b_t - b_s + ig_row, NEG)
    m_inter = b_col + m_prev
    m_t = jnp.maximum(m_inter, jnp.max(log_d, axis=1, keepdims=True))
    a = jnp.exp(log_d - m_t) * _dot_nt(qb, kb)
    w_inter = jnp.exp(m_inter - m_t)
    c_prev = c_ref[...]
    n_prev = n_ref[...]
    num = _dot(a.astype(BF16), vb) + w_inter * _dot(qb, c_prev.astype(BF16))
    den = jnp.sum(a, axis=1, keepdims=True) + w_inter * jnp.sum(qc * n_prev, axis=1, keepdims=True)
    h = num / jnp.maximum(jnp.abs(den), jnp.exp(-m_t))

    m_new = m_t[L - 1:L, :]
    b_last = b_col[L - 1:L, :]
    ws_col = jnp.exp(b_last - b_col + ig_col - m_new)
    decay = jnp.exp(b_last + m_prev - m_new)
    c_ref[...] = decay * c_prev + _dot_tn(kb, (ws_col * vb.astype(F32)).astype(BF16))
    n_ref[...] = decay * n_prev + jnp.sum(ws_col * kc, axis=0, keepdims=True)
    m_ref[...] = m_new

    hn = h * lax.rsqrt(jnp.mean(h * h, axis=-1, keepdims=True) + EPS) * nw_ref[...]
    o_ref[...] = (jax.nn.sigmoid(mo_ref[...].astype(F32)) * hn).astype(o_ref.dtype)


def _mlstm(mq, mk, mv, mo, g_col, g_row, gate_b, conv_wq, conv_wk, conv_bq, conv_bk, norm_w):
    b, s, w = mq.shape
    L = MLSTM_CHUNK
    d = MLSTM_HEAD_DIM
    assert s % L == 0
    seq = pl.BlockSpec((None, s, d), lambda bi, h, c: (bi, 0, h))
    chunk = pl.BlockSpec((None, L, d), lambda bi, h, c: (bi, c, h))
    per_head = lambda rows: pl.BlockSpec((rows, d), lambda bi, h, c: (0, h))
    return pl.pallas_call(
        _mlstm_kernel,
        grid=(b, w // d, s // L),
        in_specs=[pl.BlockSpec(memory_space=pltpu.SMEM), seq, seq, chunk, chunk,
                  pl.BlockSpec((None, None, L, 2), lambda bi, h, c: (bi, h, c, 0)),
                  pl.BlockSpec((None, None, 2, L), lambda bi, h, c: (bi, h, 0, c)),
                  per_head(CONV_WIDTH), per_head(CONV_WIDTH), per_head(1), per_head(1), per_head(1)],
        out_specs=chunk,
        out_shape=jax.ShapeDtypeStruct((b, s, w), BF16),
        scratch_shapes=[pltpu.VMEM((d, d), F32), pltpu.VMEM((1, d), F32), pltpu.VMEM((1, 1), F32)],
        compiler_params=_params(("parallel", "parallel", "arbitrary")),
        name="mlstm",
    )(gate_b, mq, mk, mv, mo, g_col, g_row, conv_wq, conv_wk, conv_bq, conv_bk, norm_w)


def _merge_kernel(x_ref, attn_ref, hm_ref, ga_ref, gm_ref, wab_ref, wmb_ref, mb_ref, wout_ref,
                  fnw_ref, wq_ref, x2_ref, h2_ref, q_ref):
    ya = _dot(attn_ref[...], wab_ref[...])
    ym = _dot(hm_ref[...], wmb_ref[...])
    y = (jax.nn.sigmoid(ga_ref[...].astype(F32) + mb_ref[0:1, :]) * ya
         + jax.nn.sigmoid(gm_ref[...].astype(F32) + mb_ref[1:2, :]) * ym)
    x2 = x_ref[...] + _dot(y.astype(BF16), wout_ref[...])
    x2_ref[...] = x2
    h2 = x2 * lax.rsqrt(jnp.mean(x2 * x2, axis=-1, keepdims=True) + EPS) * fnw_ref[...]
    h2b = h2.astype(BF16)
    h2_ref[...] = pltpu.bitcast(h2b, jnp.int32)
    q_ref[...] = _dot(h2b, wq_ref[...])


def _merge(x2d, attn, hm, ga, gm, wab, wmb, merge_b, wout, fnw, wq):
    t, d = x2d.shape
    tm = MERGE_TM
    nq = wq.shape[1]
    row = lambda w: pl.BlockSpec((tm, w), lambda i: (i, 0))
    const = lambda a: pl.BlockSpec(a.shape, lambda i: (0, 0))
    return pl.pallas_call(
        _merge_kernel,
        grid=(t // tm,),
        in_specs=[row(d), row(ATTN_W), row(MLSTM_W), row(d), row(d), const(wab), const(wmb),
                  const(merge_b), const(wout), const(fnw), const(wq)],
        out_specs=[row(d), pl.BlockSpec((tm // 2, d), lambda i: (i, 0)), row(nq)],
        out_shape=[jax.ShapeDtypeStruct((t, d), F32), jax.ShapeDtypeStruct((t // 2, d), jnp.int32),
                   jax.ShapeDtypeStruct((t, nq), F32)],
        compiler_params=_params(("parallel",)),
        name="merge",
    )(x2d, attn, hm, ga, gm, wab, wmb, merge_b, wout, fnw, wq)


_CAND_ROWS = [16] + [8] * (PEER_TOPK - 1)


def _select_kernel(q1_ref, q2_ref, k1_ref, k2_ref, e1_ref, na_ref, e2_ref, r2_ref,
                   s1_ref, s2_ref, w1_ref, w2_ref, rk1_ref, rk2_ref, v_ref, c_ref):
    tn = q1_ref.shape[0]
    nk = PEER_N_KEYS
    topk = PEER_TOPK
    n_lt = tn // LANES
    lane_tiles = [slice(lt * LANES, (lt + 1) * LANES) for lt in range(n_lt)]
    halves = ((s1_ref, w1_ref, rk1_ref), (s2_ref, w2_ref, rk2_ref))

    def scores(k_ref, q_ref):
        kh, kl = _split2(k_ref[...])
        qh, ql = _split2(q_ref[...])
        return _dot_nt(kh, qh) + _dot_nt(kh, ql) + _dot_nt(kl, qh)

    s1_ref[...] = scores(k1_ref, q1_ref)
    s2_ref[...] = scores(k2_ref, q2_ref)
    key = lax.broadcasted_iota(jnp.int32, (nk, LANES), 0).astype(F32)

    def extract(break_ties):
        for s_ref, w_ref, rk_ref in halves:
            w_ref[...] = s_ref[...]
            rk_ref[...] = jnp.full(rk_ref.shape, float(topk), F32)
        for it in range(topk):
            for half, (_, w_ref, rk_ref) in enumerate(halves):
                for cols in lane_tiles:
                    s_cur = w_ref[:, cols]
                    m = jnp.max(s_cur, axis=0, keepdims=True)
                    if break_ties:
                        hit = key == jnp.min(jnp.where(s_cur == m, key, float(nk)), axis=0, keepdims=True)
                    else:
                        hit = s_cur == m
                    w_ref[:, cols] = jnp.where(hit, -jnp.inf, s_cur)
                    rk_ref[:, cols] = jnp.where(hit, float(it), rk_ref[:, cols])
                    v_ref[half, it, :, cols] = m

    extract(break_ties=False)
    off_count = jnp.zeros((1, LANES), F32)
    for _, _, rk_ref in halves:
        for cols in lane_tiles:
            taken = jnp.sum(jnp.where(rk_ref[:, cols] < float(topk), 1.0, 0.0), axis=0, keepdims=True)
            off_count = off_count + jnp.where(taken == float(topk), 0.0, 1.0)

    @pl.when(jnp.max(off_count) > 0.0)
    def _():
        extract(break_ties=True)

    row16 = lax.broadcasted_iota(jnp.int32, (topk, LANES), 0)
    flats, valid = [], []
    for i, rows in enumerate(_CAND_ROWS):
        j = lax.broadcasted_iota(jnp.int32, (rows, LANES), 0)
        flats.append((j + i * topk).astype(F32))
        valid.append((i + 1) * (j + 1) <= topk)
    flat = jnp.concatenate(flats, axis=0)
    offs = [sum(_CAND_ROWS[:i]) for i in range(topk)]
    v2_dense = []
    for lt, cols in enumerate(lane_tiles):
        v2d = jnp.zeros((topk, LANES), F32)
        for j in range(topk):
            v2d = jnp.where(row16 == j, v_ref[1, j, :, cols], v2d)
        v2_dense.append(v2d)
        for i, rows in enumerate(_CAND_ROWS):
            c_ref[offs[i]:offs[i] + rows, cols] = jnp.where(
                valid[i], v_ref[0, i, :, cols] + v2d[0:rows, :], -jnp.inf)

    for _ in range(topk):
        for lt, cols in enumerate(lane_tiles):
            c_cur = c_ref[:, cols]
            m = jnp.max(c_cur, axis=0, keepdims=True)
            idx = jnp.min(jnp.where(c_cur == m, flat, 1e9), axis=0, keepdims=True)
            c_ref[:, cols] = jnp.where(flat == idx, -jnp.inf, c_cur)

    for lt, cols in enumerate(lane_tiles):
        v1, v2d = [v_ref[0, i, :, cols] for i in range(topk)], v2_dense[lt]
        ev2 = jnp.exp(v2d - v2d[0:1, :])
        z = jnp.zeros((1, LANES), F32)
        counts = []
        for i, rows in enumerate(_CAND_ROWS):
            ch = jnp.where(valid[i], jnp.where(c_ref[offs[i]:offs[i] + rows, cols] == -jnp.inf, 1.0, 0.0), 0.0)
            counts.append(jnp.sum(ch, axis=0, keepdims=True))
            ev1_i = jnp.exp(v1[i] - v1[0])
            z = z + jnp.sum(ch * (ev1_i * ev2[0:rows, :]), axis=0, keepdims=True)

        r1 = rk1_ref[:, cols]
        r2 = rk2_ref[:, cols]
        n_a = jnp.zeros((nk, LANES), F32)
        for i in range(topk):
            n_a = jnp.where(r1 == float(i), counts[i], n_a)
        in1 = r1 < float(topk)
        in2 = r2 < float(topk)
        e1 = jnp.where(in1, jnp.exp(s1_ref[:, cols] - v1[0]) * (0.5 / z), 0.0)
        e2 = jnp.where(in2, jnp.exp(s2_ref[:, cols] - v2d[0:1, :]), 0.0)
        e1_ref[:, cols] = _pack_pairs(e1, e1)
        na_ref[:, cols] = _pack_pairs(n_a, n_a)
        e2_ref[:, cols] = _pack_pairs(e2[:nk // 2], e2[nk // 2:])
        r2_ref[:, cols] = _pack_pairs(r2[:nk // 2], r2[nk // 2:])


def _select(q, keys_1, keys_2):
    t = q.shape[0]
    tn = SELECT_TN
    assert t % tn == 0
    nk = PEER_N_KEYS
    q1 = pl.BlockSpec((tn, PEER_HALF), lambda i, h: (i, 2 * h))
    q2 = pl.BlockSpec((tn, PEER_HALF), lambda i, h: (i, 2 * h + 1))
    kspec = pl.BlockSpec((None, nk, PEER_HALF), lambda i, h: (h, 0, 0))
    ospec = lambda rows: pl.BlockSpec((None, rows, tn), lambda i, h: (h, 0, i))
    oshape = lambda rows: jax.ShapeDtypeStruct((PEER_HEADS, rows, t), jnp.int32)
    keys_buf = pltpu.VMEM((nk, tn), F32)
    return pl.pallas_call(
        _select_kernel,
        grid=(t // tn, PEER_HEADS),
        in_specs=[q1, q2, kspec, kspec],
        out_specs=[ospec(nk), ospec(nk), ospec(nk // 2), ospec(nk // 2)],
        out_shape=[oshape(nk), oshape(nk), oshape(nk // 2), oshape(nk // 2)],
        scratch_shapes=[keys_buf] * 6 + [pltpu.VMEM((2, PEER_TOPK, 1, tn), F32),
                                         pltpu.VMEM((sum(_CAND_ROWS), tn), F32)],
        compiler_params=_params(("parallel", "parallel")),
        name="peer_select",
    )(q, q, keys_1, keys_2)


def _experts_kernel(h2_ref, wd_ref, wut_ref, e1_in, na_in, e2_in, r2_in, x2_ref, o_ref,
                    acc_ref, z0_ref, z1_ref, act0_ref, act1_ref, e1_ref, na_ref, e2_ref, r2_ref):
    s = pl.program_id(1)
    n_chunks = pl.num_programs(1) - 2
    nk = PEER_N_KEYS
    tn = z0_ref.shape[1]
    rows_per_chunk = EXPERT_CHUNK // nk

    @pl.when(s == 0)
    def _():
        for ref in (acc_ref, z0_ref, z1_ref, act0_ref, act1_ref):
            ref[...] = jnp.zeros_like(ref)
        e2_ref[...] = e2_in[...]
        r2_ref[...] = r2_in[...]

    chunk = jnp.clip(s - 1, 0, n_chunks - 1)
    r0 = (chunk % (SUBLANES // rows_per_chunk)) * rows_per_chunk
    for h in range(PEER_HEADS):
        for aa in range(rows_per_chunk):
            e1_ref[h, aa:aa + 1, :] = e1_in[h, pl.ds(r0 + aa, 1), :]
            na_ref[h, aa:aa + 1, :] = na_in[h, pl.ds(r0 + aa, 1), :]

    half = nk // 2
    groups = half // SUBLANES

    def gate_piece(lt, k_lo, k_hi, z_r, act_w):
        cols = slice(lt * LANES, (lt + 1) * LANES)
        g = [[jnp.zeros((2 * SUBLANES, LANES), BF16) for _ in range(k_lo, k_hi)]
             for _ in range(rows_per_chunk)]
        for h in range(PEER_HEADS):
            as_pair = lambda ref, aa: _as_bf16_pairs(
                jnp.broadcast_to(ref[h, aa:aa + 1, cols], (SUBLANES, LANES)))
            e1 = [as_pair(e1_ref, aa) for aa in range(rows_per_chunk)]
            n_a = [as_pair(na_ref, aa) for aa in range(rows_per_chunk)]
            for k in range(k_lo, k_hi):
                grp = slice(k * SUBLANES, (k + 1) * SUBLANES)
                r2 = _as_bf16_pairs(r2_ref[h, grp, cols])
                e2 = _as_bf16_pairs(e2_ref[h, grp, cols])
                for aa in range(rows_per_chunk):
                    g[aa][k - k_lo] = (g[aa][k - k_lo]
                                       + jnp.where(r2 < n_a[aa], e2, jnp.zeros_like(e2)) * e1[aa])
        for aa in range(rows_per_chunk):
            for k in range(k_lo, k_hi):
                g_lo, g_hi = _unpack_pairs(g[aa][k - k_lo])
                for g_half, base in ((g_lo, aa * nk), (g_hi, aa * nk + half)):
                    rows = slice(base + k * SUBLANES, base + (k + 1) * SUBLANES)
                    z = z_r[rows, cols]
                    act = z * (1.0 + lax.erf(z * (2.0 ** -0.5)))
                    act_w[rows, cols] = (act * g_half).astype(BF16)

    def step(z_w, z_r, act_w, act_r):
        d = acc_ref.shape[0]
        mxu = []
        for p in range(MXU_PIECES):
            pc = slice(p * tn // MXU_PIECES, (p + 1) * tn // MXU_PIECES)
            pw = slice(pc.start // 2, pc.stop // 2)

            def z_piece(pc=pc, pw=pw):
                z_w[:, pc] = _dot_nt(_rows_bf16(wd_ref[...]), _rows_bf16(h2_ref[pw, :]))

            mxu.append(z_piece)
            for r in range(UP_ROW_PIECES):
                rs = slice(r * d // UP_ROW_PIECES, (r + 1) * d // UP_ROW_PIECES)
                rw = slice(rs.start // 2, rs.stop // 2)

                def up_piece(pc=pc, rs=rs, rw=rw):
                    acc_ref[rs, pc] += _dot(_rows_bf16(wut_ref[rw, :]), act_r[:, pc])

                mxu.append(up_piece)
        gates = [(lt, k, k + GATE_GROUPS) for lt in range(tn // LANES)
                 for k in range(0, groups, GATE_GROUPS)]
        per_mxu = -(-len(gates) // len(mxu))
        for m, piece in enumerate(mxu):
            piece()
            for lt, k_lo, k_hi in gates[m * per_mxu:(m + 1) * per_mxu]:
                gate_piece(lt, k_lo, k_hi, z_r, act_w)
        for lt, k_lo, k_hi in gates[len(mxu) * per_mxu:]:
            gate_piece(lt, k_lo, k_hi, z_r, act_w)

    @pl.when(s % 2 == 0)
    def _():
        step(z0_ref, z1_ref, act1_ref, act0_ref)

    @pl.when(s % 2 == 1)
    def _():
        step(z1_ref, z0_ref, act0_ref, act1_ref)

    @pl.when(s == pl.num_programs(1) - 1)
    def _():
        o_ref[...] = x2_ref[...] + acc_ref[...].T


def _experts(h2, wd, wut, e1, na, e2, r2, x2):
    t, d = x2.shape
    tn = EXPERT_TN
    ne = 2 * wd.shape[0]
    nk = PEER_N_KEYS
    assert t % tn == 0 and ne % EXPERT_CHUNK == 0 and EXPERT_CHUNK % nk == 0
    n_chunks = ne // EXPERT_CHUNK
    assert n_chunks % 2 == 0
    tok = pl.BlockSpec((tn, d), lambda i, s: (i, 0))
    gate = pl.BlockSpec((PEER_HEADS, nk // 2, tn), lambda i, s: (0, 0, i))
    rpc = EXPERT_CHUNK // nk
    assert SUBLANES % rpc == 0
    chunks_per_group = SUBLANES // rpc
    row = pl.BlockSpec((PEER_HEADS, None, SUBLANES, tn),
                       lambda i, s: (0, jnp.clip(s - 1, 0, n_chunks - 1) // chunks_per_group, 0, i))
    e1 = e1.reshape(PEER_HEADS, nk // SUBLANES, SUBLANES, t)
    na = na.reshape(PEER_HEADS, nk // SUBLANES, SUBLANES, t)
    return pl.pallas_call(
        _experts_kernel,
        grid=(t // tn, n_chunks + 2),
        in_specs=[pl.BlockSpec((tn // 2, d), lambda i, s: (i, 0)),
                  pl.BlockSpec((EXPERT_CHUNK // 2, d), lambda i, s: (jnp.minimum(s, n_chunks - 1), 0)),
                  pl.BlockSpec((None, d // 2, EXPERT_CHUNK),
                               lambda i, s: (jnp.clip(s - 2, 0, n_chunks - 1), 0, 0)),
                  row, row, gate, gate, tok],
        out_specs=tok,
        out_shape=jax.ShapeDtypeStruct((t, d), F32),
        scratch_shapes=[pltpu.VMEM((d, tn), F32),
                        pltpu.VMEM((EXPERT_CHUNK, tn), F32), pltpu.VMEM((EXPERT_CHUNK, tn), F32),
                        pltpu.VMEM((EXPERT_CHUNK, tn), BF16), pltpu.VMEM((EXPERT_CHUNK, tn), BF16),
                        pltpu.VMEM((PEER_HEADS, rpc, tn), jnp.int32), pltpu.VMEM((PEER_HEADS, rpc, tn), jnp.int32),
                        pltpu.VMEM((PEER_HEADS, nk // 2, tn), jnp.int32),
                        pltpu.VMEM((PEER_HEADS, nk // 2, tn), jnp.int32)],
        compiler_params=_params(("parallel", "arbitrary")),
        name="peer_experts",
    )(h2, wd, wut, e1, na, e2, r2, x2)


def _rope_tables(seq):
    pos = jnp.arange(seq, dtype=F32)
    inv_freq = ROPE_THETA ** (-jnp.arange(0, ROT_DIM, 2, dtype=F32) / ROT_DIM)
    ang = pos[:, None] * inv_freq[None, :]
    cos, sin = jnp.cos(ang), jnp.sin(ang)
    rest = ATTN_HEAD_DIM - ROT_DIM
    cos_h = jnp.concatenate([cos, cos, jnp.ones((seq, rest), F32)], axis=-1)
    sin_h = jnp.concatenate([-sin, sin, jnp.zeros((seq, rest), F32)], axis=-1)
    return jnp.tile(cos_h, (1, ATTN_HEADS)), jnp.tile(sin_h, (1, ATTN_HEADS))


def kernel(x, mix_norm_w, w_in, q_norm_w, k_norm_w, conv_w, conv_b, igate_b, fgate_b, mlstm_norm_w,
           w_attn_branch, w_mlstm_branch, merge_b, w_out, ffn_norm_w, peer_w_query, peer_keys_1,
           peer_keys_2, peer_w_down, peer_w_up):
    b, s, d = x.shape
    t = b * s
    x2d = x.reshape(t, d)

    n_main = 3 * ATTN_W + 4 * MLSTM_W
    n_gate = 2 * MLSTM_HEADS
    w_cat = jnp.concatenate(
        [w_in[:, :n_main], w_in[:, n_main + n_gate:], w_in[:, n_main:n_main + n_gate],
         jnp.zeros((d, LANES - n_gate), w_in.dtype)], axis=1).astype(BF16)
    cos_t, sin_t = _rope_tables(s)
    head_of = jnp.arange(ATTN_W) // ATTN_HEAD_DIM
    bd = (head_of[:, None] == head_of[None, :]).astype(BF16)
    qnw = jnp.tile(q_norm_w, ATTN_HEADS).reshape(1, ATTN_W)
    knw = jnp.tile(k_norm_w, ATTN_HEADS).reshape(1, ATTN_W)

    aq, ak, av, mq, mk, mv, mo, ga, gm, gt = _inproj(
        x2d, mix_norm_w.reshape(1, d), w_cat, cos_t, sin_t, bd, qnw, knw, s)

    attn = _moba(aq.reshape(b, s, ATTN_W), ak.reshape(b, s, ATTN_W), av.reshape(b, s, ATTN_W))

    gates = gt[:, :n_gate].reshape(b, s, 2, MLSTM_HEADS)
    g_col = gates.transpose(0, 3, 1, 2)
    g_row = gates.transpose(0, 3, 2, 1)
    gate_b = jnp.stack([igate_b, fgate_b]).astype(F32)
    r3 = lambda a: a.reshape(b, s, MLSTM_W)
    hm = _mlstm(r3(mq), r3(mk), r3(mv), r3(mo), g_col, g_row, gate_b,
                conv_w[:, :MLSTM_W], conv_w[:, MLSTM_W:], conv_b[:MLSTM_W].reshape(1, MLSTM_W),
                conv_b[MLSTM_W:].reshape(1, MLSTM_W), mlstm_norm_w.reshape(1, MLSTM_W))

    x2, h2, q = _merge(x2d, attn.reshape(t, ATTN_W), hm.reshape(t, MLSTM_W), ga, gm,
                       w_attn_branch.astype(BF16), w_mlstm_branch.astype(BF16), merge_b,
                       w_out.astype(BF16), ffn_norm_w.reshape(1, d), peer_w_query.astype(BF16))

    e1, na, e2, r2 = _select(q, peer_keys_1, peer_keys_2)
    wut = peer_w_up.astype(BF16).reshape(-1, EXPERT_CHUNK, d).transpose(0, 2, 1)
    out = _experts(h2, _pack_rows(peer_w_down.astype(BF16)), _pack_rows(wut), e1, na, e2, r2, x2)
    return out.reshape(b, s, d)
```

```python
import functools

import jax
import jax.numpy as jnp
from jax import lax
from jax.experimental import pallas as pl
from jax.experimental.pallas import tpu as pltpu

F32 = jnp.float32
BF16 = jnp.bfloat16

EPS = 1e-6
ATTN_HEADS = 8
ATTN_HEAD_DIM = 64
ATTN_W = ATTN_HEADS * ATTN_HEAD_DIM
ROT_DIM = ATTN_HEAD_DIM // 4
ROPE_THETA = 500000.0
MOBA_BLOCK = 256
MOBA_TOPK = 3
MLSTM_HEADS = 4
MLSTM_HEAD_DIM = 128
MLSTM_W = MLSTM_HEADS * MLSTM_HEAD_DIM
CONV_WIDTH = 4
PEER_HEADS = 8
PEER_N_KEYS = 128
PEER_HALF = 128
PEER_TOPK = 16

LANES = 128
SUBLANES = 8
NEG = -1e30
VMEM_LIMIT = 56 * 1024 * 1024

INPROJ_TM = 512
MERGE_TM = 512
MLSTM_CHUNK = 256
SELECT_TN = 512
EXPERT_TN = 512
EXPERT_CHUNK = 512
MXU_PIECES = 2
UP_ROW_PIECES = 4
GATE_GROUPS = 2


def _dot(a, b):
    return jnp.dot(a, b, preferred_element_type=F32)


def _dot_nt(a, b):
    return lax.dot_general(a, b, (((1,), (1,)), ((), ())), preferred_element_type=F32)


def _dot_tn(a, b):
    return lax.dot_general(a, b, (((0,), (0,)), ((), ())), preferred_element_type=F32)


def _bf16_bits(x):
    return lax.bitcast_convert_type(x.astype(BF16).astype(F32), jnp.int32)


def _pack_pairs(lo, hi):
    return lax.shift_right_logical(_bf16_bits(lo), jnp.int32(16)) | _bf16_bits(hi)


def _as_bf16_pairs(words):
    return pltpu.bitcast(words, BF16)


def _unpack_pairs(pairs):
    words = pltpu.bitcast(pairs, jnp.int32)
    lo = lax.bitcast_convert_type(lax.shift_left(words, jnp.int32(16)), F32)
    hi = lax.bitcast_convert_type(words & jnp.int32(-65536), F32)
    return lo, hi


def _pack_rows(x):
    *lead, m, n = x.shape
    pairs = jnp.swapaxes(x.reshape(*lead, m // 2, 2, n), -1, -2)
    return lax.bitcast_convert_type(pairs, jnp.int32)


def _rows_bf16(words):
    return pltpu.bitcast(words, BF16)


def _split2(x):
    hi = x.astype(BF16)
    lo = (x - hi.astype(F32)).astype(BF16)
    return hi, lo


def _params(sem, flags=None):
    return pltpu.CompilerParams(dimension_semantics=sem, vmem_limit_bytes=VMEM_LIMIT, flags=flags)


def _inproj_kernel(x_ref, nw_ref, w_ref, cos_ref, sin_ref, bd_ref, qnw_ref, knw_ref,
                   aq_ref, ak_ref, av_ref, mq_ref, mk_ref, mv_ref, mo_ref, ga_ref, gm_ref, gt_ref):
    x = x_ref[...]
    h = x * lax.rsqrt(jnp.mean(x * x, axis=-1, keepdims=True) + EPS) * nw_ref[...]
    hb = h.astype(BF16)

    def proj(lo, width):
        return _dot(hb, w_ref[:, lo:lo + width])

    lane = lax.broadcasted_iota(jnp.int32, (x.shape[0], ATTN_W), 1) % ATTN_HEAD_DIM
    half = ROT_DIM // 2

    def qk_norm_rope(t, nw):
        hi, lo = _split2(t * t)
        ms = (_dot(hi, bd_ref[...]) + _dot(lo, bd_ref[...])) * (1.0 / ATTN_HEAD_DIM)
        tn = t * lax.rsqrt(ms + EPS) * nw
        fwd = pltpu.roll(tn, ATTN_W - half, axis=1)
        bwd = pltpu.roll(tn, half, axis=1)
        swapped = jnp.where(lane < half, fwd, jnp.where(lane < ROT_DIM, bwd, 0.0))
        return tn * cos_ref[...] + swapped * sin_ref[...]

    aq_ref[...] = qk_norm_rope(proj(0, ATTN_W), qnw_ref[...]).astype(BF16)
    ak_ref[...] = qk_norm_rope(proj(ATTN_W, ATTN_W), knw_ref[...]).astype(BF16)
    av_ref[...] = proj(2 * ATTN_W, ATTN_W).astype(BF16)
    base = 3 * ATTN_W
    mq_ref[...] = proj(base, MLSTM_W).astype(BF16)
    mk_ref[...] = proj(base + MLSTM_W, MLSTM_W).astype(BF16)
    mv_ref[...] = proj(base + 2 * MLSTM_W, MLSTM_W).astype(BF16)
    mo_ref[...] = proj(base + 3 * MLSTM_W, MLSTM_W).astype(BF16)
    base = base + 4 * MLSTM_W
    d = x.shape[1]
    ga_ref[...] = proj(base, d).astype(BF16)
    gm_ref[...] = proj(base + d, d).astype(BF16)
    gt_ref[...] = proj(base + 2 * d, LANES)


def _inproj(x2d, mix_norm_w, w_cat, cos_t, sin_t, bd, qnw, knw, seq):
    t, d = x2d.shape
    tm = INPROJ_TM
    assert t % tm == 0 and seq % tm == 0
    tiles_per_seq = seq // tm
    ncols = w_cat.shape[1]
    row = lambda w: pl.BlockSpec((tm, w), lambda i: (i, 0))
    const = lambda shape: pl.BlockSpec(shape, lambda i: (0, 0))
    pos = pl.BlockSpec((tm, ATTN_W), lambda i: (i % tiles_per_seq, 0))
    outs = [ATTN_W] * 3 + [MLSTM_W] * 4 + [d, d]
    return pl.pallas_call(
        _inproj_kernel,
        grid=(t // tm,),
        in_specs=[row(d), const((1, d)), const((d, ncols)), pos, pos, const((ATTN_W, ATTN_W)),
                  const((1, ATTN_W)), const((1, ATTN_W))],
        out_specs=[row(w) for w in outs] + [row(LANES)],
        out_shape=[jax.ShapeDtypeStruct((t, w), BF16) for w in outs]
        + [jax.ShapeDtypeStruct((t, LANES), F32)],
        compiler_params=_params(("parallel",)),
        name="inproj",
    )(x2d, mix_norm_w, w_cat, cos_t, sin_t, bd, qnw, knw)


def _moba_kernel(q_ref, k_ref, v_ref, o_ref, kmean_ref, kext_ref, vext_ref, s_ref, *, n_blocks):
    i = pl.program_id(2)
    blk = MOBA_BLOCK
    hd = ATTN_HEAD_DIM
    heads = (0, 1)
    lane = lax.broadcasted_iota(jnp.int32, (blk, LANES), 1)
    in_head = [lane < hd, lane >= hd]
    spare0 = [hd, 0]

    @pl.when(i == 0)
    def _():
        kmean_ref[...] = jnp.zeros_like(kmean_ref)
        for n in range(n_blocks):
            rows = slice(n * blk, (n + 1) * blk)
            kb = k_ref[rows, :]
            vb = v_ref[rows, :]
            kmean_ref[n:n + 1, :] = jnp.mean(kb.astype(F32), axis=0, keepdims=True)
            for h in heads:
                onehot = jnp.where(lane == spare0[h] + n, 1.0, 0.0).astype(BF16)
                kext_ref[h, rows, :] = jnp.where(in_head[h], kb, onehot)
                vext_ref[h, rows, :] = jnp.where(in_head[h], vb, jnp.ones_like(vb))

    q = q_ref[...]
    kmean_hi, kmean_lo = _split2(kmean_ref[...])
    brow = lax.broadcasted_iota(jnp.int32, (16, blk), 0)
    qext = []
    for h in heads:
        qh = jnp.where(in_head[h], q, jnp.zeros_like(q)) * jnp.asarray(hd ** -0.5, BF16)
        g_t = (_dot_nt(kmean_hi, qh) + _dot_nt(kmean_lo, qh))[0:16, :]
        rank = jnp.zeros((16, blk), F32)
        for m in range(min(n_blocks, 16)):
            gm = g_t[m:m + 1, :]
            beats = jnp.where(gm > g_t, 1.0, jnp.where(gm == g_t, jnp.where(brow > m, 1.0, 0.0), 0.0))
            rank = rank + jnp.where(m < i, beats, 0.0)
        bias_t = jnp.where(brow < i, jnp.where(rank < MOBA_TOPK, 0.0, NEG), 0.0)
        bias = jnp.concatenate([bias_t, jnp.zeros((LANES - 16, blk), F32)], axis=0).T
        if spare0[h]:
            bias = pltpu.roll(bias, spare0[h], axis=1)
        qext.append(qh + bias.astype(BF16))

    own = pl.multiple_of(i * blk, blk)
    causal = (lax.broadcasted_iota(jnp.int32, (blk, blk), 1)
              <= lax.broadcasted_iota(jnp.int32, (blk, blk), 0))
    def scores(t, m_run):
        m_new = list(m_run)
        for u in range(2):
            n = 2 * t + u
            start = pl.multiple_of(n * blk, blk)
            for h in heads:
                s = _dot_nt(qext[h], kext_ref[h, pl.ds(start, blk), :])
                s_ref[h, n] = s
                m_cand = jnp.maximum(m_new[h], jnp.max(s, axis=1, keepdims=True))
                m_new[h] = m_cand if u == 0 else jnp.where(n < i, m_cand, m_new[h])
        return tuple(m_new)

    m_past = lax.fori_loop(0, (i + 1) // 2, scores, tuple(jnp.full((blk, 1), NEG, F32) for _ in heads))
    m_fin = []
    for h in heads:
        s = jnp.where(causal, _dot_nt(qext[h], kext_ref[h, pl.ds(own, blk), :]), NEG)
        s_ref[h, i] = s
        m_fin.append(jnp.maximum(m_past[h], jnp.max(s, axis=1, keepdims=True)))

    def values(t, acc):
        out = list(acc)
        for u in range(2):
            n = 2 * t + u
            n_read = jnp.minimum(n, i)
            start = pl.multiple_of(n_read * blk, blk)
            for h in heads:
                m_eff = m_fin[h] if u == 0 else jnp.where(n <= i, m_fin[h], -NEG)
                p = jnp.exp(s_ref[h, n_read] - m_eff).astype(BF16)
                out[h] = out[h] + _dot(p, vext_ref[h, pl.ds(start, blk), :])
        return tuple(out)

    acc = lax.fori_loop(0, (i + 2) // 2, values, tuple(jnp.zeros((blk, LANES), F32) for _ in heads))
    outs = [a / pltpu.roll(a, hd, axis=1) for a in acc]
    o_ref[...] = jnp.where(in_head[0], outs[0], outs[1]).astype(o_ref.dtype)


def _moba(q, k, v):
    b, s, w = q.shape
    assert s % MOBA_BLOCK == 0 and w % LANES == 0
    n_blocks = s // MOBA_BLOCK
    assert n_blocks <= 16
    qspec = pl.BlockSpec((None, MOBA_BLOCK, LANES), lambda bi, hp, i: (bi, i, hp))
    kvspec = pl.BlockSpec((None, s, LANES), lambda bi, hp, i: (bi, 0, hp))
    return pl.pallas_call(
        functools.partial(_moba_kernel, n_blocks=n_blocks),
        grid=(b, w // LANES, n_blocks),
        in_specs=[qspec, kvspec, kvspec],
        out_specs=qspec,
        out_shape=jax.ShapeDtypeStruct((b, s, w), BF16),
        scratch_shapes=[pltpu.VMEM((LANES, LANES), F32),
                        pltpu.VMEM((2, s, LANES), BF16), pltpu.VMEM((2, s, LANES), BF16),
                        pltpu.VMEM((2, n_blocks, MOBA_BLOCK, MOBA_BLOCK), F32)],
        compiler_params=_params(("parallel", "parallel", "arbitrary")),
        name="moba",
    )(q, k, v)


def _log_sigmoid(f):
    return jnp.minimum(f, 0.0) - jnp.log(1.0 + jnp.exp(-jnp.abs(f)))


def _mlstm_kernel(gb_ref, mq_ref, mk_ref, mv_ref, mo_ref, gc_ref, gr_ref, cwq_ref, cwk_ref,
                  cbq_ref, cbk_ref, nw_ref, o_ref, c_ref, n_ref, m_ref):
    hh = pl.program_id(1)
    c = pl.program_id(2)
    L = MLSTM_CHUNK
    d = MLSTM_HEAD_DIM

    @pl.when(c == 0)
    def _():
        c_ref[...] = jnp.zeros_like(c_ref)
        n_ref[...] = jnp.zeros_like(n_ref)
        m_ref[...] = jnp.zeros_like(m_ref)

    start = pl.multiple_of(c * L, L)
    prev = pl.multiple_of(jnp.maximum(c - 1, 0) * L, L)
    row_d = lax.broadcasted_iota(jnp.int32, (L, d), 0)

    def conv_silu(u_ref, w_ref, b_ref):
        u = u_ref[pl.ds(start, L), :].astype(F32)
        pv = u_ref[pl.ds(prev, L), :].astype(F32)
        pv = jnp.where(c > 0, pv, 0.0)
        y = u * w_ref[CONV_WIDTH - 1:CONV_WIDTH, :] + b_ref[...]
        for k in range(1, CONV_WIDTH):
            shifted = jnp.where(row_d < k, pltpu.roll(pv, k, axis=0), pltpu.roll(u, k, axis=0))
            y = y + shifted * w_ref[CONV_WIDTH - 1 - k:CONV_WIDTH - k, :]
        return y * jax.nn.sigmoid(y)

    qc = conv_silu(mq_ref, cwq_ref, cbq_ref)
    kc = conv_silu(mk_ref, cwk_ref, cbk_ref) * (d ** -0.5)
    qb = qc.astype(BF16)
    kb = kc.astype(BF16)
    vb = mv_ref[...]

    ib = gb_ref[0, hh]
    fb = gb_ref[1, hh]
    gcol = gc_ref[...]
    grow = gr_ref[...]
    ig_col = gcol[:, 0:1] + ib
    ig_row = grow[0:1, :] + ib
    lf_col = _log_sigmoid(gcol[:, 1:2] + fb)
    lf_row = _log_sigmoid(grow[1:2, :] + fb)

    r = lax.broadcasted_iota(jnp.int32, (L, L), 0)
    s = lax.broadcasted_iota(jnp.int32, (L, L), 1)
    causal = s <= r
    tri_l = jnp.where(causal, 1.0, 0.0).astype(BF16)
    tri_u = jnp.where(r <= s, 1.0, 0.0).astype(BF16)
    lfc_hi, lfc_lo = _split2(jnp.broadcast_to(lf_col, (L, L)))
    lfr_hi, lfr_lo = _split2(jnp.broadcast_to(lf_row, (L, L)))
    b_t = _dot(tri_l, lfc_hi) + _dot(tri_l, lfc_lo)
    b_s = _dot(lfr_hi, tri_u) + _dot(lfr_lo, tri_u)

    m_prev = m_ref[...]
    b_col = b_t[:, 0:1]
    log_d = jnp.where(causal, b_t - b_s + ig_row, NEG)
    m_inter = b_col + m_prev
    m_t = jnp.maximum(m_inter, jnp.max(log_d, axis=1, keepdims=True))
    a = jnp.exp(log_d - m_t) * _dot_nt(qb, kb)
    w_inter = jnp.exp(m_inter - m_t)
    c_prev = c_ref[...]
    n_prev = n_ref[...]
    num = _dot(a.astype(BF16), vb) + w_inter * _dot(qb, c_prev.astype(BF16))
    den = jnp.sum(a, axis=1, keepdims=True) + w_inter * jnp.sum(qc * n_prev, axis=1, keepdims=True)
    h = num / jnp.maximum(jnp.abs(den), jnp.exp(-m_t))

    m_new = m_t[L - 1:L, :]
    b_last = b_col[L - 1:L, :]
    ws_col = jnp.exp(b_last - b_col + ig_col - m_new)
    decay = jnp.exp(b_last + m_prev - m_new)
    c_ref[...] = decay * c_prev + _dot_tn(kb, (ws_col * vb.astype(F32)).astype(BF16))
    n_ref[...] = decay * n_prev + jnp.sum(ws_col * kc, axis=0, keepdims=True)
    m_ref[...] = m_new

    hn = h * lax.rsqrt(jnp.mean(h * h, axis=-1, keepdims=True) + EPS) * nw_ref[...]
    o_ref[...] = (jax.nn.sigmoid(mo_ref[...].astype(F32)) * hn).astype(o_ref.dtype)


def _mlstm(mq, mk, mv, mo, g_col, g_row, gate_b, conv_wq, conv_wk, conv_bq, conv_bk, norm_w):
    b, s, w = mq.shape
    L = MLSTM_CHUNK
    d = MLSTM_HEAD_DIM
    assert s % L == 0
    seq = pl.BlockSpec((None, s, d), lambda bi, h, c: (bi, 0, h))
    chunk = pl.BlockSpec((None, L, d), lambda bi, h, c: (bi, c, h))
    per_head = lambda rows: pl.BlockSpec((rows, d), lambda bi, h, c: (0, h))
    return pl.pallas_call(
        _mlstm_kernel,
        grid=(b, w // d, s // L),
        in_specs=[pl.BlockSpec(memory_space=pltpu.SMEM), seq, seq, chunk, chunk,
                  pl.BlockSpec((None, None, L, 2), lambda bi, h, c: (bi, h, c, 0)),
                  pl.BlockSpec((None, None, 2, L), lambda bi, h, c: (bi, h, 0, c)),
                  per_head(CONV_WIDTH), per_head(CONV_WIDTH), per_head(1), per_head(1), per_head(1)],
        out_specs=chunk,
        out_shape=jax.ShapeDtypeStruct((b, s, w), BF16),
        scratch_shapes=[pltpu.VMEM((d, d), F32), pltpu.VMEM((1, d), F32), pltpu.VMEM((1, 1), F32)],
        compiler_params=_params(("parallel", "parallel", "arbitrary")),
        name="mlstm",
    )(gate_b, mq, mk, mv, mo, g_col, g_row, conv_wq, conv_wk, conv_bq, conv_bk, norm_w)


def _merge_kernel(x_ref, attn_ref, hm_ref, ga_ref, gm_ref, wab_ref, wmb_ref, mb_ref, wout_ref,
                  fnw_ref, wq_ref, x2_ref, h2_ref, q_ref):
    ya = _dot(attn_ref[...], wab_ref[...])
    ym = _dot(hm_ref[...], wmb_ref[...])
    y = (jax.nn.sigmoid(ga_ref[...].astype(F32) + mb_ref[0:1, :]) * ya
         + jax.nn.sigmoid(gm_ref[...].astype(F32) + mb_ref[1:2, :]) * ym)
    x2 = x_ref[...] + _dot(y.astype(BF16), wout_ref[...])
    x2_ref[...] = x2
    h2 = x2 * lax.rsqrt(jnp.mean(x2 * x2, axis=-1, keepdims=True) + EPS) * fnw_ref[...]
    h2b = h2.astype(BF16)
    h2_ref[...] = pltpu.bitcast(h2b, jnp.int32)
    q_ref[...] = _dot(h2b, wq_ref[...])


def _merge(x2d, attn, hm, ga, gm, wab, wmb, merge_b, wout, fnw, wq):
    t, d = x2d.shape
    tm = MERGE_TM
    nq = wq.shape[1]
    row = lambda w: pl.BlockSpec((tm, w), lambda i: (i, 0))
    const = lambda a: pl.BlockSpec(a.shape, lambda i: (0, 0))
    return pl.pallas_call(
        _merge_kernel,
        grid=(t // tm,),
        in_specs=[row(d), row(ATTN_W), row(MLSTM_W), row(d), row(d), const(wab), const(wmb),
                  const(merge_b), const(wout), const(fnw), const(wq)],
        out_specs=[row(d), pl.BlockSpec((tm // 2, d), lambda i: (i, 0)), row(nq)],
        out_shape=[jax.ShapeDtypeStruct((t, d), F32), jax.ShapeDtypeStruct((t // 2, d), jnp.int32),
                   jax.ShapeDtypeStruct((t, nq), F32)],
        compiler_params=_params(("parallel",)),
        name="merge",
    )(x2d, attn, hm, ga, gm, wab, wmb, merge_b, wout, fnw, wq)


_CAND_GROUPS = (
    ((0, 0, 0, 8),), ((0, 8, 0, 8),), ((1, 0, 0, 8),),
    ((2, 0, 0, 5), (4, 0, 5, 3)),
    ((3, 0, 0, 4), (5, 0, 4, 2), (6, 0, 6, 2)),
    ((7, 0, 0, 2),) + tuple((i, 0, i - 6, 1) for i in range(8, 14)),
    ((14, 0, 0, 1), (15, 0, 1, 1)),
)


def _select_kernel(q1_ref, q2_ref, k1_ref, k2_ref, e1_ref, na_ref, e2_ref, r2_ref,
                   s1_ref, s2_ref, w1_ref, w2_ref, rk1_ref, rk2_ref, v_ref, c_ref, e_ref):
    tn = q1_ref.shape[0]
    nk = PEER_N_KEYS
    topk = PEER_TOPK
    n_lt = tn // LANES
    lane_tiles = [slice(lt * LANES, (lt + 1) * LANES) for lt in range(n_lt)]
    halves = ((s1_ref, w1_ref, rk1_ref), (s2_ref, w2_ref, rk2_ref))

    def scores(k_ref, q_ref):
        kh, kl = _split2(k_ref[...])
        qh, ql = _split2(q_ref[...])
        return _dot_nt(kh, qh) + _dot_nt(kh, ql) + _dot_nt(kl, qh)

    s1_ref[...] = scores(k1_ref, q1_ref)
    s2_ref[...] = scores(k2_ref, q2_ref)
    key = lax.broadcasted_iota(jnp.int32, (nk, LANES), 0).astype(F32)

    def extract(break_ties):
        for s_ref, w_ref, rk_ref in halves:
            w_ref[...] = s_ref[...]
            rk_ref[...] = jnp.full(rk_ref.shape, float(topk), F32)
        for it in range(topk):
            for half, (_, w_ref, rk_ref) in enumerate(halves):
                for cols in lane_tiles:
                    s_cur = w_ref[:, cols]
                    m = jnp.max(s_cur, axis=0, keepdims=True)
                    if break_ties:
                        hit = key == jnp.min(jnp.where(s_cur == m, key, float(nk)), axis=0, keepdims=True)
                    else:
                        hit = s_cur == m
                    w_ref[:, cols] = jnp.where(hit, -jnp.inf, s_cur)
                    rk_ref[:, cols] = jnp.where(hit, float(it), rk_ref[:, cols])
                    v_ref[half, it, :, cols] = m

    extract(break_ties=False)
    off_count = jnp.zeros((1, LANES), F32)
    for _, _, rk_ref in halves:
        for cols in lane_tiles:
            taken = jnp.sum(jnp.where(rk_ref[:, cols] < float(topk), 1.0, 0.0), axis=0, keepdims=True)
            off_count = off_count + jnp.where(taken == float(topk), 0.0, 1.0)

    @pl.when(jnp.max(off_count) > 0.0)
    def _():
        extract(break_ties=True)

    row8 = lax.broadcasted_iota(jnp.int32, (SUBLANES, LANES), 0)
    seg_mask = {seg: (row8 >= seg[2]) & (row8 < seg[2] + seg[3]) for grp in _CAND_GROUPS for seg in grp}
    flats = []
    for grp in _CAND_GROUPS:
        f = jnp.full((SUBLANES, LANES), 1e9, F32)
        for seg in grp:
            i, j0, r0, _ = seg
            f = jnp.where(seg_mask[seg], (row8 + (i * topk + j0 - r0)).astype(F32), f)
        flats.append(f)
    flat = jnp.concatenate(flats, axis=0)
    for cols in lane_tiles:
        v1 = [v_ref[0, i, :, cols] for i in range(topk)]
        v2 = [v_ref[1, j, :, cols] for j in range(topk)]
        v2_groups = []
        for j0 in (0, SUBLANES):
            dense = jnp.zeros((SUBLANES, LANES), F32)
            for r in range(SUBLANES):
                dense = jnp.where(row8 == r, v2[j0 + r], dense)
            v2_groups.append(dense)
        top = v1[0] + v2[0]
        for g, grp in enumerate(_CAND_GROUPS):
            cand = jnp.full((SUBLANES, LANES), -jnp.inf, F32)
            for seg in grp:
                i, j0, r0, n = seg
                if n == 1 and j0 == 0:
                    piece = v1[i] + v2[0]
                else:
                    src = v2_groups[j0 // SUBLANES]
                    piece = v1[i] + (pltpu.roll(src, r0, axis=0) if r0 else src)
                cand = jnp.where(seg_mask[seg], piece, cand)
            rows = slice(g * SUBLANES, (g + 1) * SUBLANES)
            c_ref[rows, cols] = cand
            e_ref[rows, cols] = jnp.exp(cand - top)

    for _ in range(topk):
        for cols in lane_tiles:
            c_cur = c_ref[:, cols]
            m = jnp.max(c_cur, axis=0, keepdims=True)
            idx = jnp.min(jnp.where(c_cur == m, flat, 1e9), axis=0, keepdims=True)
            c_ref[:, cols] = jnp.where(flat == idx, -jnp.inf, c_cur)

    for lt, cols in enumerate(lane_tiles):
        chosen = jnp.where(flat < 1e8, jnp.where(c_ref[:, cols] == -jnp.inf, 1.0, 0.0), 0.0)
        z = jnp.sum(chosen * e_ref[:, cols], axis=0, keepdims=True)
        counts = [jnp.zeros((1, LANES), F32) for _ in range(topk)]
        for g, grp in enumerate(_CAND_GROUPS):
            ch = chosen[g * SUBLANES:(g + 1) * SUBLANES, :]
            for seg in grp:
                counts[seg[0]] = counts[seg[0]] + jnp.sum(
                    jnp.where(seg_mask[seg], ch, 0.0), axis=0, keepdims=True)
        v1_top = v_ref[0, 0, :, cols]
        v2_top = v_ref[1, 0, :, cols]

        r1 = rk1_ref[:, cols]
        r2 = rk2_ref[:, cols]
        n_a = jnp.zeros((nk, LANES), F32)
        for i in range(topk):
            n_a = jnp.where(r1 == float(i), counts[i], n_a)
        in1 = r1 < float(topk)
        in2 = r2 < float(topk)
        e1 = jnp.where(in1, jnp.exp(s1_ref[:, cols] - v1_top) * (0.5 / z), 0.0)
        e2 = jnp.where(in2, jnp.exp(s2_ref[:, cols] - v2_top), 0.0)
        e1_ref[:, cols] = _pack_pairs(e1, e1)
        na_ref[:, cols] = _pack_pairs(n_a, n_a)
        e2_ref[:, cols] = _pack_pairs(e2[:nk // 2], e2[nk // 2:])
        r2_ref[:, cols] = _pack_pairs(r2[:nk // 2], r2[nk // 2:])


def _select(q, keys_1, keys_2):
    t = q.shape[0]
    tn = SELECT_TN
    assert t % tn == 0
    nk = PEER_N_KEYS
    q1 = pl.BlockSpec((tn, PEER_HALF), lambda i, h: (i, 2 * h))
    q2 = pl.BlockSpec((tn, PEER_HALF), lambda i, h: (i, 2 * h + 1))
    kspec = pl.BlockSpec((None, nk, PEER_HALF), lambda i, h: (h, 0, 0))
    ospec = lambda rows: pl.BlockSpec((None, rows, tn), lambda i, h: (h, 0, i))
    oshape = lambda rows: jax.ShapeDtypeStruct((PEER_HEADS, rows, t), jnp.int32)
    keys_buf = pltpu.VMEM((nk, tn), F32)
    return pl.pallas_call(
        _select_kernel,
        grid=(t // tn, PEER_HEADS),
        in_specs=[q1, q2, kspec, kspec],
        out_specs=[ospec(nk), ospec(nk), ospec(nk // 2), ospec(nk // 2)],
        out_shape=[oshape(nk), oshape(nk), oshape(nk // 2), oshape(nk // 2)],
        scratch_shapes=[keys_buf] * 6 + [pltpu.VMEM((2, PEER_TOPK, 1, tn), F32)]
        + [pltpu.VMEM((SUBLANES * len(_CAND_GROUPS), tn), F32)] * 2,
        compiler_params=_params(("parallel", "parallel")),
        name="peer_select",
    )(q, q, keys_1, keys_2)


def _experts_kernel(h2_ref, wd_ref, wut_ref, e1_in, na_in, e2_in, r2_in, x2_ref, o_ref,
                    acc_ref, z0_ref, z1_ref, act0_ref, act1_ref, e1_ref, na_ref, e2_ref, r2_ref):
    s = pl.program_id(1)
    n_chunks = pl.num_programs(1) - 2
    nk = PEER_N_KEYS
    tn = z0_ref.shape[1]
    rows_per_chunk = EXPERT_CHUNK // nk

    @pl.when(s == 0)
    def _():
        for ref in (acc_ref, z0_ref, z1_ref, act0_ref, act1_ref):
            ref[...] = jnp.zeros_like(ref)
        e2_ref[...] = e2_in[...]
        r2_ref[...] = r2_in[...]

    chunk = jnp.clip(s - 1, 0, n_chunks - 1)
    r0 = (chunk % (SUBLANES // rows_per_chunk)) * rows_per_chunk
    for h in range(PEER_HEADS):
        for aa in range(rows_per_chunk):
            e1_ref[h, aa:aa + 1, :] = e1_in[h, pl.ds(r0 + aa, 1), :]
            na_ref[h, aa:aa + 1, :] = na_in[h, pl.ds(r0 + aa, 1), :]

    half = nk // 2
    groups = half // SUBLANES

    def gate_piece(lt, k_lo, k_hi, z_r, act_w):
        cols = slice(lt * LANES, (lt + 1) * LANES)
        g = [[jnp.zeros((2 * SUBLANES, LANES), BF16) for _ in range(k_lo, k_hi)]
             for _ in range(rows_per_chunk)]
        for h in range(PEER_HEADS):
            as_pair = lambda ref, aa: _as_bf16_pairs(
                jnp.broadcast_to(ref[h, aa:aa + 1, cols], (SUBLANES, LANES)))
            e1 = [as_pair(e1_ref, aa) for aa in range(rows_per_chunk)]
            n_a = [as_pair(na_ref, aa) for aa in range(rows_per_chunk)]
            for k in range(k_lo, k_hi):
                grp = slice(k * SUBLANES, (k + 1) * SUBLANES)
                r2 = _as_bf16_pairs(r2_ref[h, grp, cols])
                e2 = _as_bf16_pairs(e2_ref[h, grp, cols])
                for aa in range(rows_per_chunk):
                    g[aa][k - k_lo] = (g[aa][k - k_lo]
                                       + jnp.where(r2 < n_a[aa], e2, jnp.zeros_like(e2)) * e1[aa])
        for aa in range(rows_per_chunk):
            for k in range(k_lo, k_hi):
                g_lo, g_hi = _unpack_pairs(g[aa][k - k_lo])
                for g_half, base in ((g_lo, aa * nk), (g_hi, aa * nk + half)):
                    rows = slice(base + k * SUBLANES, base + (k + 1) * SUBLANES)
                    z = z_r[rows, cols]
                    act = z * (1.0 + lax.erf(z * (2.0 ** -0.5)))
                    act_w[rows, cols] = (act * g_half).astype(BF16)

    def step(z_w, z_r, act_w, act_r):
        d = acc_ref.shape[0]
        mxu = []
        for p in range(MXU_PIECES):
            pc = slice(p * tn // MXU_PIECES, (p + 1) * tn // MXU_PIECES)
            pw = slice(pc.start // 2, pc.stop // 2)

            def z_piece(pc=pc, pw=pw):
                z_w[:, pc] = _dot_nt(_rows_bf16(wd_ref[...]), _rows_bf16(h2_ref[pw, :]))

            mxu.append(z_piece)
            for r in range(UP_ROW_PIECES):
                rs = slice(r * d // UP_ROW_PIECES, (r + 1) * d // UP_ROW_PIECES)
                rw = slice(rs.start // 2, rs.stop // 2)

                def up_piece(pc=pc, rs=rs, rw=rw):
                    acc_ref[rs, pc] += _dot(_rows_bf16(wut_ref[rw, :]), act_r[:, pc])

                mxu.append(up_piece)
        gates = [(lt, k, k + GATE_GROUPS) for lt in range(tn // LANES)
                 for k in range(0, groups, GATE_GROUPS)]
        per_mxu = -(-len(gates) // len(mxu))
        for m, piece in enumerate(mxu):
            piece()
            for lt, k_lo, k_hi in gates[m * per_mxu:(m + 1) * per_mxu]:
                gate_piece(lt, k_lo, k_hi, z_r, act_w)
        for lt, k_lo, k_hi in gates[len(mxu) * per_mxu:]:
            gate_piece(lt, k_lo, k_hi, z_r, act_w)

    @pl.when(s % 2 == 0)
    def _():
        step(z0_ref, z1_ref, act1_ref, act0_ref)

    @pl.when(s % 2 == 1)
    def _():
        step(z1_ref, z0_ref, act0_ref, act1_ref)

    @pl.when(s == pl.num_programs(1) - 1)
    def _():
        o_ref[...] = x2_ref[...] + acc_ref[...].T


def _experts(h2, wd, wut, e1, na, e2, r2, x2):
    t, d = x2.shape
    tn = EXPERT_TN
    ne = 2 * wd.shape[0]
    nk = PEER_N_KEYS
    assert t % tn == 0 and ne % EXPERT_CHUNK == 0 and EXPERT_CHUNK % nk == 0
    n_chunks = ne // EXPERT_CHUNK
    assert n_chunks % 2 == 0
    tok = pl.BlockSpec((tn, d), lambda i, s: (i, 0))
    gate = pl.BlockSpec((PEER_HEADS, nk // 2, tn), lambda i, s: (0, 0, i))
    rpc = EXPERT_CHUNK // nk
    assert SUBLANES % rpc == 0
    chunks_per_group = SUBLANES // rpc
    row = pl.BlockSpec((PEER_HEADS, None, SUBLANES, tn),
                       lambda i, s: (0, jnp.clip(s - 1, 0, n_chunks - 1) // chunks_per_group, 0, i))
    e1 = e1.reshape(PEER_HEADS, nk // SUBLANES, SUBLANES, t)
    na = na.reshape(PEER_HEADS, nk // SUBLANES, SUBLANES, t)
    return pl.pallas_call(
        _experts_kernel,
        grid=(t // tn, n_chunks + 2),
        in_specs=[pl.BlockSpec((tn // 2, d), lambda i, s: (i, 0)),
                  pl.BlockSpec((EXPERT_CHUNK // 2, d), lambda i, s: (jnp.minimum(s, n_chunks - 1), 0)),
                  pl.BlockSpec((None, d // 2, EXPERT_CHUNK),
                               lambda i, s: (jnp.clip(s - 2, 0, n_chunks - 1), 0, 0)),
                  row, row, gate, gate, tok],
        out_specs=tok,
        out_shape=jax.ShapeDtypeStruct((t, d), F32),
        scratch_shapes=[pltpu.VMEM((d, tn), F32),
                        pltpu.VMEM((EXPERT_CHUNK, tn), F32), pltpu.VMEM((EXPERT_CHUNK, tn), F32),
                        pltpu.VMEM((EXPERT_CHUNK, tn), BF16), pltpu.VMEM((EXPERT_CHUNK, tn), BF16),
                        pltpu.VMEM((PEER_HEADS, rpc, tn), jnp.int32), pltpu.VMEM((PEER_HEADS, rpc, tn), jnp.int32),
                        pltpu.VMEM((PEER_HEADS, nk // 2, tn), jnp.int32),
                        pltpu.VMEM((PEER_HEADS, nk // 2, tn), jnp.int32)],
        compiler_params=_params(("parallel", "arbitrary")),
        name="peer_experts",
    )(h2, wd, wut, e1, na, e2, r2, x2)


def _rope_tables(seq):
    pos = jnp.arange(seq, dtype=F32)
    inv_freq = ROPE_THETA ** (-jnp.arange(0, ROT_DIM, 2, dtype=F32) / ROT_DIM)
    ang = pos[:, None] * inv_freq[None, :]
    cos, sin = jnp.cos(ang), jnp.sin(ang)
    rest = ATTN_HEAD_DIM - ROT_DIM
    cos_h = jnp.concatenate([cos, cos, jnp.ones((seq, rest), F32)], axis=-1)
    sin_h = jnp.concatenate([-sin, sin, jnp.zeros((seq, rest), F32)], axis=-1)
    return jnp.tile(cos_h, (1, ATTN_HEADS)), jnp.tile(sin_h, (1, ATTN_HEADS))


def kernel(x, mix_norm_w, w_in, q_norm_w, k_norm_w, conv_w, conv_b, igate_b, fgate_b, mlstm_norm_w,
           w_attn_branch, w_mlstm_branch, merge_b, w_out, ffn_norm_w, peer_w_query, peer_keys_1,
           peer_keys_2, peer_w_down, peer_w_up):
    b, s, d = x.shape
    t = b * s
    x2d = x.reshape(t, d)

    n_main = 3 * ATTN_W + 4 * MLSTM_W
    n_gate = 2 * MLSTM_HEADS
    w_cat = jnp.concatenate(
        [w_in[:, :n_main], w_in[:, n_main + n_gate:], w_in[:, n_main:n_main + n_gate],
         jnp.zeros((d, LANES - n_gate), w_in.dtype)], axis=1).astype(BF16)
    cos_t, sin_t = _rope_tables(s)
    head_of = jnp.arange(ATTN_W) // ATTN_HEAD_DIM
    bd = (head_of[:, None] == head_of[None, :]).astype(BF16)
    qnw = jnp.tile(q_norm_w, ATTN_HEADS).reshape(1, ATTN_W)
    knw = jnp.tile(k_norm_w, ATTN_HEADS).reshape(1, ATTN_W)

    aq, ak, av, mq, mk, mv, mo, ga, gm, gt = _inproj(
        x2d, mix_norm_w.reshape(1, d), w_cat, cos_t, sin_t, bd, qnw, knw, s)

    attn = _moba(aq.reshape(b, s, ATTN_W), ak.reshape(b, s, ATTN_W), av.reshape(b, s, ATTN_W))

    gates = gt[:, :n_gate].reshape(b, s, 2, MLSTM_HEADS)
    g_col = gates.transpose(0, 3, 1, 2)
    g_row = gates.transpose(0, 3, 2, 1)
    gate_b = jnp.stack([igate_b, fgate_b]).astype(F32)
    r3 = lambda a: a.reshape(b, s, MLSTM_W)
    hm = _mlstm(r3(mq), r3(mk), r3(mv), r3(mo), g_col, g_row, gate_b,
                conv_w[:, :MLSTM_W], conv_w[:, MLSTM_W:], conv_b[:MLSTM_W].reshape(1, MLSTM_W),
                conv_b[MLSTM_W:].reshape(1, MLSTM_W), mlstm_norm_w.reshape(1, MLSTM_W))

    x2, h2, q = _merge(x2d, attn.reshape(t, ATTN_W), hm.reshape(t, MLSTM_W), ga, gm,
                       w_attn_branch.astype(BF16), w_mlstm_branch.astype(BF16), merge_b,
                       w_out.astype(BF16), ffn_norm_w.reshape(1, d), peer_w_query.astype(BF16))

    e1, na, e2, r2 = _select(q, peer_keys_1, peer_keys_2)
    wut = peer_w_up.astype(BF16).reshape(-1, EXPERT_CHUNK, d).transpose(0, 2, 1)
    out = _experts(h2, _pack_rows(peer_w_down.astype(BF16)), _pack_rows(wut), e1, na, e2, r2, x2)
    return out.reshape(b, s, d)
```

```python
import functools

import jax
import jax.numpy as jnp
from jax import lax
from jax.experimental import pallas as pl
from jax.experimental.pallas import tpu as pltpu

F32 = jnp.float32
BF16 = jnp.bfloat16

EPS = 1e-6
ATTN_HEADS = 8
ATTN_HEAD_DIM = 64
ATTN_W = ATTN_HEADS * ATTN_HEAD_DIM
ROT_DIM = ATTN_HEAD_DIM // 4
ROPE_THETA = 500000.0
MOBA_BLOCK = 256
MOBA_TOPK = 3
MLSTM_HEADS = 4
MLSTM_HEAD_DIM = 128
MLSTM_W = MLSTM_HEADS * MLSTM_HEAD_DIM
CONV_WIDTH = 4
PEER_HEADS = 8
PEER_N_KEYS = 128
PEER_HALF = 128
PEER_TOPK = 16

LANES = 128
SUBLANES = 8
NEG = -1e30
VMEM_LIMIT = 56 * 1024 * 1024

INPROJ_TM = 512
MERGE_TM = 512
MOBA_UNROLL = 4
MLSTM_CHUNK = 256
SELECT_TN = 512
EXPERT_TN = 512
EXPERT_CHUNK = 1024
MXU_PIECES = 2
UP_ROW_PIECES = 4
GATE_GROUPS = 2


def _dot(a, b):
    return jnp.dot(a, b, preferred_element_type=F32)


def _dot_nt(a, b):
    return lax.dot_general(a, b, (((1,), (1,)), ((), ())), preferred_element_type=F32)


def _dot_tn(a, b):
    return lax.dot_general(a, b, (((0,), (0,)), ((), ())), preferred_element_type=F32)


def _bf16_bits(x):
    return lax.bitcast_convert_type(x.astype(BF16).astype(F32), jnp.int32)


def _pack_pairs(lo, hi):
    return lax.shift_right_logical(_bf16_bits(lo), jnp.int32(16)) | _bf16_bits(hi)


def _as_bf16_pairs(words):
    return pltpu.bitcast(words, BF16)


def _unpack_pairs(pairs):
    words = pltpu.bitcast(pairs, jnp.int32)
    lo = lax.bitcast_convert_type(lax.shift_left(words, jnp.int32(16)), F32)
    hi = lax.bitcast_convert_type(words & jnp.int32(-65536), F32)
    return lo, hi


def _pack_rows(x):
    *lead, m, n = x.shape
    pairs = jnp.swapaxes(x.reshape(*lead, m // 2, 2, n), -1, -2)
    return lax.bitcast_convert_type(pairs, jnp.int32)


def _rows_bf16(words):
    return pltpu.bitcast(words, BF16)


def _split2(x):
    hi = x.astype(BF16)
    lo = (x - hi.astype(F32)).astype(BF16)
    return hi, lo


def _params(sem, flags=None):
    return pltpu.CompilerParams(dimension_semantics=sem, vmem_limit_bytes=VMEM_LIMIT, flags=flags)


def _inproj_kernel(x_ref, nw_ref, w_ref, cos_ref, sin_ref, bd_ref, qnw_ref, knw_ref,
                   aq_ref, ak_ref, av_ref, mq_ref, mk_ref, mv_ref, mo_ref, ga_ref, gm_ref, gt_ref):
    x = x_ref[...]
    h = x * lax.rsqrt(jnp.mean(x * x, axis=-1, keepdims=True) + EPS) * nw_ref[...]
    hb = h.astype(BF16)

    def proj(lo, width):
        return _dot(hb, w_ref[:, lo:lo + width])

    lane = lax.broadcasted_iota(jnp.int32, (x.shape[0], ATTN_W), 1) % ATTN_HEAD_DIM
    half = ROT_DIM // 2

    def qk_norm_rope(t, nw):
        hi, lo = _split2(t * t)
        ms = (_dot(hi, bd_ref[...]) + _dot(lo, bd_ref[...])) * (1.0 / ATTN_HEAD_DIM)
        tn = t * lax.rsqrt(ms + EPS) * nw
        fwd = pltpu.roll(tn, ATTN_W - half, axis=1)
        bwd = pltpu.roll(tn, half, axis=1)
        swapped = jnp.where(lane < half, fwd, jnp.where(lane < ROT_DIM, bwd, 0.0))
        return tn * cos_ref[...] + swapped * sin_ref[...]

    aq_ref[...] = qk_norm_rope(proj(0, ATTN_W), qnw_ref[...]).astype(BF16)
    ak_ref[...] = qk_norm_rope(proj(ATTN_W, ATTN_W), knw_ref[...]).astype(BF16)
    av_ref[...] = proj(2 * ATTN_W, ATTN_W).astype(BF16)
    base = 3 * ATTN_W
    mq_ref[...] = proj(base, MLSTM_W).astype(BF16)
    mk_ref[...] = proj(base + MLSTM_W, MLSTM_W).astype(BF16)
    mv_ref[...] = proj(base + 2 * MLSTM_W, MLSTM_W).astype(BF16)
    mo_ref[...] = proj(base + 3 * MLSTM_W, MLSTM_W).astype(BF16)
    base = base + 4 * MLSTM_W
    d = x.shape[1]
    ga_ref[...] = proj(base, d).astype(BF16)
    gm_ref[...] = proj(base + d, d).astype(BF16)
    gt_ref[...] = proj(base + 2 * d, LANES)


def _inproj(x2d, mix_norm_w, w_cat, cos_t, sin_t, bd, qnw, knw, seq):
    t, d = x2d.shape
    tm = INPROJ_TM
    assert t % tm == 0 and seq % tm == 0
    tiles_per_seq = seq // tm
    ncols = w_cat.shape[1]
    row = lambda w: pl.BlockSpec((tm, w), lambda i: (i, 0))
    const = lambda shape: pl.BlockSpec(shape, lambda i: (0, 0))
    pos = pl.BlockSpec((tm, ATTN_W), lambda i: (i % tiles_per_seq, 0))
    outs = [ATTN_W] * 3 + [MLSTM_W] * 4 + [d, d]
    return pl.pallas_call(
        _inproj_kernel,
        grid=(t // tm,),
        in_specs=[row(d), const((1, d)), const((d, ncols)), pos, pos, const((ATTN_W, ATTN_W)),
                  const((1, ATTN_W)), const((1, ATTN_W))],
        out_specs=[row(w) for w in outs] + [row(LANES)],
        out_shape=[jax.ShapeDtypeStruct((t, w), BF16) for w in outs]
        + [jax.ShapeDtypeStruct((t, LANES), F32)],
        compiler_params=_params(("parallel",)),
        name="inproj",
    )(x2d, mix_norm_w, w_cat, cos_t, sin_t, bd, qnw, knw)


def _moba_kernel(q_ref, k_ref, v_ref, o_ref, kmean_ref, kext_ref, vext_ref, s_ref, *, n_blocks):
    i = pl.program_id(2)
    blk = MOBA_BLOCK
    hd = ATTN_HEAD_DIM
    heads = (0, 1)
    lane = lax.broadcasted_iota(jnp.int32, (blk, LANES), 1)
    in_head = [lane < hd, lane >= hd]
    spare0 = [hd, 0]

    @pl.when(i == 0)
    def _():
        kmean_ref[...] = jnp.zeros_like(kmean_ref)
        for n in range(n_blocks):
            rows = slice(n * blk, (n + 1) * blk)
            kb = k_ref[rows, :]
            vb = v_ref[rows, :]
            kmean_ref[n:n + 1, :] = jnp.mean(kb.astype(F32), axis=0, keepdims=True)
            for h in heads:
                onehot = jnp.where(lane == spare0[h] + n, 1.0, 0.0).astype(BF16)
                kext_ref[h, rows, :] = jnp.where(in_head[h], kb, onehot)
                vext_ref[h, rows, :] = jnp.where(in_head[h], vb, jnp.ones_like(vb))

    q = q_ref[...]
    kmean_hi, kmean_lo = _split2(kmean_ref[...])
    brow = lax.broadcasted_iota(jnp.int32, (16, blk), 0)
    qext = []
    for h in heads:
        qh = jnp.where(in_head[h], q, jnp.zeros_like(q)) * jnp.asarray(hd ** -0.5, BF16)
        g_t = (_dot_nt(kmean_hi, qh) + _dot_nt(kmean_lo, qh))[0:16, :]
        rank = jnp.zeros((16, blk), F32)
        for m in range(min(n_blocks, 16)):
            gm = g_t[m:m + 1, :]
            beats = jnp.where(gm > g_t, 1.0, jnp.where(gm == g_t, jnp.where(brow > m, 1.0, 0.0), 0.0))
            rank = rank + jnp.where(m < i, beats, 0.0)
        bias_t = jnp.where(brow < i, jnp.where(rank < MOBA_TOPK, 0.0, NEG), 0.0)
        bias = jnp.concatenate([bias_t, jnp.zeros((LANES - 16, blk), F32)], axis=0).T
        if spare0[h]:
            bias = pltpu.roll(bias, spare0[h], axis=1)
        qext.append(qh + bias.astype(BF16))

    own = pl.multiple_of(i * blk, blk)
    causal = (lax.broadcasted_iota(jnp.int32, (blk, blk), 1)
              <= lax.broadcasted_iota(jnp.int32, (blk, blk), 0))
    unroll = MOBA_UNROLL

    def scores(t, m_run):
        m_new = list(m_run)
        for u in range(unroll):
            n = unroll * t + u
            start = pl.multiple_of(n * blk, blk)
            for h in heads:
                s = _dot_nt(qext[h], kext_ref[h, pl.ds(start, blk), :])
                s_ref[h, n] = s
                m_cand = jnp.maximum(m_new[h], jnp.max(s, axis=1, keepdims=True))
                m_new[h] = m_cand if u == 0 else jnp.where(n < i, m_cand, m_new[h])
        return tuple(m_new)

    m_past = lax.fori_loop(0, (i + unroll - 1) // unroll, scores,
                           tuple(jnp.full((blk, 1), NEG, F32) for _ in heads))
    m_fin = []
    for h in heads:
        s = jnp.where(causal, _dot_nt(qext[h], kext_ref[h, pl.ds(own, blk), :]), NEG)
        s_ref[h, i] = s
        m_fin.append(jnp.maximum(m_past[h], jnp.max(s, axis=1, keepdims=True)))

    def values(t, acc):
        out = list(acc)
        for u in range(unroll):
            n = unroll * t + u
            n_read = jnp.minimum(n, i)
            start = pl.multiple_of(n_read * blk, blk)
            for h in heads:
                m_eff = m_fin[h] if u == 0 else jnp.where(n <= i, m_fin[h], -NEG)
                p = jnp.exp(s_ref[h, n_read] - m_eff).astype(BF16)
                out[h] = out[h] + _dot(p, vext_ref[h, pl.ds(start, blk), :])
        return tuple(out)

    acc = lax.fori_loop(0, (i + unroll) // unroll, values,
                        tuple(jnp.zeros((blk, LANES), F32) for _ in heads))
    outs = [a / pltpu.roll(a, hd, axis=1) for a in acc]
    o_ref[...] = jnp.where(in_head[0], outs[0], outs[1]).astype(o_ref.dtype)


def _moba(q, k, v):
    b, s, w = q.shape
    assert s % MOBA_BLOCK == 0 and w % LANES == 0
    n_blocks = s // MOBA_BLOCK
    assert n_blocks <= 16 and n_blocks % MOBA_UNROLL == 0
    qspec = pl.BlockSpec((None, MOBA_BLOCK, LANES), lambda bi, hp, i: (bi, i, hp))
    kvspec = pl.BlockSpec((None, s, LANES), lambda bi, hp, i: (bi, 0, hp))
    return pl.pallas_call(
        functools.partial(_moba_kernel, n_blocks=n_blocks),
        grid=(b, w // LANES, n_blocks),
        in_specs=[qspec, kvspec, kvspec],
        out_specs=qspec,
        out_shape=jax.ShapeDtypeStruct((b, s, w), BF16),
        scratch_shapes=[pltpu.VMEM((LANES, LANES), F32),
                        pltpu.VMEM((2, s, LANES), BF16), pltpu.VMEM((2, s, LANES), BF16),
                        pltpu.VMEM((2, n_blocks, MOBA_BLOCK, MOBA_BLOCK), F32)],
        compiler_params=_params(("parallel", "parallel", "arbitrary")),
        name="moba",
    )(q, k, v)


def _log_sigmoid(f):
    return jnp.minimum(f, 0.0) - jnp.log(1.0 + jnp.exp(-jnp.abs(f)))


def _mlstm_kernel(gb_ref, mq_ref, mk_ref, mv_ref, mo_ref, gc_ref, gr_ref, cwq_ref, cwk_ref,
                  cbq_ref, cbk_ref, nw_ref, o_ref, c_ref, n_ref, m_ref):
    hh = pl.program_id(1)
    c = pl.program_id(2)
    L = MLSTM_CHUNK
    d = MLSTM_HEAD_DIM

    @pl.when(c == 0)
    def _():
        c_ref[...] = jnp.zeros_like(c_ref)
        n_ref[...] = jnp.zeros_like(n_ref)
        m_ref[...] = jnp.zeros_like(m_ref)

    start = pl.multiple_of(c * L, L)
    prev = pl.multiple_of(jnp.maximum(c - 1, 0) * L, L)
    row_d = lax.broadcasted_iota(jnp.int32, (L, d), 0)

    def conv_silu(u_ref, w_ref, b_ref):
        u = u_ref[pl.ds(start, L), :].astype(F32)
        pv = u_ref[pl.ds(prev, L), :].astype(F32)
        pv = jnp.where(c > 0, pv, 0.0)
        y = u * w_ref[CONV_WIDTH - 1:CONV_WIDTH, :] + b_ref[...]
        for k in range(1, CONV_WIDTH):
            shifted = jnp.where(row_d < k, pltpu.roll(pv, k, axis=0), pltpu.roll(u, k, axis=0))
            y = y + shifted * w_ref[CONV_WIDTH - 1 - k:CONV_WIDTH - k, :]
        return y * jax.nn.sigmoid(y)

    qc = conv_silu(mq_ref, cwq_ref, cbq_ref)
    kc = conv_silu(mk_ref, cwk_ref, cbk_ref) * (d ** -0.5)
    qb = qc.astype(BF16)
    kb = kc.astype(BF16)
    vb = mv_ref[...]

    ib = gb_ref[0, hh]
    fb = gb_ref[1, hh]
    gcol = gc_ref[...]
    grow = gr_ref[...]
    ig_col = gcol[:, 0:1] + ib
    ig_row = grow[0:1, :] + ib
    lf_col = _log_sigmoid(gcol[:, 1:2] + fb)
    lf_row = _log_sigmoid(grow[1:2, :] + fb)

    r = lax.broadcasted_iota(jnp.int32, (L, L), 0)
    s = lax.broadcasted_iota(jnp.int32, (L, L), 1)
    causal = s <= r
    tri_l = jnp.where(causal, 1.0, 0.0).astype(BF16)
    tri_u = jnp.where(r <= s, 1.0, 0.0).astype(BF16)
    lfc_hi, lfc_lo = _split2(jnp.broadcast_to(lf_col, (L, L)))
    lfr_hi, lfr_lo = _split2(jnp.broadcast_to(lf_row, (L, L)))
    b_t = _dot(tri_l, lfc_hi) + _dot(tri_l, lfc_lo)
    b_s = _dot(lfr_hi, tri_u) + _dot(lfr_lo, tri_u)

    m_prev = m_ref[...]
    b_col = b_t[:, 0:1]
    log_d = jnp.where(causal, b_t - b_s + ig_row, NEG)
    m_inter = b_col + m_prev
    m_t = jnp.maximum(m_inter, jnp.max(log_d, axis=1, keepdims=True))
    a = jnp.exp(log_d - m_t) * _dot_nt(qb, kb)
    w_inter = jnp.exp(m_inter - m_t)
    c_prev = c_ref[...]
    n_prev = n_ref[...]
    num = _dot(a.astype(BF16), vb) + w_inter * _dot(qb, c_prev.astype(BF16))
    den = jnp.sum(a, axis=1, keepdims=True) + w_inter * jnp.sum(qc * n_prev, axis=1, keepdims=True)
    h = num / jnp.maximum(jnp.abs(den), jnp.exp(-m_t))

    m_new = m_t[L - 1:L, :]
    b_last = b_col[L - 1:L, :]
    ws_col = jnp.exp(b_last - b_col + ig_col - m_new)
    decay = jnp.exp(b_last + m_prev - m_new)
    c_ref[...] = decay * c_prev + _dot_tn(kb, (ws_col * vb.astype(F32)).astype(BF16))
    n_ref[...] = decay * n_prev + jnp.sum(ws_col * kc, axis=0, keepdims=True)
    m_ref[...] = m_new

    hn = h * lax.rsqrt(jnp.mean(h * h, axis=-1, keepdims=True) + EPS) * nw_ref[...]
    o_ref[...] = (jax.nn.sigmoid(mo_ref[...].astype(F32)) * hn).astype(o_ref.dtype)


def _mlstm(mq, mk, mv, mo, g_col, g_row, gate_b, conv_wq, conv_wk, conv_bq, conv_bk, norm_w):
    b, s, w = mq.shape
    L = MLSTM_CHUNK
    d = MLSTM_HEAD_DIM
    assert s % L == 0
    seq = pl.BlockSpec((None, s, d), lambda bi, h, c: (bi, 0, h))
    chunk = pl.BlockSpec((None, L, d), lambda bi, h, c: (bi, c, h))
    per_head = lambda rows: pl.BlockSpec((rows, d), lambda bi, h, c: (0, h))
    return pl.pallas_call(
        _mlstm_kernel,
        grid=(b, w // d, s // L),
        in_specs=[pl.BlockSpec(memory_space=pltpu.SMEM), seq, seq, chunk, chunk,
                  pl.BlockSpec((None, None, L, 2), lambda bi, h, c: (bi, h, c, 0)),
                  pl.BlockSpec((None, None, 2, L), lambda bi, h, c: (bi, h, 0, c)),
                  per_head(CONV_WIDTH), per_head(CONV_WIDTH), per_head(1), per_head(1), per_head(1)],
        out_specs=chunk,
        out_shape=jax.ShapeDtypeStruct((b, s, w), BF16),
        scratch_shapes=[pltpu.VMEM((d, d), F32), pltpu.VMEM((1, d), F32), pltpu.VMEM((1, 1), F32)],
        compiler_params=_params(("parallel", "parallel", "arbitrary")),
        name="mlstm",
    )(gate_b, mq, mk, mv, mo, g_col, g_row, conv_wq, conv_wk, conv_bq, conv_bk, norm_w)


def _merge_kernel(x_ref, attn_ref, hm_ref, ga_ref, gm_ref, wab_ref, wmb_ref, mb_ref, wout_ref,
                  fnw_ref, wq_ref, x2_ref, h2_ref, q_ref):
    ya = _dot(attn_ref[...], wab_ref[...])
    ym = _dot(hm_ref[...], wmb_ref[...])
    y = (jax.nn.sigmoid(ga_ref[...].astype(F32) + mb_ref[0:1, :]) * ya
         + jax.nn.sigmoid(gm_ref[...].astype(F32) + mb_ref[1:2, :]) * ym)
    x2 = x_ref[...] + _dot(y.astype(BF16), wout_ref[...])
    x2_ref[...] = x2
    h2 = x2 * lax.rsqrt(jnp.mean(x2 * x2, axis=-1, keepdims=True) + EPS) * fnw_ref[...]
    h2b = h2.astype(BF16)
    h2_ref[...] = pltpu.bitcast(h2b, jnp.int32)
    q_ref[...] = _dot(h2b, wq_ref[...])


def _merge(x2d, attn, hm, ga, gm, wab, wmb, merge_b, wout, fnw, wq):
    t, d = x2d.shape
    tm = MERGE_TM
    nq = wq.shape[1]
    row = lambda w: pl.BlockSpec((tm, w), lambda i: (i, 0))
    const = lambda a: pl.BlockSpec(a.shape, lambda i: (0, 0))
    return pl.pallas_call(
        _merge_kernel,
        grid=(t // tm,),
        in_specs=[row(d), row(ATTN_W), row(MLSTM_W), row(d), row(d), const(wab), const(wmb),
                  const(merge_b), const(wout), const(fnw), const(wq)],
        out_specs=[row(d), pl.BlockSpec((tm // 2, d), lambda i: (i, 0)), row(nq)],
        out_shape=[jax.ShapeDtypeStruct((t, d), F32), jax.ShapeDtypeStruct((t // 2, d), jnp.int32),
                   jax.ShapeDtypeStruct((t, nq), F32)],
        compiler_params=_params(("parallel",)),
        name="merge",
    )(x2d, attn, hm, ga, gm, wab, wmb, merge_b, wout, fnw, wq)


_CAND_GROUPS = (
    ((0, 0, 0, 8),), ((0, 8, 0, 8),), ((1, 0, 0, 8),),
    ((2, 0, 0, 5), (4, 0, 5, 3)),
    ((3, 0, 0, 4), (5, 0, 4, 2), (6, 0, 6, 2)),
    ((7, 0, 0, 2),) + tuple((i, 0, i - 6, 1) for i in range(8, 14)),
    ((14, 0, 0, 1), (15, 0, 1, 1)),
)


def _select_kernel(q1_ref, q2_ref, k1_ref, k2_ref, e1_ref, na_ref, e2_ref, r2_ref,
                   s1_ref, s2_ref, w1_ref, w2_ref, rk1_ref, rk2_ref, v_ref, c_ref, e_ref):
    tn = q1_ref.shape[0]
    nk = PEER_N_KEYS
    topk = PEER_TOPK
    n_lt = tn // LANES
    lane_tiles = [slice(lt * LANES, (lt + 1) * LANES) for lt in range(n_lt)]
    halves = ((s1_ref, w1_ref, rk1_ref), (s2_ref, w2_ref, rk2_ref))

    def scores(k_ref, q_ref):
        kh, kl = _split2(k_ref[...])
        qh, ql = _split2(q_ref[...])
        return _dot_nt(kh, qh) + _dot_nt(kh, ql) + _dot_nt(kl, qh)

    s1_ref[...] = scores(k1_ref, q1_ref)
    s2_ref[...] = scores(k2_ref, q2_ref)
    key = lax.broadcasted_iota(jnp.int32, (nk, LANES), 0).astype(F32)

    def extract(break_ties):
        for s_ref, w_ref, rk_ref in halves:
            w_ref[...] = s_ref[...]
            rk_ref[...] = jnp.full(rk_ref.shape, float(topk), F32)
        for it in range(topk):
            for half, (_, w_ref, rk_ref) in enumerate(halves):
                for cols in lane_tiles:
                    s_cur = w_ref[:, cols]
                    m = jnp.max(s_cur, axis=0, keepdims=True)
                    if break_ties:
                        hit = key == jnp.min(jnp.where(s_cur == m, key, float(nk)), axis=0, keepdims=True)
                    else:
                        hit = s_cur == m
                    w_ref[:, cols] = jnp.where(hit, -jnp.inf, s_cur)
                    rk_ref[:, cols] = jnp.where(hit, float(it), rk_ref[:, cols])
                    v_ref[half, it, :, cols] = m

    extract(break_ties=False)
    off_count = jnp.zeros((1, LANES), F32)
    for _, _, rk_ref in halves:
        for cols in lane_tiles:
            taken = jnp.sum(jnp.where(rk_ref[:, cols] < float(topk), 1.0, 0.0), axis=0, keepdims=True)
            off_count = off_count + jnp.where(taken == float(topk), 0.0, 1.0)

    @pl.when(jnp.max(off_count) > 0.0)
    def _():
        extract(break_ties=True)

    row8 = lax.broadcasted_iota(jnp.int32, (SUBLANES, LANES), 0)
    seg_mask = {seg: (row8 >= seg[2]) & (row8 < seg[2] + seg[3]) for grp in _CAND_GROUPS for seg in grp}
    flats = []
    for grp in _CAND_GROUPS:
        f = jnp.full((SUBLANES, LANES), 1e9, F32)
        for seg in grp:
            i, j0, r0, _ = seg
            f = jnp.where(seg_mask[seg], (row8 + (i * topk + j0 - r0)).astype(F32), f)
        flats.append(f)
    flat = jnp.concatenate(flats, axis=0)
    for cols in lane_tiles:
        v1 = [v_ref[0, i, :, cols] for i in range(topk)]
        v2 = [v_ref[1, j, :, cols] for j in range(topk)]
        v2_groups = []
        for j0 in (0, SUBLANES):
            dense = jnp.zeros((SUBLANES, LANES), F32)
            for r in range(SUBLANES):
                dense = jnp.where(row8 == r, v2[j0 + r], dense)
            v2_groups.append(dense)
        top = v1[0] + v2[0]
        for g, grp in enumerate(_CAND_GROUPS):
            cand = jnp.full((SUBLANES, LANES), -jnp.inf, F32)
            for seg in grp:
                i, j0, r0, n = seg
                if n == 1 and j0 == 0:
                    piece = v1[i] + v2[0]
                else:
                    src = v2_groups[j0 // SUBLANES]
                    piece = v1[i] + (pltpu.roll(src, r0, axis=0) if r0 else src)
                cand = jnp.where(seg_mask[seg], piece, cand)
            rows = slice(g * SUBLANES, (g + 1) * SUBLANES)
            c_ref[rows, cols] = cand
            e_ref[rows, cols] = jnp.exp(cand - top)

    for _ in range(topk):
        for cols in lane_tiles:
            c_cur = c_ref[:, cols]
            m = jnp.max(c_cur, axis=0, keepdims=True)
            idx = jnp.min(jnp.where(c_cur == m, flat, 1e9), axis=0, keepdims=True)
            c_ref[:, cols] = jnp.where(flat == idx, -jnp.inf, c_cur)

    for lt, cols in enumerate(lane_tiles):
        chosen = jnp.where(flat < 1e8, jnp.where(c_ref[:, cols] == -jnp.inf, 1.0, 0.0), 0.0)
        z = jnp.sum(chosen * e_ref[:, cols], axis=0, keepdims=True)
        counts = [jnp.zeros((1, LANES), F32) for _ in range(topk)]
        for g, grp in enumerate(_CAND_GROUPS):
            ch = chosen[g * SUBLANES:(g + 1) * SUBLANES, :]
            for seg in grp:
                counts[seg[0]] = counts[seg[0]] + jnp.sum(
                    jnp.where(seg_mask[seg], ch, 0.0), axis=0, keepdims=True)
        v1_top = v_ref[0, 0, :, cols]
        v2_top = v_ref[1, 0, :, cols]

        r1 = rk1_ref[:, cols]
        r2 = rk2_ref[:, cols]
        n_a = jnp.zeros((nk, LANES), F32)
        for i in range(topk):
            n_a = jnp.where(r1 == float(i), counts[i], n_a)
        in1 = r1 < float(topk)
        in2 = r2 < float(topk)
        e1 = jnp.where(in1, jnp.exp(s1_ref[:, cols] - v1_top) * (0.5 / z), 0.0)
        e2 = jnp.where(in2, jnp.exp(s2_ref[:, cols] - v2_top), 0.0)
        e1_ref[:, cols] = _pack_pairs(e1, e1)
        na_ref[:, cols] = _pack_pairs(n_a, n_a)
        e2_ref[:, cols] = _pack_pairs(e2[:nk // 2], e2[nk // 2:])
        r2_ref[:, cols] = _pack_pairs(r2[:nk // 2], r2[nk // 2:])


def _select(q, keys_1, keys_2):
    t = q.shape[0]
    tn = SELECT_TN
    assert t % tn == 0
    nk = PEER_N_KEYS
    q1 = pl.BlockSpec((tn, PEER_HALF), lambda i, h: (i, 2 * h))
    q2 = pl.BlockSpec((tn, PEER_HALF), lambda i, h: (i, 2 * h + 1))
    kspec = pl.BlockSpec((None, nk, PEER_HALF), lambda i, h: (h, 0, 0))
    ospec = lambda rows: pl.BlockSpec((None, rows, tn), lambda i, h: (h, 0, i))
    oshape = lambda rows: jax.ShapeDtypeStruct((PEER_HEADS, rows, t), jnp.int32)
    keys_buf = pltpu.VMEM((nk, tn), F32)
    return pl.pallas_call(
        _select_kernel,
        grid=(t // tn, PEER_HEADS),
        in_specs=[q1, q2, kspec, kspec],
        out_specs=[ospec(nk), ospec(nk), ospec(nk // 2), ospec(nk // 2)],
        out_shape=[oshape(nk), oshape(nk), oshape(nk // 2), oshape(nk // 2)],
        scratch_shapes=[keys_buf] * 6 + [pltpu.VMEM((2, PEER_TOPK, 1, tn), F32)]
        + [pltpu.VMEM((SUBLANES * len(_CAND_GROUPS), tn), F32)] * 2,
        compiler_params=_params(("parallel", "parallel")),
        name="peer_select",
    )(q, q, keys_1, keys_2)


def _experts_kernel(h2_ref, wd_ref, wut_ref, e1_in, na_in, e2_in, r2_in, x2_ref, o_ref,
                    acc_ref, z0_ref, z1_ref, act0_ref, act1_ref, e1_ref, na_ref, e2_ref, r2_ref):
    s = pl.program_id(1)
    n_chunks = pl.num_programs(1) - 2
    nk = PEER_N_KEYS
    tn = z0_ref.shape[1]
    rows_per_chunk = EXPERT_CHUNK // nk

    @pl.when(s == 0)
    def _():
        for ref in (acc_ref, z0_ref, z1_ref, act0_ref, act1_ref):
            ref[...] = jnp.zeros_like(ref)
        e2_ref[...] = e2_in[...]
        r2_ref[...] = r2_in[...]

    chunk = jnp.clip(s - 1, 0, n_chunks - 1)
    r0 = (chunk % (SUBLANES // rows_per_chunk)) * rows_per_chunk
    for h in range(PEER_HEADS):
        for aa in range(rows_per_chunk):
            e1_ref[h, aa:aa + 1, :] = e1_in[h, pl.ds(r0 + aa, 1), :]
            na_ref[h, aa:aa + 1, :] = na_in[h, pl.ds(r0 + aa, 1), :]

    half = nk // 2
    groups = half // SUBLANES

    def gate_piece(lt, k_lo, k_hi, z_r, act_w):
        cols = slice(lt * LANES, (lt + 1) * LANES)
        g = [[jnp.zeros((2 * SUBLANES, LANES), BF16) for _ in range(k_lo, k_hi)]
             for _ in range(rows_per_chunk)]
        for h in range(PEER_HEADS):
            as_pair = lambda ref, aa: _as_bf16_pairs(
                jnp.broadcast_to(ref[h, aa:aa + 1, cols], (SUBLANES, LANES)))
            e1 = [as_pair(e1_ref, aa) for aa in range(rows_per_chunk)]
            n_a = [as_pair(na_ref, aa) for aa in range(rows_per_chunk)]
            for k in range(k_lo, k_hi):
                grp = slice(k * SUBLANES, (k + 1) * SUBLANES)
                r2 = _as_bf16_pairs(r2_ref[h, grp, cols])
                e2 = _as_bf16_pairs(e2_ref[h, grp, cols])
                for aa in range(rows_per_chunk):
                    g[aa][k - k_lo] = (g[aa][k - k_lo]
                                       + jnp.where(r2 < n_a[aa], e2, jnp.zeros_like(e2)) * e1[aa])
        for aa in range(rows_per_chunk):
            for k in range(k_lo, k_hi):
                g_lo, g_hi = _unpack_pairs(g[aa][k - k_lo])
                for g_half, base in ((g_lo, aa * nk), (g_hi, aa * nk + half)):
                    rows = slice(base + k * SUBLANES, base + (k + 1) * SUBLANES)
                    z = z_r[rows, cols]
                    act = z * (1.0 + lax.erf(z * (2.0 ** -0.5)))
                    act_w[rows, cols] = (act * g_half).astype(BF16)

    def step(z_w, z_r, act_w, act_r):
        d = acc_ref.shape[0]
        mxu = []
        for p in range(MXU_PIECES):
            pc = slice(p * tn // MXU_PIECES, (p + 1) * tn // MXU_PIECES)
            pw = slice(pc.start // 2, pc.stop // 2)

            def z_piece(pc=pc, pw=pw):
                z_w[:, pc] = _dot_nt(_rows_bf16(wd_ref[...]), _rows_bf16(h2_ref[pw, :]))

            mxu.append(z_piece)
            for r in range(UP_ROW_PIECES):
                rs = slice(r * d // UP_ROW_PIECES, (r + 1) * d // UP_ROW_PIECES)
                rw = slice(rs.start // 2, rs.stop // 2)

                def up_piece(pc=pc, rs=rs, rw=rw):
                    acc_ref[rs, pc] += _dot(_rows_bf16(wut_ref[rw, :]), act_r[:, pc])

                mxu.append(up_piece)
        gates = [(lt, k, k + GATE_GROUPS) for lt in range(tn // LANES)
                 for k in range(0, groups, GATE_GROUPS)]
        per_mxu = -(-len(gates) // len(mxu))
        for m, piece in enumerate(mxu):
            piece()
            for lt, k_lo, k_hi in gates[m * per_mxu:(m + 1) * per_mxu]:
                gate_piece(lt, k_lo, k_hi, z_r, act_w)
        for lt, k_lo, k_hi in gates[len(mxu) * per_mxu:]:
            gate_piece(lt, k_lo, k_hi, z_r, act_w)

    @pl.when(s % 2 == 0)
    def _():
        step(z0_ref, z1_ref, act1_ref, act0_ref)

    @pl.when(s % 2 == 1)
    def _():
        step(z1_ref, z0_ref, act0_ref, act1_ref)

    @pl.when(s == pl.num_programs(1) - 1)
    def _():
        o_ref[...] = x2_ref[...] + acc_ref[...].T


def _experts(h2, wd, wut, e1, na, e2, r2, x2):
    t, d = x2.shape
    tn = EXPERT_TN
    ne = 2 * wd.shape[0]
    nk = PEER_N_KEYS
    assert t % tn == 0 and ne % EXPERT_CHUNK == 0 and EXPERT_CHUNK % nk == 0
    n_chunks = ne // EXPERT_CHUNK
    assert n_chunks % 2 == 0
    tok = pl.BlockSpec((tn, d), lambda i, s: (i, 0))
    gate = pl.BlockSpec((PEER_HEADS, nk // 2, tn), lambda i, s: (0, 0, i))
    rpc = EXPERT_CHUNK // nk
    assert SUBLANES % rpc == 0
    chunks_per_group = SUBLANES // rpc
    row = pl.BlockSpec((PEER_HEADS, None, SUBLANES, tn),
                       lambda i, s: (0, jnp.clip(s - 1, 0, n_chunks - 1) // chunks_per_group, 0, i))
    e1 = e1.reshape(PEER_HEADS, nk // SUBLANES, SUBLANES, t)
    na = na.reshape(PEER_HEADS, nk // SUBLANES, SUBLANES, t)
    return pl.pallas_call(
        _experts_kernel,
        grid=(t // tn, n_chunks + 2),
        in_specs=[pl.BlockSpec((tn // 2, d), lambda i, s: (i, 0)),
                  pl.BlockSpec((EXPERT_CHUNK // 2, d), lambda i, s: (jnp.minimum(s, n_chunks - 1), 0)),
                  pl.BlockSpec((None, d // 2, EXPERT_CHUNK),
                               lambda i, s: (jnp.clip(s - 2, 0, n_chunks - 1), 0, 0)),
                  row, row, gate, gate, tok],
        out_specs=tok,
        out_shape=jax.ShapeDtypeStruct((t, d), F32),
        scratch_shapes=[pltpu.VMEM((d, tn), F32),
                        pltpu.VMEM((EXPERT_CHUNK, tn), F32), pltpu.VMEM((EXPERT_CHUNK, tn), F32),
                        pltpu.VMEM((EXPERT_CHUNK, tn), BF16), pltpu.VMEM((EXPERT_CHUNK, tn), BF16),
                        pltpu.VMEM((PEER_HEADS, rpc, tn), jnp.int32), pltpu.VMEM((PEER_HEADS, rpc, tn), jnp.int32),
                        pltpu.VMEM((PEER_HEADS, nk // 2, tn), jnp.int32),
                        pltpu.VMEM((PEER_HEADS, nk // 2, tn), jnp.int32)],
        compiler_params=_params(("parallel", "arbitrary")),
        name="peer_experts",
    )(h2, wd, wut, e1, na, e2, r2, x2)


def _rope_tables(seq):
    pos = jnp.arange(seq, dtype=F32)
    inv_freq = ROPE_THETA ** (-jnp.arange(0, ROT_DIM, 2, dtype=F32) / ROT_DIM)
    ang = pos[:, None] * inv_freq[None, :]
    cos, sin = jnp.cos(ang), jnp.sin(ang)
    rest = ATTN_HEAD_DIM - ROT_DIM
    cos_h = jnp.concatenate([cos, cos, jnp.ones((seq, rest), F32)], axis=-1)
    sin_h = jnp.concatenate([-sin, sin, jnp.zeros((seq, rest), F32)], axis=-1)
    return jnp.tile(cos_h, (1, ATTN_HEADS)), jnp.tile(sin_h, (1, ATTN_HEADS))


def kernel(x, mix_norm_w, w_in, q_norm_w, k_norm_w, conv_w, conv_b, igate_b, fgate_b, mlstm_norm_w,
           w_attn_branch, w_mlstm_branch, merge_b, w_out, ffn_norm_w, peer_w_query, peer_keys_1,
           peer_keys_2, peer_w_down, peer_w_up):
    b, s, d = x.shape
    t = b * s
    x2d = x.reshape(t, d)

    n_main = 3 * ATTN_W + 4 * MLSTM_W
    n_gate = 2 * MLSTM_HEADS
    w_cat = jnp.concatenate(
        [w_in[:, :n_main], w_in[:, n_main + n_gate:], w_in[:, n_main:n_main + n_gate],
         jnp.zeros((d, LANES - n_gate), w_in.dtype)], axis=1).astype(BF16)
    cos_t, sin_t = _rope_tables(s)
    head_of = jnp.arange(ATTN_W) // ATTN_HEAD_DIM
    bd = (head_of[:, None] == head_of[None, :]).astype(BF16)
    qnw = jnp.tile(q_norm_w, ATTN_HEADS).reshape(1, ATTN_W)
    knw = jnp.tile(k_norm_w, ATTN_HEADS).reshape(1, ATTN_W)

    aq, ak, av, mq, mk, mv, mo, ga, gm, gt = _inproj(
        x2d, mix_norm_w.reshape(1, d), w_cat, cos_t, sin_t, bd, qnw, knw, s)

    attn = _moba(aq.reshape(b, s, ATTN_W), ak.reshape(b, s, ATTN_W), av.reshape(b, s, ATTN_W))

    gates = gt[:, :n_gate].reshape(b, s, 2, MLSTM_HEADS)
    g_col = gates.transpose(0, 3, 1, 2)
    g_row = gates.transpose(0, 3, 2, 1)
    gate_b = jnp.stack([igate_b, fgate_b]).astype(F32)
    r3 = lambda a: a.reshape(b, s, MLSTM_W)
    hm = _mlstm(r3(mq), r3(mk), r3(mv), r3(mo), g_col, g_row, gate_b,
                conv_w[:, :MLSTM_W], conv_w[:, MLSTM_W:], conv_b[:MLSTM_W].reshape(1, MLSTM_W),
                conv_b[MLSTM_W:].reshape(1, MLSTM_W), mlstm_norm_w.reshape(1, MLSTM_W))

    x2, h2, q = _merge(x2d, attn.reshape(t, ATTN_W), hm.reshape(t, MLSTM_W), ga, gm,
                       w_attn_branch.astype(BF16), w_mlstm_branch.astype(BF16), merge_b,
                       w_out.astype(BF16), ffn_norm_w.reshape(1, d), peer_w_query.astype(BF16))

    e1, na, e2, r2 = _select(q, peer_keys_1, peer_keys_2)
    wut = peer_w_up.astype(BF16).reshape(-1, EXPERT_CHUNK, d).transpose(0, 2, 1)
    out = _experts(h2, _pack_rows(peer_w_down.astype(BF16)), _pack_rows(wut), e1, na, e2, r2, x2)
    return out.reshape(b, s, d)
```

```python
import functools

import jax
import jax.numpy as jnp
from jax import lax
from jax.experimental import pallas as pl
from jax.experimental.pallas import tpu as pltpu

F32 = jnp.float32
BF16 = jnp.bfloat16

EPS = 1e-6
ATTN_HEADS = 8
ATTN_HEAD_DIM = 64
ATTN_W = ATTN_HEADS * ATTN_HEAD_DIM
ROT_DIM = ATTN_HEAD_DIM // 4
ROPE_THETA = 500000.0
MOBA_BLOCK = 256
MOBA_TOPK = 3
MLSTM_HEADS = 4
MLSTM_HEAD_DIM = 128
MLSTM_W = MLSTM_HEADS * MLSTM_HEAD_DIM
CONV_WIDTH = 4
PEER_HEADS = 8
PEER_N_KEYS = 128
PEER_HALF = 128
PEER_TOPK = 16

LANES = 128
SUBLANES = 8
NEG = -1e30
VMEM_LIMIT = 56 * 1024 * 1024

INPROJ_TM = 512
MERGE_TM = 512
MOBA_UNROLL = 4
MLSTM_CHUNK = 256
MLSTM_HEADS_PER_STEP = 4
SELECT_TN = 512
EXPERT_TN = 512
EXPERT_CHUNK = 1024
MXU_PIECES = 2
UP_ROW_PIECES = 4
GATE_GROUPS = 4


def _dot(a, b):
    return jnp.dot(a, b, preferred_element_type=F32)


def _dot_nt(a, b):
    return lax.dot_general(a, b, (((1,), (1,)), ((), ())), preferred_element_type=F32)


def _dot_tn(a, b):
    return lax.dot_general(a, b, (((0,), (0,)), ((), ())), preferred_element_type=F32)


def _bf16_bits(x):
    return lax.bitcast_convert_type(x.astype(BF16).astype(F32), jnp.int32)


def _pack_pairs(lo, hi):
    return lax.shift_right_logical(_bf16_bits(lo), jnp.int32(16)) | _bf16_bits(hi)


def _as_bf16_pairs(words):
    return pltpu.bitcast(words, BF16)


def _unpack_pairs(pairs):
    words = pltpu.bitcast(pairs, jnp.int32)
    lo = lax.bitcast_convert_type(lax.shift_left(words, jnp.int32(16)), F32)
    hi = lax.bitcast_convert_type(words & jnp.int32(-65536), F32)
    return lo, hi


def _pack_rows(x):
    *lead, m, n = x.shape
    pairs = jnp.swapaxes(x.reshape(*lead, m // 2, 2, n), -1, -2)
    return lax.bitcast_convert_type(pairs, jnp.int32)


def _rows_bf16(words):
    return pltpu.bitcast(words, BF16)


def _split2(x):
    hi = x.astype(BF16)
    lo = (x - hi.astype(F32)).astype(BF16)
    return hi, lo


def _params(sem, flags=None):
    return pltpu.CompilerParams(dimension_semantics=sem, vmem_limit_bytes=VMEM_LIMIT, flags=flags)


def _inproj_kernel(x_ref, nw_ref, w_ref, cos_ref, sin_ref, bd_ref, qnw_ref, knw_ref,
                   aq_ref, ak_ref, av_ref, mq_ref, mk_ref, mv_ref, mo_ref, ga_ref, gm_ref, gt_ref):
    x = x_ref[...]
    h = x * lax.rsqrt(jnp.mean(x * x, axis=-1, keepdims=True) + EPS) * nw_ref[...]
    hb = h.astype(BF16)

    def proj(lo, width):
        return _dot(hb, w_ref[:, lo:lo + width])

    lane = lax.broadcasted_iota(jnp.int32, (x.shape[0], ATTN_W), 1) % ATTN_HEAD_DIM
    half = ROT_DIM // 2

    def qk_norm_rope(t, nw):
        hi, lo = _split2(t * t)
        ms = (_dot(hi, bd_ref[...]) + _dot(lo, bd_ref[...])) * (1.0 / ATTN_HEAD_DIM)
        tn = t * lax.rsqrt(ms + EPS) * nw
        fwd = pltpu.roll(tn, ATTN_W - half, axis=1)
        bwd = pltpu.roll(tn, half, axis=1)
        swapped = jnp.where(lane < half, fwd, jnp.where(lane < ROT_DIM, bwd, 0.0))
        return tn * cos_ref[...] + swapped * sin_ref[...]

    aq_ref[...] = qk_norm_rope(proj(0, ATTN_W), qnw_ref[...]).astype(BF16)
    ak_ref[...] = qk_norm_rope(proj(ATTN_W, ATTN_W), knw_ref[...]).astype(BF16)
    av_ref[...] = proj(2 * ATTN_W, ATTN_W).astype(BF16)
    base = 3 * ATTN_W
    mq_ref[...] = proj(base, MLSTM_W).astype(BF16)
    mk_ref[...] = proj(base + MLSTM_W, MLSTM_W).astype(BF16)
    mv_ref[...] = proj(base + 2 * MLSTM_W, MLSTM_W).astype(BF16)
    mo_ref[...] = proj(base + 3 * MLSTM_W, MLSTM_W).astype(BF16)
    base = base + 4 * MLSTM_W
    d = x.shape[1]
    ga_ref[...] = proj(base, d).astype(BF16)
    gm_ref[...] = proj(base + d, d).astype(BF16)
    gt_ref[...] = proj(base + 2 * d, LANES)


def _inproj(x2d, mix_norm_w, w_cat, cos_t, sin_t, bd, qnw, knw, seq):
    t, d = x2d.shape
    tm = INPROJ_TM
    assert t % tm == 0 and seq % tm == 0
    tiles_per_seq = seq // tm
    ncols = w_cat.shape[1]
    row = lambda w: pl.BlockSpec((tm, w), lambda i: (i, 0))
    const = lambda shape: pl.BlockSpec(shape, lambda i: (0, 0))
    pos = pl.BlockSpec((tm, ATTN_W), lambda i: (i % tiles_per_seq, 0))
    outs = [ATTN_W] * 3 + [MLSTM_W] * 4 + [d, d]
    return pl.pallas_call(
        _inproj_kernel,
        grid=(t // tm,),
        in_specs=[row(d), const((1, d)), const((d, ncols)), pos, pos, const((ATTN_W, ATTN_W)),
                  const((1, ATTN_W)), const((1, ATTN_W))],
        out_specs=[row(w) for w in outs] + [row(LANES)],
        out_shape=[jax.ShapeDtypeStruct((t, w), BF16) for w in outs]
        + [jax.ShapeDtypeStruct((t, LANES), F32)],
        compiler_params=_params(("parallel",)),
        name="inproj",
    )(x2d, mix_norm_w, w_cat, cos_t, sin_t, bd, qnw, knw)


def _moba_kernel(q_ref, k_ref, v_ref, o_ref, kmean_ref, kext_ref, vext_ref, s_ref, *, n_blocks):
    i = pl.program_id(2)
    blk = MOBA_BLOCK
    hd = ATTN_HEAD_DIM
    heads = (0, 1)
    lane = lax.broadcasted_iota(jnp.int32, (blk, LANES), 1)
    in_head = [lane < hd, lane >= hd]
    spare0 = [hd, 0]

    @pl.when(i == 0)
    def _():
        kmean_ref[...] = jnp.zeros_like(kmean_ref)
        for n in range(n_blocks):
            rows = slice(n * blk, (n + 1) * blk)
            kb = k_ref[rows, :]
            vb = v_ref[rows, :]
            kmean_ref[n:n + 1, :] = jnp.mean(kb.astype(F32), axis=0, keepdims=True)
            for h in heads:
                onehot = jnp.where(lane == spare0[h] + n, 1.0, 0.0).astype(BF16)
                kext_ref[h, rows, :] = jnp.where(in_head[h], kb, onehot)
                vext_ref[h, rows, :] = jnp.where(in_head[h], vb, jnp.ones_like(vb))

    q = q_ref[...]
    kmean_hi, kmean_lo = _split2(kmean_ref[...])
    brow = lax.broadcasted_iota(jnp.int32, (16, blk), 0)
    qext = []
    for h in heads:
        qh = jnp.where(in_head[h], q, jnp.zeros_like(q)) * jnp.asarray(hd ** -0.5, BF16)
        g_t = (_dot_nt(kmean_hi, qh) + _dot_nt(kmean_lo, qh))[0:16, :]
        rank = jnp.zeros((16, blk), F32)
        for m in range(min(n_blocks, 16)):
            gm = g_t[m:m + 1, :]
            beats = jnp.where(gm > g_t, 1.0, jnp.where(gm == g_t, jnp.where(brow > m, 1.0, 0.0), 0.0))
            rank = rank + jnp.where(m < i, beats, 0.0)
        bias_t = jnp.where(brow < i, jnp.where(rank < MOBA_TOPK, 0.0, NEG), 0.0)
        bias = jnp.concatenate([bias_t, jnp.zeros((LANES - 16, blk), F32)], axis=0).T
        if spare0[h]:
            bias = pltpu.roll(bias, spare0[h], axis=1)
        qext.append(qh + bias.astype(BF16))

    own = pl.multiple_of(i * blk, blk)
    causal = (lax.broadcasted_iota(jnp.int32, (blk, blk), 1)
              <= lax.broadcasted_iota(jnp.int32, (blk, blk), 0))
    unroll = MOBA_UNROLL

    def scores(t, m_run):
        m_new = list(m_run)
        for u in range(unroll):
            n = unroll * t + u
            start = pl.multiple_of(n * blk, blk)
            for h in heads:
                s = _dot_nt(qext[h], kext_ref[h, pl.ds(start, blk), :])
                s_ref[h, n] = s
                m_cand = jnp.maximum(m_new[h], jnp.max(s, axis=1, keepdims=True))
                m_new[h] = m_cand if u == 0 else jnp.where(n < i, m_cand, m_new[h])
        return tuple(m_new)

    m_past = lax.fori_loop(0, (i + unroll - 1) // unroll, scores,
                           tuple(jnp.full((blk, 1), NEG, F32) for _ in heads))
    m_fin = []
    for h in heads:
        s = jnp.where(causal, _dot_nt(qext[h], kext_ref[h, pl.ds(own, blk), :]), NEG)
        s_ref[h, i] = s
        m_fin.append(jnp.maximum(m_past[h], jnp.max(s, axis=1, keepdims=True)))

    def values(t, acc):
        out = list(acc)
        for u in range(unroll):
            n = unroll * t + u
            n_read = jnp.minimum(n, i)
            start = pl.multiple_of(n_read * blk, blk)
            for h in heads:
                m_eff = m_fin[h] if u == 0 else jnp.where(n <= i, m_fin[h], -NEG)
                p = jnp.exp(s_ref[h, n_read] - m_eff).astype(BF16)
                out[h] = out[h] + _dot(p, vext_ref[h, pl.ds(start, blk), :])
        return tuple(out)

    acc = lax.fori_loop(0, (i + unroll) // unroll, values,
                        tuple(jnp.zeros((blk, LANES), F32) for _ in heads))
    outs = [a / pltpu.roll(a, hd, axis=1) for a in acc]
    o_ref[...] = jnp.where(in_head[0], outs[0], outs[1]).astype(o_ref.dtype)


def _moba(q, k, v):
    b, s, w = q.shape
    assert s % MOBA_BLOCK == 0 and w % LANES == 0
    n_blocks = s // MOBA_BLOCK
    assert n_blocks <= 16 and n_blocks % MOBA_UNROLL == 0
    qspec = pl.BlockSpec((None, MOBA_BLOCK, LANES), lambda bi, hp, i: (bi, i, hp))
    kvspec = pl.BlockSpec((None, s, LANES), lambda bi, hp, i: (bi, 0, hp))
    return pl.pallas_call(
        functools.partial(_moba_kernel, n_blocks=n_blocks),
        grid=(b, w // LANES, n_blocks),
        in_specs=[qspec, kvspec, kvspec],
        out_specs=qspec,
        out_shape=jax.ShapeDtypeStruct((b, s, w), BF16),
        scratch_shapes=[pltpu.VMEM((LANES, LANES), F32),
                        pltpu.VMEM((2, s, LANES), BF16), pltpu.VMEM((2, s, LANES), BF16),
                        pltpu.VMEM((2, n_blocks, MOBA_BLOCK, MOBA_BLOCK), F32)],
        compiler_params=_params(("parallel", "parallel", "arbitrary")),
        name="moba",
    )(q, k, v)


def _log_sigmoid(f):
    return jnp.minimum(f, 0.0) - jnp.log(1.0 + jnp.exp(-jnp.abs(f)))


def _mlstm_kernel(gb_ref, mq_ref, mk_ref, mv_ref, mo_ref, gc_ref, gr_ref, cwq_ref, cwk_ref,
                  cbq_ref, cbk_ref, nw_ref, o_ref, c_ref, n_ref, m_ref):
    group = pl.program_id(1)
    c = pl.program_id(2)
    L = MLSTM_CHUNK
    d = MLSTM_HEAD_DIM
    heads = range(MLSTM_HEADS_PER_STEP)
    lanes = [slice(h * d, (h + 1) * d) for h in heads]

    @pl.when(c == 0)
    def _():
        c_ref[...] = jnp.zeros_like(c_ref)
        n_ref[...] = jnp.zeros_like(n_ref)
        m_ref[...] = jnp.zeros_like(m_ref)

    start = pl.multiple_of(c * L, L)
    prev = pl.multiple_of(jnp.maximum(c - 1, 0) * L, L)
    row_d = lax.broadcasted_iota(jnp.int32, (L, d), 0)

    def conv_silu(u_ref, w_ref, b_ref, ls):
        u = u_ref[pl.ds(start, L), ls].astype(F32)
        pv = u_ref[pl.ds(prev, L), ls].astype(F32)
        pv = jnp.where(c > 0, pv, 0.0)
        y = u * w_ref[CONV_WIDTH - 1:CONV_WIDTH, ls] + b_ref[:, ls]
        for k in range(1, CONV_WIDTH):
            shifted = jnp.where(row_d < k, pltpu.roll(pv, k, axis=0), pltpu.roll(u, k, axis=0))
            y = y + shifted * w_ref[CONV_WIDTH - 1 - k:CONV_WIDTH - k, ls]
        return y * jax.nn.sigmoid(y)

    qc = [conv_silu(mq_ref, cwq_ref, cbq_ref, ls) for ls in lanes]
    kc = [conv_silu(mk_ref, cwk_ref, cbk_ref, ls) * (d ** -0.5) for ls in lanes]
    qb = [x.astype(BF16) for x in qc]
    kb = [x.astype(BF16) for x in kc]
    vb = [mv_ref[:, ls] for ls in lanes]

    ib = [gb_ref[0, group * MLSTM_HEADS_PER_STEP + h] for h in heads]
    fb = [gb_ref[1, group * MLSTM_HEADS_PER_STEP + h] for h in heads]
    gcol = [gc_ref[h] for h in heads]
    grow = [gr_ref[h] for h in heads]
    ig_col = [gcol[h][:, 0:1] + ib[h] for h in heads]
    ig_row = [grow[h][0:1, :] + ib[h] for h in heads]
    lf_col = [_log_sigmoid(gcol[h][:, 1:2] + fb[h]) for h in heads]
    lf_row = [_log_sigmoid(grow[h][1:2, :] + fb[h]) for h in heads]

    r = lax.broadcasted_iota(jnp.int32, (L, L), 0)
    s = lax.broadcasted_iota(jnp.int32, (L, L), 1)
    causal = s <= r
    tri_l = jnp.where(causal, 1.0, 0.0).astype(BF16)
    tri_u = jnp.where(r <= s, 1.0, 0.0).astype(BF16)
    lfc = [_split2(jnp.broadcast_to(lf_col[h], (L, L))) for h in heads]
    lfr = [_split2(jnp.broadcast_to(lf_row[h], (L, L))) for h in heads]
    b_t = [_dot(tri_l, lfc[h][0]) + _dot(tri_l, lfc[h][1]) for h in heads]
    b_s = [_dot(lfr[h][0], tri_u) + _dot(lfr[h][1], tri_u) for h in heads]

    m_prev = [m_ref[h] for h in heads]
    b_col = [b_t[h][:, 0:1] for h in heads]
    log_d = [jnp.where(causal, b_t[h] - b_s[h] + ig_row[h], NEG) for h in heads]
    m_inter = [b_col[h] + m_prev[h] for h in heads]
    m_t = [jnp.maximum(m_inter[h], jnp.max(log_d[h], axis=1, keepdims=True)) for h in heads]
    a = [jnp.exp(log_d[h] - m_t[h]) * _dot_nt(qb[h], kb[h]) for h in heads]
    w_inter = [jnp.exp(m_inter[h] - m_t[h]) for h in heads]
    c_prev = [c_ref[h] for h in heads]
    n_prev = [n_ref[h] for h in heads]
    num = [_dot(a[h].astype(BF16), vb[h]) + w_inter[h] * _dot(qb[h], c_prev[h].astype(BF16)) for h in heads]
    den = [jnp.sum(a[h], axis=1, keepdims=True)
           + w_inter[h] * jnp.sum(qc[h] * n_prev[h], axis=1, keepdims=True) for h in heads]
    hs = [num[h] / jnp.maximum(jnp.abs(den[h]), jnp.exp(-m_t[h])) for h in heads]

    m_new = [m_t[h][L - 1:L, :] for h in heads]
    b_last = [b_col[h][L - 1:L, :] for h in heads]
    ws_col = [jnp.exp(b_last[h] - b_col[h] + ig_col[h] - m_new[h]) for h in heads]
    decay = [jnp.exp(b_last[h] + m_prev[h] - m_new[h]) for h in heads]
    for h in heads:
        c_ref[h] = decay[h] * c_prev[h] + _dot_tn(kb[h], (ws_col[h] * vb[h].astype(F32)).astype(BF16))
        n_ref[h] = decay[h] * n_prev[h] + jnp.sum(ws_col[h] * kc[h], axis=0, keepdims=True)
        m_ref[h] = m_new[h]

    for h, ls in zip(heads, lanes):
        hn = hs[h] * lax.rsqrt(jnp.mean(hs[h] * hs[h], axis=-1, keepdims=True) + EPS) * nw_ref[:, ls]
        o_ref[:, ls] = (jax.nn.sigmoid(mo_ref[:, ls].astype(F32)) * hn).astype(o_ref.dtype)


def _mlstm(mq, mk, mv, mo, g_col, g_row, gate_b, conv_wq, conv_wk, conv_bq, conv_bk, norm_w):
    b, s, w = mq.shape
    L = MLSTM_CHUNK
    d = MLSTM_HEAD_DIM
    hps = MLSTM_HEADS_PER_STEP
    assert s % L == 0 and (w // d) % hps == 0
    seq = pl.BlockSpec((None, s, hps * d), lambda bi, g, c: (bi, 0, g))
    chunk = pl.BlockSpec((None, L, hps * d), lambda bi, g, c: (bi, c, g))
    per_group = lambda rows: pl.BlockSpec((rows, hps * d), lambda bi, g, c: (0, g))
    return pl.pallas_call(
        _mlstm_kernel,
        grid=(b, w // d // hps, s // L),
        in_specs=[pl.BlockSpec(memory_space=pltpu.SMEM), seq, seq, chunk, chunk,
                  pl.BlockSpec((None, hps, L, 2), lambda bi, g, c: (bi, g, c, 0)),
                  pl.BlockSpec((None, hps, 2, L), lambda bi, g, c: (bi, g, 0, c)),
                  per_group(CONV_WIDTH), per_group(CONV_WIDTH), per_group(1), per_group(1), per_group(1)],
        out_specs=chunk,
        out_shape=jax.ShapeDtypeStruct((b, s, w), BF16),
        scratch_shapes=[pltpu.VMEM((hps, d, d), F32), pltpu.VMEM((hps, 1, d), F32),
                        pltpu.VMEM((hps, 1, 1), F32)],
        compiler_params=_params(("parallel", "parallel", "arbitrary")),
        name="mlstm",
    )(gate_b, mq, mk, mv, mo, g_col, g_row, conv_wq, conv_wk, conv_bq, conv_bk, norm_w)


def _merge_kernel(x_ref, attn_ref, hm_ref, ga_ref, gm_ref, wab_ref, wmb_ref, mb_ref, wout_ref,
                  fnw_ref, wq_ref, x2_ref, h2_ref, q_ref):
    ya = _dot(attn_ref[...], wab_ref[...])
    ym = _dot(hm_ref[...], wmb_ref[...])
    y = (jax.nn.sigmoid(ga_ref[...].astype(F32) + mb_ref[0:1, :]) * ya
         + jax.nn.sigmoid(gm_ref[...].astype(F32) + mb_ref[1:2, :]) * ym)
    x2 = x_ref[...] + _dot(y.astype(BF16), wout_ref[...])
    x2_ref[...] = x2
    h2 = x2 * lax.rsqrt(jnp.mean(x2 * x2, axis=-1, keepdims=True) + EPS) * fnw_ref[...]
    h2b = h2.astype(BF16)
    h2_ref[...] = pltpu.bitcast(h2b, jnp.int32)
    q_ref[...] = _dot(h2b, wq_ref[...])


def _merge(x2d, attn, hm, ga, gm, wab, wmb, merge_b, wout, fnw, wq):
    t, d = x2d.shape
    tm = MERGE_TM
    nq = wq.shape[1]
    row = lambda w: pl.BlockSpec((tm, w), lambda i: (i, 0))
    const = lambda a: pl.BlockSpec(a.shape, lambda i: (0, 0))
    return pl.pallas_call(
        _merge_kernel,
        grid=(t // tm,),
        in_specs=[row(d), row(ATTN_W), row(MLSTM_W), row(d), row(d), const(wab), const(wmb),
                  const(merge_b), const(wout), const(fnw), const(wq)],
        out_specs=[row(d), pl.BlockSpec((tm // 2, d), lambda i: (i, 0)), row(nq)],
        out_shape=[jax.ShapeDtypeStruct((t, d), F32), jax.ShapeDtypeStruct((t // 2, d), jnp.int32),
                   jax.ShapeDtypeStruct((t, nq), F32)],
        compiler_params=_params(("parallel",)),
        name="merge",
    )(x2d, attn, hm, ga, gm, wab, wmb, merge_b, wout, fnw, wq)


_CAND_GROUPS = (
    ((0, 0, 0, 8),), ((0, 8, 0, 8),), ((1, 0, 0, 8),),
    ((2, 0, 0, 5), (4, 0, 5, 3)),
    ((3, 0, 0, 4), (5, 0, 4, 2), (6, 0, 6, 2)),
    ((7, 0, 0, 2),) + tuple((i, 0, i - 6, 1) for i in range(8, 14)),
    ((14, 0, 0, 1), (15, 0, 1, 1)),
)


def _select_kernel(q1_ref, q2_ref, k1_ref, k2_ref, e1_ref, na_ref, e2_ref, r2_ref,
                   s1_ref, s2_ref, w1_ref, w2_ref, rk1_ref, rk2_ref, v_ref, c_ref, e_ref):
    tn = q1_ref.shape[0]
    nk = PEER_N_KEYS
    topk = PEER_TOPK
    n_lt = tn // LANES
    lane_tiles = [slice(lt * LANES, (lt + 1) * LANES) for lt in range(n_lt)]
    halves = ((s1_ref, w1_ref, rk1_ref), (s2_ref, w2_ref, rk2_ref))

    def scores(k_ref, q_ref):
        kh, kl = _split2(k_ref[...])
        qh, ql = _split2(q_ref[...])
        return _dot_nt(kh, qh) + _dot_nt(kh, ql) + _dot_nt(kl, qh)

    s1_ref[...] = scores(k1_ref, q1_ref)
    s2_ref[...] = scores(k2_ref, q2_ref)
    key = lax.broadcasted_iota(jnp.int32, (nk, LANES), 0).astype(F32)

    def extract(break_ties):
        for s_ref, w_ref, rk_ref in halves:
            w_ref[...] = s_ref[...]
            rk_ref[...] = jnp.full(rk_ref.shape, float(topk), F32)
        for it in range(topk):
            for half, (_, w_ref, rk_ref) in enumerate(halves):
                for cols in lane_tiles:
                    s_cur = w_ref[:, cols]
                    m = jnp.max(s_cur, axis=0, keepdims=True)
                    if break_ties:
                        hit = key == jnp.min(jnp.where(s_cur == m, key, float(nk)), axis=0, keepdims=True)
                    else:
                        hit = s_cur == m
                    w_ref[:, cols] = jnp.where(hit, -jnp.inf, s_cur)
                    rk_ref[:, cols] = jnp.where(hit, float(it), rk_ref[:, cols])
                    v_ref[half, it, :, cols] = m

    extract(break_ties=False)
    off_count = jnp.zeros((1, LANES), F32)
    for _, _, rk_ref in halves:
        for cols in lane_tiles:
            taken = jnp.sum(jnp.where(rk_ref[:, cols] < float(topk), 1.0, 0.0), axis=0, keepdims=True)
            off_count = off_count + jnp.where(taken == float(topk), 0.0, 1.0)

    @pl.when(jnp.max(off_count) > 0.0)
    def _():
        extract(break_ties=True)

    row8 = lax.broadcasted_iota(jnp.int32, (SUBLANES, LANES), 0)
    seg_mask = {seg: (row8 >= seg[2]) & (row8 < seg[2] + seg[3]) for grp in _CAND_GROUPS for seg in grp}
    flats = []
    for grp in _CAND_GROUPS:
        f = jnp.full((SUBLANES, LANES), 1e9, F32)
        for seg in grp:
            i, j0, r0, _ = seg
            f = jnp.where(seg_mask[seg], (row8 + (i * topk + j0 - r0)).astype(F32), f)
        flats.append(f)
    flat = jnp.concatenate(flats, axis=0)
    for cols in lane_tiles:
        v1 = [v_ref[0, i, :, cols] for i in range(topk)]
        v2 = [v_ref[1, j, :, cols] for j in range(topk)]
        v2_groups = []
        for j0 in (0, SUBLANES):
            dense = jnp.zeros((SUBLANES, LANES), F32)
            for r in range(SUBLANES):
                dense = jnp.where(row8 == r, v2[j0 + r], dense)
            v2_groups.append(dense)
        top = v1[0] + v2[0]
        for g, grp in enumerate(_CAND_GROUPS):
            cand = jnp.full((SUBLANES, LANES), -jnp.inf, F32)
            for seg in grp:
                i, j0, r0, n = seg
                if n == 1 and j0 == 0:
                    piece = v1[i] + v2[0]
                else:
                    src = v2_groups[j0 // SUBLANES]
                    piece = v1[i] + (pltpu.roll(src, r0, axis=0) if r0 else src)
                cand = jnp.where(seg_mask[seg], piece, cand)
            rows = slice(g * SUBLANES, (g + 1) * SUBLANES)
            c_ref[rows, cols] = cand
            e_ref[rows, cols] = jnp.exp(cand - top)

    for _ in range(topk):
        for cols in lane_tiles:
            c_cur = c_ref[:, cols]
            m = jnp.max(c_cur, axis=0, keepdims=True)
            idx = jnp.min(jnp.where(c_cur == m, flat, 1e9), axis=0, keepdims=True)
            c_ref[:, cols] = jnp.where(flat == idx, -jnp.inf, c_cur)

    for lt, cols in enumerate(lane_tiles):
        chosen = jnp.where(flat < 1e8, jnp.where(c_ref[:, cols] == -jnp.inf, 1.0, 0.0), 0.0)
        z = jnp.sum(chosen * e_ref[:, cols], axis=0, keepdims=True)
        counts = [jnp.zeros((1, LANES), F32) for _ in range(topk)]
        for g, grp in enumerate(_CAND_GROUPS):
            ch = chosen[g * SUBLANES:(g + 1) * SUBLANES, :]
            for seg in grp:
                counts[seg[0]] = counts[seg[0]] + jnp.sum(
                    jnp.where(seg_mask[seg], ch, 0.0), axis=0, keepdims=True)
        v1_top = v_ref[0, 0, :, cols]
        v2_top = v_ref[1, 0, :, cols]

        r1 = rk1_ref[:, cols]
        r2 = rk2_ref[:, cols]
        n_a = jnp.zeros((nk, LANES), F32)
        for i in range(topk):
            n_a = jnp.where(r1 == float(i), counts[i], n_a)
        in1 = r1 < float(topk)
        in2 = r2 < float(topk)
        e1 = jnp.where(in1, jnp.exp(s1_ref[:, cols] - v1_top) * (0.5 / z), 0.0)
        e2 = jnp.where(in2, jnp.exp(s2_ref[:, cols] - v2_top), 0.0)
        e1_ref[:, cols] = _pack_pairs(e1, e1)
        na_ref[:, cols] = _pack_pairs(n_a, n_a)
        e2_ref[:, cols] = _pack_pairs(e2[:nk // 2], e2[nk // 2:])
        r2_ref[:, cols] = _pack_pairs(r2[:nk // 2], r2[nk // 2:])


def _select(q, keys_1, keys_2):
    t = q.shape[0]
    tn = SELECT_TN
    assert t % tn == 0
    nk = PEER_N_KEYS
    q1 = pl.BlockSpec((tn, PEER_HALF), lambda i, h: (i, 2 * h))
    q2 = pl.BlockSpec((tn, PEER_HALF), lambda i, h: (i, 2 * h + 1))
    kspec = pl.BlockSpec((None, nk, PEER_HALF), lambda i, h: (h, 0, 0))
    ospec = lambda rows: pl.BlockSpec((None, rows, tn), lambda i, h: (h, 0, i))
    oshape = lambda rows: jax.ShapeDtypeStruct((PEER_HEADS, rows, t), jnp.int32)
    keys_buf = pltpu.VMEM((nk, tn), F32)
    return pl.pallas_call(
        _select_kernel,
        grid=(t // tn, PEER_HEADS),
        in_specs=[q1, q2, kspec, kspec],
        out_specs=[ospec(nk), ospec(nk), ospec(nk // 2), ospec(nk // 2)],
        out_shape=[oshape(nk), oshape(nk), oshape(nk // 2), oshape(nk // 2)],
        scratch_shapes=[keys_buf] * 6 + [pltpu.VMEM((2, PEER_TOPK, 1, tn), F32)]
        + [pltpu.VMEM((SUBLANES * len(_CAND_GROUPS), tn), F32)] * 2,
        compiler_params=_params(("parallel", "parallel")),
        name="peer_select",
    )(q, q, keys_1, keys_2)


def _experts_kernel(h2_ref, wd_ref, wut_ref, e1_in, na_in, e2_in, r2_in, x2_ref, o_ref,
                    acc_ref, z0_ref, z1_ref, act0_ref, act1_ref, e1_ref, na_ref, e2_ref, r2_ref):
    s = pl.program_id(1)
    n_chunks = pl.num_programs(1) - 2
    nk = PEER_N_KEYS
    tn = z0_ref.shape[1]
    rows_per_chunk = EXPERT_CHUNK // nk

    @pl.when(s == 0)
    def _():
        for ref in (acc_ref, z0_ref, z1_ref, act0_ref, act1_ref):
            ref[...] = jnp.zeros_like(ref)
        e2_ref[...] = e2_in[...]
        r2_ref[...] = r2_in[...]

    chunk = jnp.clip(s - 1, 0, n_chunks - 1)
    r0 = (chunk % (SUBLANES // rows_per_chunk)) * rows_per_chunk
    for h in range(PEER_HEADS):
        for aa in range(rows_per_chunk):
            e1_ref[h, aa:aa + 1, :] = e1_in[h, pl.ds(r0 + aa, 1), :]
            na_ref[h, aa:aa + 1, :] = na_in[h, pl.ds(r0 + aa, 1), :]

    half = nk // 2
    groups = half // SUBLANES

    def gate_piece(lt, k_lo, k_hi, z_r, act_w):
        cols = slice(lt * LANES, (lt + 1) * LANES)
        g = [[jnp.zeros((2 * SUBLANES, LANES), BF16) for _ in range(k_lo, k_hi)]
             for _ in range(rows_per_chunk)]
        for h in range(PEER_HEADS):
            as_pair = lambda ref, aa: _as_bf16_pairs(
                jnp.broadcast_to(ref[h, aa:aa + 1, cols], (SUBLANES, LANES)))
            e1 = [as_pair(e1_ref, aa) for aa in range(rows_per_chunk)]
            n_a = [as_pair(na_ref, aa) for aa in range(rows_per_chunk)]
            for k in range(k_lo, k_hi):
                grp = slice(k * SUBLANES, (k + 1) * SUBLANES)
                r2 = _as_bf16_pairs(r2_ref[h, grp, cols])
                e2 = _as_bf16_pairs(e2_ref[h, grp, cols])
                for aa in range(rows_per_chunk):
                    g[aa][k - k_lo] = (g[aa][k - k_lo]
                                       + jnp.where(r2 < n_a[aa], e2, jnp.zeros_like(e2)) * e1[aa])
        for aa in range(rows_per_chunk):
            for k in range(k_lo, k_hi):
                g_lo, g_hi = _unpack_pairs(g[aa][k - k_lo])
                for g_half, base in ((g_lo, aa * nk), (g_hi, aa * nk + half)):
                    rows = slice(base + k * SUBLANES, base + (k + 1) * SUBLANES)
                    z = z_r[rows, cols]
                    act = z * (1.0 + lax.erf(z * (2.0 ** -0.5)))
                    act_w[rows, cols] = (act * g_half).astype(BF16)

    def step(z_w, z_r, act_w, act_r):
        d = acc_ref.shape[0]
        mxu = []
        for p in range(MXU_PIECES):
            pc = slice(p * tn // MXU_PIECES, (p + 1) * tn // MXU_PIECES)
            pw = slice(pc.start // 2, pc.stop // 2)

            def z_piece(pc=pc, pw=pw):
                z_w[:, pc] = _dot_nt(_rows_bf16(wd_ref[...]), _rows_bf16(h2_ref[pw, :]))

            mxu.append(z_piece)
            for r in range(UP_ROW_PIECES):
                rs = slice(r * d // UP_ROW_PIECES, (r + 1) * d // UP_ROW_PIECES)
                rw = slice(rs.start // 2, rs.stop // 2)

                def up_piece(pc=pc, rs=rs, rw=rw):
                    acc_ref[rs, pc] += _dot(_rows_bf16(wut_ref[rw, :]), act_r[:, pc])

                mxu.append(up_piece)
        gates = [(lt, k, k + GATE_GROUPS) for lt in range(tn // LANES)
                 for k in range(0, groups, GATE_GROUPS)]
        per_mxu = -(-len(gates) // len(mxu))
        for m, piece in enumerate(mxu):
            piece()
            for lt, k_lo, k_hi in gates[m * per_mxu:(m + 1) * per_mxu]:
                gate_piece(lt, k_lo, k_hi, z_r, act_w)
        for lt, k_lo, k_hi in gates[len(mxu) * per_mxu:]:
            gate_piece(lt, k_lo, k_hi, z_r, act_w)

    @pl.when(s % 2 == 0)
    def _():
        step(z0_ref, z1_ref, act1_ref, act0_ref)

    @pl.when(s % 2 == 1)
    def _():
        step(z1_ref, z0_ref, act0_ref, act1_ref)

    @pl.when(s == pl.num_programs(1) - 1)
    def _():
        o_ref[...] = x2_ref[...] + acc_ref[...].T


def _experts(h2, wd, wut, e1, na, e2, r2, x2):
    t, d = x2.shape
    tn = EXPERT_TN
    ne = 2 * wd.shape[0]
    nk = PEER_N_KEYS
    assert t % tn == 0 and ne % EXPERT_CHUNK == 0 and EXPERT_CHUNK % nk == 0
    n_chunks = ne // EXPERT_CHUNK
    assert n_chunks % 2 == 0
    tok = pl.BlockSpec((tn, d), lambda i, s: (i, 0))
    gate = pl.BlockSpec((PEER_HEADS, nk // 2, tn), lambda i, s: (0, 0, i))
    rpc = EXPERT_CHUNK // nk
    assert SUBLANES % rpc == 0
    chunks_per_group = SUBLANES // rpc
    row = pl.BlockSpec((PEER_HEADS, None, SUBLANES, tn),
                       lambda i, s: (0, jnp.clip(s - 1, 0, n_chunks - 1) // chunks_per_group, 0, i))
    e1 = e1.reshape(PEER_HEADS, nk // SUBLANES, SUBLANES, t)
    na = na.reshape(PEER_HEADS, nk // SUBLANES, SUBLANES, t)
    return pl.pallas_call(
        _experts_kernel,
        grid=(t // tn, n_chunks + 2),
        in_specs=[pl.BlockSpec((tn // 2, d), lambda i, s: (i, 0)),
                  pl.BlockSpec((EXPERT_CHUNK // 2, d), lambda i, s: (jnp.minimum(s, n_chunks - 1), 0)),
                  pl.BlockSpec((None, d // 2, EXPERT_CHUNK),
                               lambda i, s: (jnp.clip(s - 2, 0, n_chunks - 1), 0, 0)),
                  row, row, gate, gate, tok],
        out_specs=tok,
        out_shape=jax.ShapeDtypeStruct((t, d), F32),
        scratch_shapes=[pltpu.VMEM((d, tn), F32),
                        pltpu.VMEM((EXPERT_CHUNK, tn), F32), pltpu.VMEM((EXPERT_CHUNK, tn), F32),
                        pltpu.VMEM((EXPERT_CHUNK, tn), BF16), pltpu.VMEM((EXPERT_CHUNK, tn), BF16),
                        pltpu.VMEM((PEER_HEADS, rpc, tn), jnp.int32), pltpu.VMEM((PEER_HEADS, rpc, tn), jnp.int32),
                        pltpu.VMEM((PEER_HEADS, nk // 2, tn), jnp.int32),
                        pltpu.VMEM((PEER_HEADS, nk // 2, tn), jnp.int32)],
        compiler_params=_params(("parallel", "arbitrary")),
        name="peer_experts",
    )(h2, wd, wut, e1, na, e2, r2, x2)


def _rope_tables(seq):
    pos = jnp.arange(seq, dtype=F32)
    inv_freq = ROPE_THETA ** (-jnp.arange(0, ROT_DIM, 2, dtype=F32) / ROT_DIM)
    ang = pos[:, None] * inv_freq[None, :]
    cos, sin = jnp.cos(ang), jnp.sin(ang)
    rest = ATTN_HEAD_DIM - ROT_DIM
    cos_h = jnp.concatenate([cos, cos, jnp.ones((seq, rest), F32)], axis=-1)
    sin_h = jnp.concatenate([-sin, sin, jnp.zeros((seq, rest), F32)], axis=-1)
    return jnp.tile(cos_h, (1, ATTN_HEADS)), jnp.tile(sin_h, (1, ATTN_HEADS))


def kernel(x, mix_norm_w, w_in, q_norm_w, k_norm_w, conv_w, conv_b, igate_b, fgate_b, mlstm_norm_w,
           w_attn_branch, w_mlstm_branch, merge_b, w_out, ffn_norm_w, peer_w_query, peer_keys_1,
           peer_keys_2, peer_w_down, peer_w_up):
    b, s, d = x.shape
    t = b * s
    x2d = x.reshape(t, d)

    n_main = 3 * ATTN_W + 4 * MLSTM_W
    n_gate = 2 * MLSTM_HEADS
    w_cat = jnp.concatenate(
        [w_in[:, :n_main], w_in[:, n_main + n_gate:], w_in[:, n_main:n_main + n_gate],
         jnp.zeros((d, LANES - n_gate), w_in.dtype)], axis=1).astype(BF16)
    cos_t, sin_t = _rope_tables(s)
    head_of = jnp.arange(ATTN_W) // ATTN_HEAD_DIM
    bd = (head_of[:, None] == head_of[None, :]).astype(BF16)
    qnw = jnp.tile(q_norm_w, ATTN_HEADS).reshape(1, ATTN_W)
    knw = jnp.tile(k_norm_w, ATTN_HEADS).reshape(1, ATTN_W)

    aq, ak, av, mq, mk, mv, mo, ga, gm, gt = _inproj(
        x2d, mix_norm_w.reshape(1, d), w_cat, cos_t, sin_t, bd, qnw, knw, s)

    attn = _moba(aq.reshape(b, s, ATTN_W), ak.reshape(b, s, ATTN_W), av.reshape(b, s, ATTN_W))

    gates = gt[:, :n_gate].reshape(b, s, 2, MLSTM_HEADS)
    g_col = gates.transpose(0, 3, 1, 2)
    g_row = gates.transpose(0, 3, 2, 1)
    gate_b = jnp.stack([igate_b, fgate_b]).astype(F32)
    r3 = lambda a: a.reshape(b, s, MLSTM_W)
    hm = _mlstm(r3(mq), r3(mk), r3(mv), r3(mo), g_col, g_row, gate_b,
                conv_w[:, :MLSTM_W], conv_w[:, MLSTM_W:], conv_b[:MLSTM_W].reshape(1, MLSTM_W),
                conv_b[MLSTM_W:].reshape(1, MLSTM_W), mlstm_norm_w.reshape(1, MLSTM_W))

    x2, h2, q = _merge(x2d, attn.reshape(t, ATTN_W), hm.reshape(t, MLSTM_W), ga, gm,
                       w_attn_branch.astype(BF16), w_mlstm_branch.astype(BF16), merge_b,
                       w_out.astype(BF16), ffn_norm_w.reshape(1, d), peer_w_query.astype(BF16))

    e1, na, e2, r2 = _select(q, peer_keys_1, peer_keys_2)
    wut = peer_w_up.astype(BF16).reshape(-1, EXPERT_CHUNK, d).transpose(0, 2, 1)
    out = _experts(h2, _pack_rows(peer_w_down.astype(BF16)), _pack_rows(wut), e1, na, e2, r2, x2)
    return out.reshape(b, s, d)
```

```python
import functools

import jax
import jax.numpy as jnp
from jax import lax
from jax.experimental import pallas as pl
from jax.experimental.pallas import tpu as pltpu

F32 = jnp.float32
BF16 = jnp.bfloat16

EPS = 1e-6
ATTN_HEADS = 8
ATTN_HEAD_DIM = 64
ATTN_W = ATTN_HEADS * ATTN_HEAD_DIM
ROT_DIM = ATTN_HEAD_DIM // 4
ROPE_THETA = 500000.0
MOBA_BLOCK = 256
MOBA_TOPK = 3
MLSTM_HEADS = 4
MLSTM_HEAD_DIM = 128
MLSTM_W = MLSTM_HEADS * MLSTM_HEAD_DIM
CONV_WIDTH = 4
PEER_HEADS = 8
PEER_N_KEYS = 128
PEER_HALF = 128
PEER_TOPK = 16

LANES = 128
SUBLANES = 8
NEG = -1e30
VMEM_LIMIT = 56 * 1024 * 1024

INPROJ_TM = 512
MERGE_TM = 512
MOBA_UNROLL = 4
MLSTM_CHUNK = 256
MLSTM_HEADS_PER_STEP = 4
SELECT_TN = 512
EXPERT_TN = 1024
EXPERT_CHUNK = 512
MXU_PIECES = 4
UP_ROW_PIECES = 4
GATE_GROUPS = 4


def _dot(a, b):
    return jnp.dot(a, b, preferred_element_type=F32)


def _dot_nt(a, b):
    return lax.dot_general(a, b, (((1,), (1,)), ((), ())), preferred_element_type=F32)


def _dot_tn(a, b):
    return lax.dot_general(a, b, (((0,), (0,)), ((), ())), preferred_element_type=F32)


def _bf16_bits(x):
    return lax.bitcast_convert_type(x.astype(BF16).astype(F32), jnp.int32)


def _pack_pairs(lo, hi):
    return lax.shift_right_logical(_bf16_bits(lo), jnp.int32(16)) | _bf16_bits(hi)


def _as_bf16_pairs(words):
    return pltpu.bitcast(words, BF16)


def _unpack_pairs(pairs):
    words = pltpu.bitcast(pairs, jnp.int32)
    lo = lax.bitcast_convert_type(lax.shift_left(words, jnp.int32(16)), F32)
    hi = lax.bitcast_convert_type(words & jnp.int32(-65536), F32)
    return lo, hi


def _pack_rows(x):
    *lead, m, n = x.shape
    pairs = jnp.swapaxes(x.reshape(*lead, m // 2, 2, n), -1, -2)
    return lax.bitcast_convert_type(pairs, jnp.int32)


def _rows_bf16(words):
    return pltpu.bitcast(words, BF16)


def _split2(x):
    hi = x.astype(BF16)
    lo = (x - hi.astype(F32)).astype(BF16)
    return hi, lo


def _params(sem, flags=None):
    return pltpu.CompilerParams(dimension_semantics=sem, vmem_limit_bytes=VMEM_LIMIT, flags=flags)


def _inproj_kernel(x_ref, nw_ref, w_ref, cos_ref, sin_ref, bd_ref, qnw_ref, knw_ref,
                   aq_ref, ak_ref, av_ref, mq_ref, mk_ref, mv_ref, mo_ref, ga_ref, gm_ref, gt_ref):
    x = x_ref[...]
    h = x * lax.rsqrt(jnp.mean(x * x, axis=-1, keepdims=True) + EPS) * nw_ref[...]
    hb = h.astype(BF16)

    def proj(lo, width):
        return _dot(hb, w_ref[:, lo:lo + width])

    lane = lax.broadcasted_iota(jnp.int32, (x.shape[0], ATTN_W), 1) % ATTN_HEAD_DIM
    half = ROT_DIM // 2

    def qk_norm_rope(t, nw):
        hi, lo = _split2(t * t)
        ms = (_dot(hi, bd_ref[...]) + _dot(lo, bd_ref[...])) * (1.0 / ATTN_HEAD_DIM)
        tn = t * lax.rsqrt(ms + EPS) * nw
        fwd = pltpu.roll(tn, ATTN_W - half, axis=1)
        bwd = pltpu.roll(tn, half, axis=1)
        swapped = jnp.where(lane < half, fwd, jnp.where(lane < ROT_DIM, bwd, 0.0))
        return tn * cos_ref[...] + swapped * sin_ref[...]

    aq_ref[...] = qk_norm_rope(proj(0, ATTN_W), qnw_ref[...]).astype(BF16)
    ak_ref[...] = qk_norm_rope(proj(ATTN_W, ATTN_W), knw_ref[...]).astype(BF16)
    av_ref[...] = proj(2 * ATTN_W, ATTN_W).astype(BF16)
    base = 3 * ATTN_W
    mq_ref[...] = proj(base, MLSTM_W).astype(BF16)
    mk_ref[...] = proj(base + MLSTM_W, MLSTM_W).astype(BF16)
    mv_ref[...] = proj(base + 2 * MLSTM_W, MLSTM_W).astype(BF16)
    mo_ref[...] = proj(base + 3 * MLSTM_W, MLSTM_W).astype(BF16)
    base = base + 4 * MLSTM_W
    d = x.shape[1]
    ga_ref[...] = proj(base, d).astype(BF16)
    gm_ref[...] = proj(base + d, d).astype(BF16)
    gt_ref[...] = proj(base + 2 * d, LANES)


def _inproj(x2d, mix_norm_w, w_cat, cos_t, sin_t, bd, qnw, knw, seq):
    t, d = x2d.shape
    tm = INPROJ_TM
    assert t % tm == 0 and seq % tm == 0
    tiles_per_seq = seq // tm
    ncols = w_cat.shape[1]
    row = lambda w: pl.BlockSpec((tm, w), lambda i: (i, 0))
    const = lambda shape: pl.BlockSpec(shape, lambda i: (0, 0))
    pos = pl.BlockSpec((tm, ATTN_W), lambda i: (i % tiles_per_seq, 0))
    outs = [ATTN_W] * 3 + [MLSTM_W] * 4 + [d, d]
    return pl.pallas_call(
        _inproj_kernel,
        grid=(t // tm,),
        in_specs=[row(d), const((1, d)), const((d, ncols)), pos, pos, const((ATTN_W, ATTN_W)),
                  const((1, ATTN_W)), const((1, ATTN_W))],
        out_specs=[row(w) for w in outs] + [row(LANES)],
        out_shape=[jax.ShapeDtypeStruct((t, w), BF16) for w in outs]
        + [jax.ShapeDtypeStruct((t, LANES), F32)],
        compiler_params=_params(("parallel",)),
        name="inproj",
    )(x2d, mix_norm_w, w_cat, cos_t, sin_t, bd, qnw, knw)


def _moba_kernel(q_ref, k_ref, v_ref, o_ref, kmean_ref, kext_ref, vext_ref, s_ref, *, n_blocks):
    i = pl.program_id(2)
    blk = MOBA_BLOCK
    hd = ATTN_HEAD_DIM
    heads = (0, 1)
    lane = lax.broadcasted_iota(jnp.int32, (blk, LANES), 1)
    in_head = [lane < hd, lane >= hd]
    spare0 = [hd, 0]

    @pl.when(i == 0)
    def _():
        kmean_ref[...] = jnp.zeros_like(kmean_ref)
        for n in range(n_blocks):
            rows = slice(n * blk, (n + 1) * blk)
            kb = k_ref[rows, :]
            vb = v_ref[rows, :]
            kmean_ref[n:n + 1, :] = jnp.mean(kb.astype(F32), axis=0, keepdims=True)
            for h in heads:
                onehot = jnp.where(lane == spare0[h] + n, 1.0, 0.0).astype(BF16)
                kext_ref[h, rows, :] = jnp.where(in_head[h], kb, onehot)
                vext_ref[h, rows, :] = jnp.where(in_head[h], vb, jnp.ones_like(vb))

    q = q_ref[...]
    kmean_hi, kmean_lo = _split2(kmean_ref[...])
    brow = lax.broadcasted_iota(jnp.int32, (16, blk), 0)
    qext = []
    for h in heads:
        qh = jnp.where(in_head[h], q, jnp.zeros_like(q)) * jnp.asarray(hd ** -0.5, BF16)
        g_t = (_dot_nt(kmean_hi, qh) + _dot_nt(kmean_lo, qh))[0:16, :]
        rank = jnp.zeros((16, blk), F32)
        for m in range(min(n_blocks, 16)):
            gm = g_t[m:m + 1, :]
            beats = jnp.where(gm > g_t, 1.0, jnp.where(gm == g_t, jnp.where(brow > m, 1.0, 0.0), 0.0))
            rank = rank + jnp.where(m < i, beats, 0.0)
        bias_t = jnp.where(brow < i, jnp.where(rank < MOBA_TOPK, 0.0, NEG), 0.0)
        bias = jnp.concatenate([bias_t, jnp.zeros((LANES - 16, blk), F32)], axis=0).T
        if spare0[h]:
            bias = pltpu.roll(bias, spare0[h], axis=1)
        qext.append(qh + bias.astype(BF16))

    own = pl.multiple_of(i * blk, blk)
    causal = (lax.broadcasted_iota(jnp.int32, (blk, blk), 1)
              <= lax.broadcasted_iota(jnp.int32, (blk, blk), 0))
    unroll = MOBA_UNROLL

    def scores(t, m_run):
        m_new = list(m_run)
        for u in range(unroll):
            n = unroll * t + u
            start = pl.multiple_of(n * blk, blk)
            for h in heads:
                s = _dot_nt(qext[h], kext_ref[h, pl.ds(start, blk), :])
                s_ref[h, n] = s
                m_cand = jnp.maximum(m_new[h], jnp.max(s, axis=1, keepdims=True))
                m_new[h] = m_cand if u == 0 else jnp.where(n < i, m_cand, m_new[h])
        return tuple(m_new)

    m_past = lax.fori_loop(0, (i + unroll - 1) // unroll, scores,
                           tuple(jnp.full((blk, 1), NEG, F32) for _ in heads))
    m_fin = []
    for h in heads:
        s = jnp.where(causal, _dot_nt(qext[h], kext_ref[h, pl.ds(own, blk), :]), NEG)
        s_ref[h, i] = s
        m_fin.append(jnp.maximum(m_past[h], jnp.max(s, axis=1, keepdims=True)))

    def values(t, acc):
        out = list(acc)
        for u in range(unroll):
            n = unroll * t + u
            n_read = jnp.minimum(n, i)
            start = pl.multiple_of(n_read * blk, blk)
            for h in heads:
                m_eff = m_fin[h] if u == 0 else jnp.where(n <= i, m_fin[h], -NEG)
                p = jnp.exp(s_ref[h, n_read] - m_eff).astype(BF16)
                out[h] = out[h] + _dot(p, vext_ref[h, pl.ds(start, blk), :])
        return tuple(out)

    acc = lax.fori_loop(0, (i + unroll) // unroll, values,
                        tuple(jnp.zeros((blk, LANES), F32) for _ in heads))
    outs = [a / pltpu.roll(a, hd, axis=1) for a in acc]
    o_ref[...] = jnp.where(in_head[0], outs[0], outs[1]).astype(o_ref.dtype)


def _moba(q, k, v):
    b, s, w = q.shape
    assert s % MOBA_BLOCK == 0 and w % LANES == 0
    n_blocks = s // MOBA_BLOCK
    assert n_blocks <= 16 and n_blocks % MOBA_UNROLL == 0
    qspec = pl.BlockSpec((None, MOBA_BLOCK, LANES), lambda bi, hp, i: (bi, i, hp))
    kvspec = pl.BlockSpec((None, s, LANES), lambda bi, hp, i: (bi, 0, hp))
    return pl.pallas_call(
        functools.partial(_moba_kernel, n_blocks=n_blocks),
        grid=(b, w // LANES, n_blocks),
        in_specs=[qspec, kvspec, kvspec],
        out_specs=qspec,
        out_shape=jax.ShapeDtypeStruct((b, s, w), BF16),
        scratch_shapes=[pltpu.VMEM((LANES, LANES), F32),
                        pltpu.VMEM((2, s, LANES), BF16), pltpu.VMEM((2, s, LANES), BF16),
                        pltpu.VMEM((2, n_blocks, MOBA_BLOCK, MOBA_BLOCK), F32)],
        compiler_params=_params(("parallel", "parallel", "arbitrary")),
        name="moba",
    )(q, k, v)


def _log_sigmoid(f):
    return jnp.minimum(f, 0.0) - jnp.log(1.0 + jnp.exp(-jnp.abs(f)))


def _mlstm_kernel(gb_ref, mq_ref, mk_ref, mv_ref, mo_ref, gc_ref, gr_ref, cwq_ref, cwk_ref,
                  cbq_ref, cbk_ref, nw_ref, o_ref, c_ref, n_ref, m_ref):
    group = pl.program_id(1)
    c = pl.program_id(2)
    L = MLSTM_CHUNK
    d = MLSTM_HEAD_DIM
    heads = range(MLSTM_HEADS_PER_STEP)
    lanes = [slice(h * d, (h + 1) * d) for h in heads]

    @pl.when(c == 0)
    def _():
        c_ref[...] = jnp.zeros_like(c_ref)
        n_ref[...] = jnp.zeros_like(n_ref)
        m_ref[...] = jnp.zeros_like(m_ref)

    start = pl.multiple_of(c * L, L)
    prev = pl.multiple_of(jnp.maximum(c - 1, 0) * L, L)
    row_d = lax.broadcasted_iota(jnp.int32, (L, d), 0)

    def conv_silu(u_ref, w_ref, b_ref, ls):
        u = u_ref[pl.ds(start, L), ls].astype(F32)
        pv = u_ref[pl.ds(prev, L), ls].astype(F32)
        pv = jnp.where(c > 0, pv, 0.0)
        y = u * w_ref[CONV_WIDTH - 1:CONV_WIDTH, ls] + b_ref[:, ls]
        for k in range(1, CONV_WIDTH):
            shifted = jnp.where(row_d < k, pltpu.roll(pv, k, axis=0), pltpu.roll(u, k, axis=0))
            y = y + shifted * w_ref[CONV_WIDTH - 1 - k:CONV_WIDTH - k, ls]
        return y * jax.nn.sigmoid(y)

    qc = [conv_silu(mq_ref, cwq_ref, cbq_ref, ls) for ls in lanes]
    kc = [conv_silu(mk_ref, cwk_ref, cbk_ref, ls) * (d ** -0.5) for ls in lanes]
    qb = [x.astype(BF16) for x in qc]
    kb = [x.astype(BF16) for x in kc]
    vb = [mv_ref[:, ls] for ls in lanes]

    ib = [gb_ref[0, group * MLSTM_HEADS_PER_STEP + h] for h in heads]
    fb = [gb_ref[1, group * MLSTM_HEADS_PER_STEP + h] for h in heads]
    gcol = [gc_ref[h] for h in heads]
    grow = [gr_ref[h] for h in heads]
    ig_col = [gcol[h][:, 0:1] + ib[h] for h in heads]
    ig_row = [grow[h][0:1, :] + ib[h] for h in heads]
    lf_col = [_log_sigmoid(gcol[h][:, 1:2] + fb[h]) for h in heads]
    lf_row = [_log_sigmoid(grow[h][1:2, :] + fb[h]) for h in heads]

    r = lax.broadcasted_iota(jnp.int32, (L, L), 0)
    s = lax.broadcasted_iota(jnp.int32, (L, L), 1)
    causal = s <= r
    tri_l = jnp.where(causal, 1.0, 0.0).astype(BF16)
    tri_u = jnp.where(r <= s, 1.0, 0.0).astype(BF16)
    lfc = [_split2(jnp.broadcast_to(lf_col[h], (L, L))) for h in heads]
    lfr = [_split2(jnp.broadcast_to(lf_row[h], (L, L))) for h in heads]
    b_t = [_dot(tri_l, lfc[h][0]) + _dot(tri_l, lfc[h][1]) for h in heads]
    b_s = [_dot(lfr[h][0], tri_u) + _dot(lfr[h][1], tri_u) for h in heads]

    m_prev = [m_ref[h] for h in heads]
    b_col = [b_t[h][:, 0:1] for h in heads]
    log_d = [jnp.where(causal, b_t[h] - b_s[h] + ig_row[h], NEG) for h in heads]
    m_inter = [b_col[h] + m_prev[h] for h in heads]
    m_t = [jnp.maximum(m_inter[h], jnp.max(log_d[h], axis=1, keepdims=True)) for h in heads]
    a = [jnp.exp(log_d[h] - m_t[h]) * _dot_nt(qb[h], kb[h]) for h in heads]
    w_inter = [jnp.exp(m_inter[h] - m_t[h]) for h in heads]
    c_prev = [c_ref[h] for h in heads]
    n_prev = [n_ref[h] for h in heads]
    num = [_dot(a[h].astype(BF16), vb[h]) + w_inter[h] * _dot(qb[h], c_prev[h].astype(BF16)) for h in heads]
    den = [jnp.sum(a[h], axis=1, keepdims=True)
           + w_inter[h] * jnp.sum(qc[h] * n_prev[h], axis=1, keepdims=True) for h in heads]
    hs = [num[h] / jnp.maximum(jnp.abs(den[h]), jnp.exp(-m_t[h])) for h in heads]

    m_new = [m_t[h][L - 1:L, :] for h in heads]
    b_last = [b_col[h][L - 1:L, :] for h in heads]
    ws_col = [jnp.exp(b_last[h] - b_col[h] + ig_col[h] - m_new[h]) for h in heads]
    decay = [jnp.exp(b_last[h] + m_prev[h] - m_new[h]) for h in heads]
    for h in heads:
        c_ref[h] = decay[h] * c_prev[h] + _dot_tn(kb[h], (ws_col[h] * vb[h].astype(F32)).astype(BF16))
        n_ref[h] = decay[h] * n_prev[h] + jnp.sum(ws_col[h] * kc[h], axis=0, keepdims=True)
        m_ref[h] = m_new[h]

    for h, ls in zip(heads, lanes):
        hn = hs[h] * lax.rsqrt(jnp.mean(hs[h] * hs[h], axis=-1, keepdims=True) + EPS) * nw_ref[:, ls]
        o_ref[:, ls] = (jax.nn.sigmoid(mo_ref[:, ls].astype(F32)) * hn).astype(o_ref.dtype)


def _mlstm(mq, mk, mv, mo, g_col, g_row, gate_b, conv_wq, conv_wk, conv_bq, conv_bk, norm_w):
    b, s, w = mq.shape
    L = MLSTM_CHUNK
    d = MLSTM_HEAD_DIM
    hps = MLSTM_HEADS_PER_STEP
    assert s % L == 0 and (w // d) % hps == 0
    seq = pl.BlockSpec((None, s, hps * d), lambda bi, g, c: (bi, 0, g))
    chunk = pl.BlockSpec((None, L, hps * d), lambda bi, g, c: (bi, c, g))
    per_group = lambda rows: pl.BlockSpec((rows, hps * d), lambda bi, g, c: (0, g))
    return pl.pallas_call(
        _mlstm_kernel,
        grid=(b, w // d // hps, s // L),
        in_specs=[pl.BlockSpec(memory_space=pltpu.SMEM), seq, seq, chunk, chunk,
                  pl.BlockSpec((None, hps, L, 2), lambda bi, g, c: (bi, g, c, 0)),
                  pl.BlockSpec((None, hps, 2, L), lambda bi, g, c: (bi, g, 0, c)),
                  per_group(CONV_WIDTH), per_group(CONV_WIDTH), per_group(1), per_group(1), per_group(1)],
        out_specs=chunk,
        out_shape=jax.ShapeDtypeStruct((b, s, w), BF16),
        scratch_shapes=[pltpu.VMEM((hps, d, d), F32), pltpu.VMEM((hps, 1, d), F32),
                        pltpu.VMEM((hps, 1, 1), F32)],
        compiler_params=_params(("parallel", "parallel", "arbitrary")),
        name="mlstm",
    )(gate_b, mq, mk, mv, mo, g_col, g_row, conv_wq, conv_wk, conv_bq, conv_bk, norm_w)


def _merge_kernel(x_ref, attn_ref, hm_ref, ga_ref, gm_ref, wab_ref, wmb_ref, mb_ref, wout_ref,
                  fnw_ref, wq_ref, x2_ref, h2_ref, q_ref):
    ya = _dot(attn_ref[...], wab_ref[...])
    ym = _dot(hm_ref[...], wmb_ref[...])
    y = (jax.nn.sigmoid(ga_ref[...].astype(F32) + mb_ref[0:1, :]) * ya
         + jax.nn.sigmoid(gm_ref[...].astype(F32) + mb_ref[1:2, :]) * ym)
    x2 = x_ref[...] + _dot(y.astype(BF16), wout_ref[...])
    x2_ref[...] = x2
    h2 = x2 * lax.rsqrt(jnp.mean(x2 * x2, axis=-1, keepdims=True) + EPS) * fnw_ref[...]
    h2b = h2.astype(BF16)
    h2_ref[...] = pltpu.bitcast(h2b, jnp.int32)
    q_ref[...] = _dot(h2b, wq_ref[...])


def _merge(x2d, attn, hm, ga, gm, wab, wmb, merge_b, wout, fnw, wq):
    t, d = x2d.shape
    tm = MERGE_TM
    nq = wq.shape[1]
    row = lambda w: pl.BlockSpec((tm, w), lambda i: (i, 0))
    const = lambda a: pl.BlockSpec(a.shape, lambda i: (0, 0))
    return pl.pallas_call(
        _merge_kernel,
        grid=(t // tm,),
        in_specs=[row(d), row(ATTN_W), row(MLSTM_W), row(d), row(d), const(wab), const(wmb),
                  const(merge_b), const(wout), const(fnw), const(wq)],
        out_specs=[row(d), pl.BlockSpec((tm // 2, d), lambda i: (i, 0)), row(nq)],
        out_shape=[jax.ShapeDtypeStruct((t, d), F32), jax.ShapeDtypeStruct((t // 2, d), jnp.int32),
                   jax.ShapeDtypeStruct((t, nq), F32)],
        compiler_params=_params(("parallel",)),
        name="merge",
    )(x2d, attn, hm, ga, gm, wab, wmb, merge_b, wout, fnw, wq)


_CAND_GROUPS = (
    ((0, 0, 0, 8),), ((0, 8, 0, 8),), ((1, 0, 0, 8),),
    ((2, 0, 0, 5), (4, 0, 5, 3)),
    ((3, 0, 0, 4), (5, 0, 4, 2), (6, 0, 6, 2)),
    ((7, 0, 0, 2),) + tuple((i, 0, i - 6, 1) for i in range(8, 14)),
    ((14, 0, 0, 1), (15, 0, 1, 1)),
)


def _select_kernel(q1_ref, q2_ref, k1_ref, k2_ref, e1_ref, na_ref, e2_ref, r2_ref,
                   s1_ref, s2_ref, w1_ref, w2_ref, rk1_ref, rk2_ref, v_ref, c0_ref, c_ref, e_ref):
    tn = q1_ref.shape[0]
    nk = PEER_N_KEYS
    topk = PEER_TOPK
    n_lt = tn // LANES
    lane_tiles = [slice(lt * LANES, (lt + 1) * LANES) for lt in range(n_lt)]
    halves = ((s1_ref, w1_ref, rk1_ref), (s2_ref, w2_ref, rk2_ref))

    def scores(k_ref, q_ref):
        kh, kl = _split2(k_ref[...])
        qh, ql = _split2(q_ref[...])
        return _dot_nt(kh, qh) + _dot_nt(kh, ql) + _dot_nt(kl, qh)

    s1_ref[...] = scores(k1_ref, q1_ref)
    s2_ref[...] = scores(k2_ref, q2_ref)
    key = lax.broadcasted_iota(jnp.int32, (nk, LANES), 0).astype(F32)

    def extract(break_ties):
        for s_ref, w_ref, rk_ref in halves:
            w_ref[...] = s_ref[...]
            rk_ref[...] = jnp.full(rk_ref.shape, float(topk), F32)
        for it in range(topk):
            for half, (_, w_ref, rk_ref) in enumerate(halves):
                for cols in lane_tiles:
                    s_cur = w_ref[:, cols]
                    m = jnp.max(s_cur, axis=0, keepdims=True)
                    if break_ties:
                        hit = key == jnp.min(jnp.where(s_cur == m, key, float(nk)), axis=0, keepdims=True)
                    else:
                        hit = s_cur == m
                    w_ref[:, cols] = jnp.where(hit, -jnp.inf, s_cur)
                    rk_ref[:, cols] = jnp.where(hit, float(it), rk_ref[:, cols])
                    v_ref[half, it, :, cols] = m

    extract(break_ties=False)
    off_count = jnp.zeros((1, LANES), F32)
    for _, _, rk_ref in halves:
        for cols in lane_tiles:
            taken = jnp.sum(jnp.where(rk_ref[:, cols] < float(topk), 1.0, 0.0), axis=0, keepdims=True)
            off_count = off_count + jnp.where(taken == float(topk), 0.0, 1.0)

    @pl.when(jnp.max(off_count) > 0.0)
    def _():
        extract(break_ties=True)

    row8 = lax.broadcasted_iota(jnp.int32, (SUBLANES, LANES), 0)
    seg_mask = {seg: (row8 >= seg[2]) & (row8 < seg[2] + seg[3]) for grp in _CAND_GROUPS for seg in grp}
    flats = []
    for grp in _CAND_GROUPS:
        f = jnp.full((SUBLANES, LANES), 1e9, F32)
        for seg in grp:
            i, j0, r0, _ = seg
            f = jnp.where(seg_mask[seg], (row8 + (i * topk + j0 - r0)).astype(F32), f)
        flats.append(f)
    flat = jnp.concatenate(flats, axis=0)
    for cols in lane_tiles:
        v1 = [v_ref[0, i, :, cols] for i in range(topk)]
        v2 = [v_ref[1, j, :, cols] for j in range(topk)]
        v2_groups = []
        for j0 in (0, SUBLANES):
            dense = jnp.zeros((SUBLANES, LANES), F32)
            for r in range(SUBLANES):
                dense = jnp.where(row8 == r, v2[j0 + r], dense)
            v2_groups.append(dense)
        top = v1[0] + v2[0]
        for g, grp in enumerate(_CAND_GROUPS):
            cand = jnp.full((SUBLANES, LANES), -jnp.inf, F32)
            for seg in grp:
                i, j0, r0, n = seg
                if n == 1 and j0 == 0:
                    piece = v1[i] + v2[0]
                else:
                    src = v2_groups[j0 // SUBLANES]
                    piece = v1[i] + (pltpu.roll(src, r0, axis=0) if r0 else src)
                cand = jnp.where(seg_mask[seg], piece, cand)
            rows = slice(g * SUBLANES, (g + 1) * SUBLANES)
            c0_ref[rows, cols] = cand
            e_ref[rows, cols] = jnp.exp(cand - top)

    def pick(break_ties):
        c_ref[...] = c0_ref[...]
        for _ in range(topk):
            for cols in lane_tiles:
                c_cur = c_ref[:, cols]
                m = jnp.max(c_cur, axis=0, keepdims=True)
                if break_ties:
                    hit = flat == jnp.min(jnp.where(c_cur == m, flat, 1e9), axis=0, keepdims=True)
                else:
                    hit = c_cur == m
                c_ref[:, cols] = jnp.where(hit, -jnp.inf, c_cur)

    pick(break_ties=False)
    off_count = jnp.zeros((1, LANES), F32)
    for cols in lane_tiles:
        taken = jnp.sum(jnp.where(flat < 1e8, jnp.where(c_ref[:, cols] == -jnp.inf, 1.0, 0.0), 0.0),
                        axis=0, keepdims=True)
        off_count = off_count + jnp.where(taken == float(topk), 0.0, 1.0)

    @pl.when(jnp.max(off_count) > 0.0)
    def _():
        pick(break_ties=True)

    for lt, cols in enumerate(lane_tiles):
        chosen = jnp.where(flat < 1e8, jnp.where(c_ref[:, cols] == -jnp.inf, 1.0, 0.0), 0.0)
        z = jnp.sum(chosen * e_ref[:, cols], axis=0, keepdims=True)
        counts = [jnp.zeros((1, LANES), F32) for _ in range(topk)]
        for g, grp in enumerate(_CAND_GROUPS):
            ch = chosen[g * SUBLANES:(g + 1) * SUBLANES, :]
            for seg in grp:
                counts[seg[0]] = counts[seg[0]] + jnp.sum(
                    jnp.where(seg_mask[seg], ch, 0.0), axis=0, keepdims=True)
        v1_top = v_ref[0, 0, :, cols]
        v2_top = v_ref[1, 0, :, cols]

        r1 = rk1_ref[:, cols]
        r2 = rk2_ref[:, cols]
        n_a = jnp.zeros((nk, LANES), F32)
        for i in range(topk):
            n_a = jnp.where(r1 == float(i), counts[i], n_a)
        in1 = r1 < float(topk)
        in2 = r2 < float(topk)
        e1 = jnp.where(in1, jnp.exp(s1_ref[:, cols] - v1_top) * (0.5 / z), 0.0)
        e2 = jnp.where(in2, jnp.exp(s2_ref[:, cols] - v2_top), 0.0)
        e1_ref[:, cols] = _pack_pairs(e1, e1)
        na_ref[:, cols] = _pack_pairs(n_a, n_a)
        e2_ref[:, cols] = _pack_pairs(e2[:nk // 2], e2[nk // 2:])
        r2_ref[:, cols] = _pack_pairs(r2[:nk // 2], r2[nk // 2:])


def _select(q, keys_1, keys_2):
    t = q.shape[0]
    tn = SELECT_TN
    assert t % tn == 0
    nk = PEER_N_KEYS
    q1 = pl.BlockSpec((tn, PEER_HALF), lambda i, h: (i, 2 * h))
    q2 = pl.BlockSpec((tn, PEER_HALF), lambda i, h: (i, 2 * h + 1))
    kspec = pl.BlockSpec((None, nk, PEER_HALF), lambda i, h: (h, 0, 0))
    ospec = lambda rows: pl.BlockSpec((None, rows, tn), lambda i, h: (h, 0, i))
    oshape = lambda rows: jax.ShapeDtypeStruct((PEER_HEADS, rows, t), jnp.int32)
    keys_buf = pltpu.VMEM((nk, tn), F32)
    return pl.pallas_call(
        _select_kernel,
        grid=(t // tn, PEER_HEADS),
        in_specs=[q1, q2, kspec, kspec],
        out_specs=[ospec(nk), ospec(nk), ospec(nk // 2), ospec(nk // 2)],
        out_shape=[oshape(nk), oshape(nk), oshape(nk // 2), oshape(nk // 2)],
        scratch_shapes=[keys_buf] * 6 + [pltpu.VMEM((2, PEER_TOPK, 1, tn), F32)]
        + [pltpu.VMEM((SUBLANES * len(_CAND_GROUPS), tn), F32)] * 3,
        compiler_params=_params(("parallel", "parallel")),
        name="peer_select",
    )(q, q, keys_1, keys_2)


def _experts_kernel(h2_ref, wd_ref, wut_ref, e1_in, na_in, e2_in, r2_in, x2_ref, o_ref,
                    acc_ref, z0_ref, z1_ref, act0_ref, act1_ref, e1_ref, na_ref, e2_ref, r2_ref):
    s = pl.program_id(1)
    n_chunks = pl.num_programs(1) - 2
    nk = PEER_N_KEYS
    tn = z0_ref.shape[1]
    rows_per_chunk = EXPERT_CHUNK // nk

    @pl.when(s == 0)
    def _():
        for ref in (acc_ref, z0_ref, z1_ref, act0_ref, act1_ref):
            ref[...] = jnp.zeros_like(ref)
        e2_ref[...] = e2_in[...]
        r2_ref[...] = r2_in[...]

    chunk = jnp.clip(s - 1, 0, n_chunks - 1)
    r0 = (chunk % (SUBLANES // rows_per_chunk)) * rows_per_chunk
    for h in range(PEER_HEADS):
        for aa in range(rows_per_chunk):
            e1_ref[h, aa:aa + 1, :] = e1_in[h, pl.ds(r0 + aa, 1), :]
            na_ref[h, aa:aa + 1, :] = na_in[h, pl.ds(r0 + aa, 1), :]

    half = nk // 2
    groups = half // SUBLANES

    def gate_piece(lt, k_lo, k_hi, z_r, act_w):
        cols = slice(lt * LANES, (lt + 1) * LANES)
        g = [[jnp.zeros((2 * SUBLANES, LANES), BF16) for _ in range(k_lo, k_hi)]
             for _ in range(rows_per_chunk)]
        for h in range(PEER_HEADS):
            as_pair = lambda ref, aa: _as_bf16_pairs(
                jnp.broadcast_to(ref[h, aa:aa + 1, cols], (SUBLANES, LANES)))
            e1 = [as_pair(e1_ref, aa) for aa in range(rows_per_chunk)]
            n_a = [as_pair(na_ref, aa) for aa in range(rows_per_chunk)]
            for k in range(k_lo, k_hi):
                grp = slice(k * SUBLANES, (k + 1) * SUBLANES)
                r2 = _as_bf16_pairs(r2_ref[h, grp, cols])
                e2 = _as_bf16_pairs(e2_ref[h, grp, cols])
                for aa in range(rows_per_chunk):
                    g[aa][k - k_lo] = (g[aa][k - k_lo]
                                       + jnp.where(r2 < n_a[aa], e2, jnp.zeros_like(e2)) * e1[aa])
        for aa in range(rows_per_chunk):
            for k in range(k_lo, k_hi):
                g_lo, g_hi = _unpack_pairs(g[aa][k - k_lo])
                for g_half, base in ((g_lo, aa * nk), (g_hi, aa * nk + half)):
                    rows = slice(base + k * SUBLANES, base + (k + 1) * SUBLANES)
                    z = z_r[rows, cols]
                    act = z * (1.0 + lax.erf(z * (2.0 ** -0.5)))
                    act_w[rows, cols] = (act * g_half).astype(BF16)

    def step(z_w, z_r, act_w, act_r):
        d = acc_ref.shape[0]
        mxu = []
        for p in range(MXU_PIECES):
            pc = slice(p * tn // MXU_PIECES, (p + 1) * tn // MXU_PIECES)
            pw = slice(pc.start // 2, pc.stop // 2)

            def z_piece(pc=pc, pw=pw):
                z_w[:, pc] = _dot_nt(_rows_bf16(wd_ref[...]), _rows_bf16(h2_ref[pw, :]))

            mxu.append(z_piece)
            for r in range(UP_ROW_PIECES):
                rs = slice(r * d // UP_ROW_PIECES, (r + 1) * d // UP_ROW_PIECES)
                rw = slice(rs.start // 2, rs.stop // 2)

                def up_piece(pc=pc, rs=rs, rw=rw):
                    acc_ref[rs, pc] += _dot(_rows_bf16(wut_ref[rw, :]), act_r[:, pc])

                mxu.append(up_piece)
        gates = [(lt, k, k + GATE_GROUPS) for lt in range(tn // LANES)
                 for k in range(0, groups, GATE_GROUPS)]
        per_mxu = -(-len(gates) // len(mxu))
        for m, piece in enumerate(mxu):
            piece()
            for lt, k_lo, k_hi in gates[m * per_mxu:(m + 1) * per_mxu]:
                gate_piece(lt, k_lo, k_hi, z_r, act_w)
        for lt, k_lo, k_hi in gates[len(mxu) * per_mxu:]:
            gate_piece(lt, k_lo, k_hi, z_r, act_w)

    @pl.when(s % 2 == 0)
    def _():
        step(z0_ref, z1_ref, act1_ref, act0_ref)

    @pl.when(s % 2 == 1)
    def _():
        step(z1_ref, z0_ref, act0_ref, act1_ref)

    @pl.when(s == pl.num_programs(1) - 1)
    def _():
        o_ref[...] = x2_ref[...] + acc_ref[...].T


def _experts(h2, wd, wut, e1, na, e2, r2, x2):
    t, d = x2.shape
    tn = EXPERT_TN
    ne = 2 * wd.shape[0]
    nk = PEER_N_KEYS
    assert t % tn == 0 and ne % EXPERT_CHUNK == 0 and EXPERT_CHUNK % nk == 0
    n_chunks = ne // EXPERT_CHUNK
    assert n_chunks % 2 == 0
    tok = pl.BlockSpec((tn, d), lambda i, s: (i, 0))
    gate = pl.BlockSpec((PEER_HEADS, nk // 2, tn), lambda i, s: (0, 0, i))
    rpc = EXPERT_CHUNK // nk
    assert SUBLANES % rpc == 0
    chunks_per_group = SUBLANES // rpc
    row = pl.BlockSpec((PEER_HEADS, None, SUBLANES, tn),
                       lambda i, s: (0, jnp.clip(s - 1, 0, n_chunks - 1) // chunks_per_group, 0, i))
    e1 = e1.reshape(PEER_HEADS, nk // SUBLANES, SUBLANES, t)
    na = na.reshape(PEER_HEADS, nk // SUBLANES, SUBLANES, t)
    return pl.pallas_call(
        _experts_kernel,
        grid=(t // tn, n_chunks + 2),
        in_specs=[pl.BlockSpec((tn // 2, d), lambda i, s: (i, 0)),
                  pl.BlockSpec((EXPERT_CHUNK // 2, d), lambda i, s: (jnp.minimum(s, n_chunks - 1), 0)),
                  pl.BlockSpec((None, d // 2, EXPERT_CHUNK),
                               lambda i, s: (jnp.clip(s - 2, 0, n_chunks - 1), 0, 0)),
                  row, row, gate, gate, tok],
        out_specs=tok,
        out_shape=jax.ShapeDtypeStruct((t, d), F32),
        scratch_shapes=[pltpu.VMEM((d, tn), F32),
                        pltpu.VMEM((EXPERT_CHUNK, tn), F32), pltpu.VMEM((EXPERT_CHUNK, tn), F32),
                        pltpu.VMEM((EXPERT_CHUNK, tn), BF16), pltpu.VMEM((EXPERT_CHUNK, tn), BF16),
                        pltpu.VMEM((PEER_HEADS, rpc, tn), jnp.int32), pltpu.VMEM((PEER_HEADS, rpc, tn), jnp.int32),
                        pltpu.VMEM((PEER_HEADS, nk // 2, tn), jnp.int32),
                        pltpu.VMEM((PEER_HEADS, nk // 2, tn), jnp.int32)],
        compiler_params=_params(("parallel", "arbitrary")),
        name="peer_experts",
    )(h2, wd, wut, e1, na, e2, r2, x2)


def _rope_tables(seq):
    pos = jnp.arange(seq, dtype=F32)
    inv_freq = ROPE_THETA ** (-jnp.arange(0, ROT_DIM, 2, dtype=F32) / ROT_DIM)
    ang = pos[:, None] * inv_freq[None, :]
    cos, sin = jnp.cos(ang), jnp.sin(ang)
    rest = ATTN_HEAD_DIM - ROT_DIM
    cos_h = jnp.concatenate([cos, cos, jnp.ones((seq, rest), F32)], axis=-1)
    sin_h = jnp.concatenate([-sin, sin, jnp.zeros((seq, rest), F32)], axis=-1)
    return jnp.tile(cos_h, (1, ATTN_HEADS)), jnp.tile(sin_h, (1, ATTN_HEADS))


def kernel(x, mix_norm_w, w_in, q_norm_w, k_norm_w, conv_w, conv_b, igate_b, fgate_b, mlstm_norm_w,
           w_attn_branch, w_mlstm_branch, merge_b, w_out, ffn_norm_w, peer_w_query, peer_keys_1,
           peer_keys_2, peer_w_down, peer_w_up):
    b, s, d = x.shape
    t = b * s
    x2d = x.reshape(t, d)

    n_main = 3 * ATTN_W + 4 * MLSTM_W
    n_gate = 2 * MLSTM_HEADS
    w_cat = jnp.concatenate(
        [w_in[:, :n_main], w_in[:, n_main + n_gate:], w_in[:, n_main:n_main + n_gate],
         jnp.zeros((d, LANES - n_gate), w_in.dtype)], axis=1).astype(BF16)
    cos_t, sin_t = _rope_tables(s)
    head_of = jnp.arange(ATTN_W) // ATTN_HEAD_DIM
    bd = (head_of[:, None] == head_of[None, :]).astype(BF16)
    qnw = jnp.tile(q_norm_w, ATTN_HEADS).reshape(1, ATTN_W)
    knw = jnp.tile(k_norm_w, ATTN_HEADS).reshape(1, ATTN_W)

    aq, ak, av, mq, mk, mv, mo, ga, gm, gt = _inproj(
        x2d, mix_norm_w.reshape(1, d), w_cat, cos_t, sin_t, bd, qnw, knw, s)

    attn = _moba(aq.reshape(b, s, ATTN_W), ak.reshape(b, s, ATTN_W), av.reshape(b, s, ATTN_W))

    gates = gt[:, :n_gate].reshape(b, s, 2, MLSTM_HEADS)
    g_col = gates.transpose(0, 3, 1, 2)
    g_row = gates.transpose(0, 3, 2, 1)
    gate_b = jnp.stack([igate_b, fgate_b]).astype(F32)
    r3 = lambda a: a.reshape(b, s, MLSTM_W)
    hm = _mlstm(r3(mq), r3(mk), r3(mv), r3(mo), g_col, g_row, gate_b,
                conv_w[:, :MLSTM_W], conv_w[:, MLSTM_W:], conv_b[:MLSTM_W].reshape(1, MLSTM_W),
                conv_b[MLSTM_W:].reshape(1, MLSTM_W), mlstm_norm_w.reshape(1, MLSTM_W))

    x2, h2, q = _merge(x2d, attn.reshape(t, ATTN_W), hm.reshape(t, MLSTM_W), ga, gm,
                       w_attn_branch.astype(BF16), w_mlstm_branch.astype(BF16), merge_b,
                       w_out.astype(BF16), ffn_norm_w.reshape(1, d), peer_w_query.astype(BF16))

    e1, na, e2, r2 = _select(q, peer_keys_1, peer_keys_2)
    wut = peer_w_up.astype(BF16).reshape(-1, EXPERT_CHUNK, d).transpose(0, 2, 1)
    out = _experts(h2, _pack_rows(peer_w_down.astype(BF16)), _pack_rows(wut), e1, na, e2, r2, x2)
    return out.reshape(b, s, d)
```

```python
import functools

import jax
import jax.numpy as jnp
from jax import lax
from jax.experimental import pallas as pl
from jax.experimental.pallas import tpu as pltpu

F32 = jnp.float32
BF16 = jnp.bfloat16

EPS = 1e-6
ATTN_HEADS = 8
ATTN_HEAD_DIM = 64
ATTN_W = ATTN_HEADS * ATTN_HEAD_DIM
ROT_DIM = ATTN_HEAD_DIM // 4
ROPE_THETA = 500000.0
MOBA_BLOCK = 256
MOBA_TOPK = 3
MLSTM_HEADS = 4
MLSTM_HEAD_DIM = 128
MLSTM_W = MLSTM_HEADS * MLSTM_HEAD_DIM
CONV_WIDTH = 4
PEER_HEADS = 8
PEER_N_KEYS = 128
PEER_HALF = 128
PEER_TOPK = 16

LANES = 128
SUBLANES = 8
NEG = -1e30
VMEM_LIMIT = 56 * 1024 * 1024

INPROJ_TM = 512
MERGE_TM = 512
MOBA_MAX_BLOCKS = 16
MOBA_UNROLL = 4
MLSTM_CHUNK = 256
MLSTM_HEADS_PER_STEP = 4
SELECT_TN = 512
EXPERT_TN = 512
EXPERT_CHUNK = 1024
MXU_PIECES = 2
UP_ROW_PIECES = 4
GATE_GROUPS = 4


def _dot(a, b):
    return jnp.dot(a, b, preferred_element_type=F32)


def _dot_nt(a, b):
    return lax.dot_general(a, b, (((1,), (1,)), ((), ())), preferred_element_type=F32)


def _dot_tn(a, b):
    return lax.dot_general(a, b, (((0,), (0,)), ((), ())), preferred_element_type=F32)


def _bf16_bits(x):
    return lax.bitcast_convert_type(x.astype(BF16).astype(F32), jnp.int32)


def _pack_pairs(lo, hi):
    return lax.shift_right_logical(_bf16_bits(lo), jnp.int32(16)) | _bf16_bits(hi)


def _as_bf16_pairs(words):
    return pltpu.bitcast(words, BF16)


def _unpack_pairs(pairs):
    words = pltpu.bitcast(pairs, jnp.int32)
    lo = lax.bitcast_convert_type(lax.shift_left(words, jnp.int32(16)), F32)
    hi = lax.bitcast_convert_type(words & jnp.int32(-65536), F32)
    return lo, hi


def _pack_rows(x):
    *lead, m, n = x.shape
    pairs = jnp.swapaxes(x.reshape(*lead, m // 2, 2, n), -1, -2)
    return lax.bitcast_convert_type(pairs, jnp.int32)


def _rows_bf16(words):
    return pltpu.bitcast(words, BF16)


def _split2(x):
    hi = x.astype(BF16)
    lo = (x - hi.astype(F32)).astype(BF16)
    return hi, lo


def _params(sem, flags=None):
    return pltpu.CompilerParams(dimension_semantics=sem, vmem_limit_bytes=VMEM_LIMIT, flags=flags)


def _inproj_kernel(x_ref, nw_ref, w_ref, cos_ref, sin_ref, bd_ref, qnw_ref, knw_ref,
                   aq_ref, ak_ref, av_ref, mq_ref, mk_ref, mv_ref, mo_ref, ga_ref, gm_ref, gt_ref):
    x = x_ref[...]
    h = x * lax.rsqrt(jnp.mean(x * x, axis=-1, keepdims=True) + EPS) * nw_ref[...]
    hb = h.astype(BF16)

    def proj(lo, width):
        return _dot(hb, w_ref[:, lo:lo + width])

    lane = lax.broadcasted_iota(jnp.int32, (x.shape[0], ATTN_W), 1) % ATTN_HEAD_DIM
    half = ROT_DIM // 2

    def qk_norm_rope(t, nw):
        hi, lo = _split2(t * t)
        ms = (_dot(hi, bd_ref[...]) + _dot(lo, bd_ref[...])) * (1.0 / ATTN_HEAD_DIM)
        tn = t * lax.rsqrt(ms + EPS) * nw
        fwd = pltpu.roll(tn, ATTN_W - half, axis=1)
        bwd = pltpu.roll(tn, half, axis=1)
        swapped = jnp.where(lane < half, fwd, jnp.where(lane < ROT_DIM, bwd, 0.0))
        return tn * cos_ref[...] + swapped * sin_ref[...]

    aq_ref[...] = qk_norm_rope(proj(0, ATTN_W), qnw_ref[...]).astype(BF16)
    ak_ref[...] = qk_norm_rope(proj(ATTN_W, ATTN_W), knw_ref[...]).astype(BF16)
    av_ref[...] = proj(2 * ATTN_W, ATTN_W).astype(BF16)
    base = 3 * ATTN_W
    mq_ref[...] = proj(base, MLSTM_W).astype(BF16)
    mk_ref[...] = proj(base + MLSTM_W, MLSTM_W).astype(BF16)
    mv_ref[...] = proj(base + 2 * MLSTM_W, MLSTM_W).astype(BF16)
    mo_ref[...] = proj(base + 3 * MLSTM_W, MLSTM_W).astype(BF16)
    base = base + 4 * MLSTM_W
    d = x.shape[1]
    ga_ref[...] = proj(base, d).astype(BF16)
    gm_ref[...] = proj(base + d, d).astype(BF16)
    gt_ref[...] = proj(base + 2 * d, LANES)


def _inproj(x2d, mix_norm_w, w_cat, cos_t, sin_t, bd, qnw, knw, seq):
    t, d = x2d.shape
    tm = INPROJ_TM
    assert t % tm == 0 and seq % tm == 0
    tiles_per_seq = seq // tm
    ncols = w_cat.shape[1]
    row = lambda w: pl.BlockSpec((tm, w), lambda i: (i, 0))
    const = lambda shape: pl.BlockSpec(shape, lambda i: (0, 0))
    pos = pl.BlockSpec((tm, ATTN_W), lambda i: (i % tiles_per_seq, 0))
    outs = [ATTN_W] * 3 + [MLSTM_W] * 4 + [d, d]
    return pl.pallas_call(
        _inproj_kernel,
        grid=(t // tm,),
        in_specs=[row(d), const((1, d)), const((d, ncols)), pos, pos, const((ATTN_W, ATTN_W)),
                  const((1, ATTN_W)), const((1, ATTN_W))],
        out_specs=[row(w) for w in outs] + [row(LANES)],
        out_shape=[jax.ShapeDtypeStruct((t, w), BF16) for w in outs]
        + [jax.ShapeDtypeStruct((t, LANES), F32)],
        compiler_params=_params(("parallel",)),
        name="inproj",
    )(x2d, mix_norm_w, w_cat, cos_t, sin_t, bd, qnw, knw)


def _moba_kernel(q_ref, k_ref, v_ref, o_ref, kmean_ref, kext_ref, vext_ref, s_ref, *, n_blocks):
    i = pl.program_id(2)
    blk = MOBA_BLOCK
    hd = ATTN_HEAD_DIM
    heads = (0, 1)
    lane = lax.broadcasted_iota(jnp.int32, (blk, LANES), 1)
    in_head = [lane < hd, lane >= hd]
    spare0 = [hd, 0]

    @pl.when(i == 0)
    def _():
        kmean_ref[...] = jnp.zeros_like(kmean_ref)
        for n in range(n_blocks):
            rows = slice(n * blk, (n + 1) * blk)
            kb = k_ref[rows, :]
            vb = v_ref[rows, :]
            kmean_ref[n:n + 1, :] = jnp.mean(kb.astype(F32), axis=0, keepdims=True)
            for h in heads:
                onehot = jnp.where(lane == spare0[h] + n, 1.0, 0.0).astype(BF16)
                kext_ref[h, rows, :] = jnp.where(in_head[h], kb, onehot)
                vext_ref[h, rows, :] = jnp.where(in_head[h], vb, jnp.ones_like(vb))

    q = q_ref[...]
    kmean_hi, kmean_lo = _split2(kmean_ref[...])
    nb = MOBA_MAX_BLOCKS
    brow = lax.broadcasted_iota(jnp.int32, (nb, blk), 0)
    qext = []
    for h in heads:
        qh = jnp.where(in_head[h], q, jnp.zeros_like(q)) * jnp.asarray(hd ** -0.5, BF16)
        g_t = (_dot_nt(kmean_hi, qh) + _dot_nt(kmean_lo, qh))[0:nb, :]
        rank = jnp.zeros((nb, blk), F32)
        for m in range(n_blocks):
            gm = g_t[m:m + 1, :]
            beats = jnp.where(gm > g_t, 1.0, jnp.where(gm == g_t, jnp.where(brow > m, 1.0, 0.0), 0.0))
            rank = rank + jnp.where(m < i, beats, 0.0)
        bias_t = jnp.where(brow < i, jnp.where(rank < MOBA_TOPK, 0.0, NEG), 0.0)
        bias = jnp.concatenate([bias_t, jnp.zeros((LANES - nb, blk), F32)], axis=0).T
        if spare0[h]:
            bias = pltpu.roll(bias, spare0[h], axis=1)
        qext.append(qh + bias.astype(BF16))

    own = pl.multiple_of(i * blk, blk)
    causal = (lax.broadcasted_iota(jnp.int32, (blk, blk), 1)
              <= lax.broadcasted_iota(jnp.int32, (blk, blk), 0))
    unroll = MOBA_UNROLL

    def scores(t, m_run):
        m_new = list(m_run)
        for u in range(unroll):
            n = unroll * t + u
            start = pl.multiple_of(n * blk, blk)
            for h in heads:
                s = _dot_nt(qext[h], kext_ref[h, pl.ds(start, blk), :])
                s_ref[h, n] = s
                m_cand = jnp.maximum(m_new[h], jnp.max(s, axis=1, keepdims=True))
                m_new[h] = m_cand if u == 0 else jnp.where(n < i, m_cand, m_new[h])
        return tuple(m_new)

    m_past = lax.fori_loop(0, (i + unroll - 1) // unroll, scores,
                           tuple(jnp.full((blk, 1), NEG, F32) for _ in heads))
    m_fin = []
    for h in heads:
        s = jnp.where(causal, _dot_nt(qext[h], kext_ref[h, pl.ds(own, blk), :]), NEG)
        s_ref[h, i] = s
        m_fin.append(jnp.maximum(m_past[h], jnp.max(s, axis=1, keepdims=True)))

    def values(t, acc):
        out = list(acc)
        for u in range(unroll):
            n = unroll * t + u
            n_read = jnp.minimum(n, i)
            start = pl.multiple_of(n_read * blk, blk)
            for h in heads:
                m_eff = m_fin[h] if u == 0 else jnp.where(n <= i, m_fin[h], -NEG)
                p = jnp.exp(s_ref[h, n_read] - m_eff).astype(BF16)
                out[h] = out[h] + _dot(p, vext_ref[h, pl.ds(start, blk), :])
        return tuple(out)

    acc = lax.fori_loop(0, (i + unroll) // unroll, values,
                        tuple(jnp.zeros((blk, LANES), F32) for _ in heads))
    outs = [a / pltpu.roll(a, hd, axis=1) for a in acc]
    o_ref[...] = jnp.where(in_head[0], outs[0], outs[1]).astype(o_ref.dtype)


def _moba(q, k, v):
    b, s, w = q.shape
    assert s % MOBA_BLOCK == 0 and w % LANES == 0
    n_blocks = s // MOBA_BLOCK
    assert n_blocks <= MOBA_MAX_BLOCKS and n_blocks % MOBA_UNROLL == 0
    qspec = pl.BlockSpec((None, MOBA_BLOCK, LANES), lambda bi, hp, i: (bi, i, hp))
    kvspec = pl.BlockSpec((None, s, LANES), lambda bi, hp, i: (bi, 0, hp))
    return pl.pallas_call(
        functools.partial(_moba_kernel, n_blocks=n_blocks),
        grid=(b, w // LANES, n_blocks),
        in_specs=[qspec, kvspec, kvspec],
        out_specs=qspec,
        out_shape=jax.ShapeDtypeStruct((b, s, w), BF16),
        scratch_shapes=[pltpu.VMEM((LANES, LANES), F32),
                        pltpu.VMEM((2, s, LANES), BF16), pltpu.VMEM((2, s, LANES), BF16),
                        pltpu.VMEM((2, n_blocks, MOBA_BLOCK, MOBA_BLOCK), F32)],
        compiler_params=_params(("parallel", "parallel", "arbitrary")),
        name="moba",
    )(q, k, v)


def _log_sigmoid(f):
    return jnp.minimum(f, 0.0) - jnp.log(1.0 + jnp.exp(-jnp.abs(f)))


def _mlstm_kernel(gb_ref, mq_ref, mk_ref, mv_ref, mo_ref, gc_ref, gr_ref, cwq_ref, cwk_ref,
                  cbq_ref, cbk_ref, nw_ref, o_ref, c_ref, n_ref, m_ref):
    group = pl.program_id(1)
    c = pl.program_id(2)
    L = MLSTM_CHUNK
    d = MLSTM_HEAD_DIM
    heads = range(MLSTM_HEADS_PER_STEP)
    lanes = [slice(h * d, (h + 1) * d) for h in heads]

    @pl.when(c == 0)
    def _():
        c_ref[...] = jnp.zeros_like(c_ref)
        n_ref[...] = jnp.zeros_like(n_ref)
        m_ref[...] = jnp.zeros_like(m_ref)

    start = pl.multiple_of(c * L, L)
    prev = pl.multiple_of(jnp.maximum(c - 1, 0) * L, L)
    row_d = lax.broadcasted_iota(jnp.int32, (L, d), 0)

    def conv_silu(u_ref, w_ref, b_ref, ls):
        u = u_ref[pl.ds(start, L), ls].astype(F32)
        pv = u_ref[pl.ds(prev, L), ls].astype(F32)
        pv = jnp.where(c > 0, pv, 0.0)
        y = u * w_ref[CONV_WIDTH - 1:CONV_WIDTH, ls] + b_ref[:, ls]
        for k in range(1, CONV_WIDTH):
            shifted = jnp.where(row_d < k, pltpu.roll(pv, k, axis=0), pltpu.roll(u, k, axis=0))
            y = y + shifted * w_ref[CONV_WIDTH - 1 - k:CONV_WIDTH - k, ls]
        return y * jax.nn.sigmoid(y)

    qc = [conv_silu(mq_ref, cwq_ref, cbq_ref, ls) for ls in lanes]
    kc = [conv_silu(mk_ref, cwk_ref, cbk_ref, ls) * (d ** -0.5) for ls in lanes]
    qb = [x.astype(BF16) for x in qc]
    kb = [x.astype(BF16) for x in kc]
    vb = [mv_ref[:, ls] for ls in lanes]

    ib = [gb_ref[0, group * MLSTM_HEADS_PER_STEP + h] for h in heads]
    fb = [gb_ref[1, group * MLSTM_HEADS_PER_STEP + h] for h in heads]
    gcol = [gc_ref[h] for h in heads]
    grow = [gr_ref[h] for h in heads]
    ig_col = [gcol[h][:, 0:1] + ib[h] for h in heads]
    ig_row = [grow[h][0:1, :] + ib[h] for h in heads]
    lf_col = [_log_sigmoid(gcol[h][:, 1:2] + fb[h]) for h in heads]
    lf_row = [_log_sigmoid(grow[h][1:2, :] + fb[h]) for h in heads]

    r = lax.broadcasted_iota(jnp.int32, (L, L), 0)
    s = lax.broadcasted_iota(jnp.int32, (L, L), 1)
    causal = s <= r
    tri_l = jnp.where(causal, 1.0, 0.0).astype(BF16)
    tri_u = jnp.where(r <= s, 1.0, 0.0).astype(BF16)
    lfc = [_split2(jnp.broadcast_to(lf_col[h], (L, L))) for h in heads]
    lfr = [_split2(jnp.broadcast_to(lf_row[h], (L, L))) for h in heads]
    b_t = [_dot(tri_l, lfc[h][0]) + _dot(tri_l, lfc[h][1]) for h in heads]
    b_s = [_dot(lfr[h][0], tri_u) + _dot(lfr[h][1], tri_u) for h in heads]

    m_prev = [m_ref[h] for h in heads]
    b_col = [b_t[h][:, 0:1] for h in heads]
    log_d = [jnp.where(causal, b_t[h] - b_s[h] + ig_row[h], NEG) for h in heads]
    m_inter = [b_col[h] + m_prev[h] for h in heads]
    m_t = [jnp.maximum(m_inter[h], jnp.max(log_d[h], axis=1, keepdims=True)) for h in heads]
    a = [jnp.exp(log_d[h] - m_t[h]) * _dot_nt(qb[h], kb[h]) for h in heads]
    w_inter = [jnp.exp(m_inter[h] - m_t[h]) for h in heads]
    c_prev = [c_ref[h] for h in heads]
    n_prev = [n_ref[h] for h in heads]
    num = [_dot(a[h].astype(BF16), vb[h]) + w_inter[h] * _dot(qb[h], c_prev[h].astype(BF16)) for h in heads]
    den = [jnp.sum(a[h], axis=1, keepdims=True)
           + w_inter[h] * jnp.sum(qc[h] * n_prev[h], axis=1, keepdims=True) for h in heads]
    hs = [num[h] / jnp.maximum(jnp.abs(den[h]), jnp.exp(-m_t[h])) for h in heads]

    m_new = [m_t[h][L - 1:L, :] for h in heads]
    b_last = [b_col[h][L - 1:L, :] for h in heads]
    ws_col = [jnp.exp(b_last[h] - b_col[h] + ig_col[h] - m_new[h]) for h in heads]
    decay = [jnp.exp(b_last[h] + m_prev[h] - m_new[h]) for h in heads]
    for h in heads:
        c_ref[h] = decay[h] * c_prev[h] + _dot_tn(kb[h], (ws_col[h] * vb[h].astype(F32)).astype(BF16))
        n_ref[h] = decay[h] * n_prev[h] + jnp.sum(ws_col[h] * kc[h], axis=0, keepdims=True)
        m_ref[h] = m_new[h]

    for h, ls in zip(heads, lanes):
        hn = hs[h] * lax.rsqrt(jnp.mean(hs[h] * hs[h], axis=-1, keepdims=True) + EPS) * nw_ref[:, ls]
        o_ref[:, ls] = (jax.nn.sigmoid(mo_ref[:, ls].astype(F32)) * hn).astype(o_ref.dtype)


def _mlstm(mq, mk, mv, mo, g_col, g_row, gate_b, conv_wq, conv_wk, conv_bq, conv_bk, norm_w):
    b, s, w = mq.shape
    L = MLSTM_CHUNK
    d = MLSTM_HEAD_DIM
    hps = MLSTM_HEADS_PER_STEP
    assert s % L == 0 and (w // d) % hps == 0
    seq = pl.BlockSpec((None, s, hps * d), lambda bi, g, c: (bi, 0, g))
    chunk = pl.BlockSpec((None, L, hps * d), lambda bi, g, c: (bi, c, g))
    per_group = lambda rows: pl.BlockSpec((rows, hps * d), lambda bi, g, c: (0, g))
    return pl.pallas_call(
        _mlstm_kernel,
        grid=(b, w // d // hps, s // L),
        in_specs=[pl.BlockSpec(memory_space=pltpu.SMEM), seq, seq, chunk, chunk,
                  pl.BlockSpec((None, hps, L, 2), lambda bi, g, c: (bi, g, c, 0)),
                  pl.BlockSpec((None, hps, 2, L), lambda bi, g, c: (bi, g, 0, c)),
                  per_group(CONV_WIDTH), per_group(CONV_WIDTH), per_group(1), per_group(1), per_group(1)],
        out_specs=chunk,
        out_shape=jax.ShapeDtypeStruct((b, s, w), BF16),
        scratch_shapes=[pltpu.VMEM((hps, d, d), F32), pltpu.VMEM((hps, 1, d), F32),
                        pltpu.VMEM((hps, 1, 1), F32)],
        compiler_params=_params(("parallel", "parallel", "arbitrary")),
        name="mlstm",
    )(gate_b, mq, mk, mv, mo, g_col, g_row, conv_wq, conv_wk, conv_bq, conv_bk, norm_w)


def _merge_kernel(x_ref, attn_ref, hm_ref, ga_ref, gm_ref, wab_ref, wmb_ref, mb_ref, wout_ref,
                  fnw_ref, wq_ref, x2_ref, h2_ref, q_ref):
    ya = _dot(attn_ref[...], wab_ref[...])
    ym = _dot(hm_ref[...], wmb_ref[...])
    y = (jax.nn.sigmoid(ga_ref[...].astype(F32) + mb_ref[0:1, :]) * ya
         + jax.nn.sigmoid(gm_ref[...].astype(F32) + mb_ref[1:2, :]) * ym)
    x2 = x_ref[...] + _dot(y.astype(BF16), wout_ref[...])
    x2_ref[...] = x2
    h2 = x2 * lax.rsqrt(jnp.mean(x2 * x2, axis=-1, keepdims=True) + EPS) * fnw_ref[...]
    h2b = h2.astype(BF16)
    h2_ref[...] = pltpu.bitcast(h2b, jnp.int32)
    q_ref[...] = _dot(h2b, wq_ref[...])


def _merge(x2d, attn, hm, ga, gm, wab, wmb, merge_b, wout, fnw, wq):
    t, d = x2d.shape
    tm = MERGE_TM
    nq = wq.shape[1]
    row = lambda w: pl.BlockSpec((tm, w), lambda i: (i, 0))
    const = lambda a: pl.BlockSpec(a.shape, lambda i: (0, 0))
    return pl.pallas_call(
        _merge_kernel,
        grid=(t // tm,),
        in_specs=[row(d), row(ATTN_W), row(MLSTM_W), row(d), row(d), const(wab), const(wmb),
                  const(merge_b), const(wout), const(fnw), const(wq)],
        out_specs=[row(d), pl.BlockSpec((tm // 2, d), lambda i: (i, 0)), row(nq)],
        out_shape=[jax.ShapeDtypeStruct((t, d), F32), jax.ShapeDtypeStruct((t // 2, d), jnp.int32),
                   jax.ShapeDtypeStruct((t, nq), F32)],
        compiler_params=_params(("parallel",)),
        name="merge",
    )(x2d, attn, hm, ga, gm, wab, wmb, merge_b, wout, fnw, wq)


_CAND_GROUPS = (
    ((0, 0, 0, 8),), ((0, 8, 0, 8),), ((1, 0, 0, 8),),
    ((2, 0, 0, 5), (4, 0, 5, 3)),
    ((3, 0, 0, 4), (5, 0, 4, 2), (6, 0, 6, 2)),
    ((7, 0, 0, 2),) + tuple((i, 0, i - 6, 1) for i in range(8, 14)),
    ((14, 0, 0, 1), (15, 0, 1, 1)),
)


def _select_kernel(q1_ref, q2_ref, k1_ref, k2_ref, e1_ref, na_ref, e2_ref, r2_ref,
                   s1_ref, s2_ref, w1_ref, w2_ref, rk1_ref, rk2_ref, v_ref, c0_ref, c_ref, e_ref):
    tn = q1_ref.shape[0]
    nk = PEER_N_KEYS
    topk = PEER_TOPK
    n_lt = tn // LANES
    lane_tiles = [slice(lt * LANES, (lt + 1) * LANES) for lt in range(n_lt)]
    halves = ((s1_ref, w1_ref, rk1_ref), (s2_ref, w2_ref, rk2_ref))

    def scores(k_ref, q_ref):
        kh, kl = _split2(k_ref[...])
        qh, ql = _split2(q_ref[...])
        return _dot_nt(kh, qh) + _dot_nt(kh, ql) + _dot_nt(kl, qh)

    s1_ref[...] = scores(k1_ref, q1_ref)
    s2_ref[...] = scores(k2_ref, q2_ref)
    key = lax.broadcasted_iota(jnp.int32, (nk, LANES), 0).astype(F32)

    def extract(break_ties):
        for s_ref, w_ref, rk_ref in halves:
            w_ref[...] = s_ref[...]
            rk_ref[...] = jnp.full(rk_ref.shape, float(topk), F32)
        for it in range(topk):
            for half, (_, w_ref, rk_ref) in enumerate(halves):
                for cols in lane_tiles:
                    s_cur = w_ref[:, cols]
                    m = jnp.max(s_cur, axis=0, keepdims=True)
                    if break_ties:
                        hit = key == jnp.min(jnp.where(s_cur == m, key, float(nk)), axis=0, keepdims=True)
                    else:
                        hit = s_cur == m
                    w_ref[:, cols] = jnp.where(hit, -jnp.inf, s_cur)
                    rk_ref[:, cols] = jnp.where(hit, float(it), rk_ref[:, cols])
                    v_ref[half, it, :, cols] = m

    extract(break_ties=False)
    off_count = jnp.zeros((1, LANES), F32)
    for _, _, rk_ref in halves:
        for cols in lane_tiles:
            taken = jnp.sum(jnp.where(rk_ref[:, cols] < float(topk), 1.0, 0.0), axis=0, keepdims=True)
            off_count = off_count + jnp.where(taken == float(topk), 0.0, 1.0)

    @pl.when(jnp.max(off_count) > 0.0)
    def _():
        extract(break_ties=True)

    row8 = lax.broadcasted_iota(jnp.int32, (SUBLANES, LANES), 0)
    seg_mask = {seg: (row8 >= seg[2]) & (row8 < seg[2] + seg[3]) for grp in _CAND_GROUPS for seg in grp}
    flats = []
    for grp in _CAND_GROUPS:
        f = jnp.full((SUBLANES, LANES), 1e9, F32)
        for seg in grp:
            i, j0, r0, _ = seg
            f = jnp.where(seg_mask[seg], (row8 + (i * topk + j0 - r0)).astype(F32), f)
        flats.append(f)
    flat = jnp.concatenate(flats, axis=0)
    for cols in lane_tiles:
        v1 = [v_ref[0, i, :, cols] for i in range(topk)]
        v2 = [v_ref[1, j, :, cols] for j in range(topk)]
        v2_groups = []
        for j0 in (0, SUBLANES):
            dense = jnp.zeros((SUBLANES, LANES), F32)
            for r in range(SUBLANES):
                dense = jnp.where(row8 == r, v2[j0 + r], dense)
            v2_groups.append(dense)
        top = v1[0] + v2[0]
        for g, grp in enumerate(_CAND_GROUPS):
            cand = jnp.full((SUBLANES, LANES), -jnp.inf, F32)
            for seg in grp:
                i, j0, r0, n = seg
                if n == 1 and j0 == 0:
                    piece = v1[i] + v2[0]
                else:
                    src = v2_groups[j0 // SUBLANES]
                    piece = v1[i] + (pltpu.roll(src, r0, axis=0) if r0 else src)
                cand = jnp.where(seg_mask[seg], piece, cand)
            rows = slice(g * SUBLANES, (g + 1) * SUBLANES)
            c0_ref[rows, cols] = cand
            e_ref[rows, cols] = jnp.exp(cand - top)

    def pick(break_ties):
        c_ref[...] = c0_ref[...]
        for _ in range(topk):
            for cols in lane_tiles:
                c_cur = c_ref[:, cols]
                m = jnp.max(c_cur, axis=0, keepdims=True)
                if break_ties:
                    hit = flat == jnp.min(jnp.where(c_cur == m, flat, 1e9), axis=0, keepdims=True)
                else:
                    hit = c_cur == m
                c_ref[:, cols] = jnp.where(hit, -jnp.inf, c_cur)

    pick(break_ties=False)
    off_count = jnp.zeros((1, LANES), F32)
    for cols in lane_tiles:
        taken = jnp.sum(jnp.where(flat < 1e8, jnp.where(c_ref[:, cols] == -jnp.inf, 1.0, 0.0), 0.0),
                        axis=0, keepdims=True)
        off_count = off_count + jnp.where(taken == float(topk), 0.0, 1.0)

    @pl.when(jnp.max(off_count) > 0.0)
    def _():
        pick(break_ties=True)

    for lt, cols in enumerate(lane_tiles):
        chosen = jnp.where(flat < 1e8, jnp.where(c_ref[:, cols] == -jnp.inf, 1.0, 0.0), 0.0)
        z = jnp.sum(chosen * e_ref[:, cols], axis=0, keepdims=True)
        counts = [jnp.zeros((1, LANES), F32) for _ in range(topk)]
        for g, grp in enumerate(_CAND_GROUPS):
            ch = chosen[g * SUBLANES:(g + 1) * SUBLANES, :]
            for seg in grp:
                counts[seg[0]] = counts[seg[0]] + jnp.sum(
                    jnp.where(seg_mask[seg], ch, 0.0), axis=0, keepdims=True)
        v1_top = v_ref[0, 0, :, cols]
        v2_top = v_ref[1, 0, :, cols]

        r1 = rk1_ref[:, cols]
        r2 = rk2_ref[:, cols]
        n_a = jnp.zeros((nk, LANES), F32)
        for i in range(topk):
            n_a = jnp.where(r1 == float(i), counts[i], n_a)
        in1 = r1 < float(topk)
        in2 = r2 < float(topk)
        e1 = jnp.where(in1, jnp.exp(s1_ref[:, cols] - v1_top) * (0.5 / z), 0.0)
        e2 = jnp.where(in2, jnp.exp(s2_ref[:, cols] - v2_top), 0.0)
        e1_ref[:, cols] = _pack_pairs(e1, e1)
        na_ref[:, cols] = _pack_pairs(n_a, n_a)
        e2_ref[:, cols] = _pack_pairs(e2[:nk // 2], e2[nk // 2:])
        r2_ref[:, cols] = _pack_pairs(r2[:nk // 2], r2[nk // 2:])


def _select(q, keys_1, keys_2):
    t = q.shape[0]
    tn = SELECT_TN
    assert t % tn == 0
    nk = PEER_N_KEYS
    q1 = pl.BlockSpec((tn, PEER_HALF), lambda i, h: (i, 2 * h))
    q2 = pl.BlockSpec((tn, PEER_HALF), lambda i, h: (i, 2 * h + 1))
    kspec = pl.BlockSpec((None, nk, PEER_HALF), lambda i, h: (h, 0, 0))
    ospec = lambda rows: pl.BlockSpec((None, rows, tn), lambda i, h: (h, 0, i))
    oshape = lambda rows: jax.ShapeDtypeStruct((PEER_HEADS, rows, t), jnp.int32)
    keys_buf = pltpu.VMEM((nk, tn), F32)
    return pl.pallas_call(
        _select_kernel,
        grid=(t // tn, PEER_HEADS),
        in_specs=[q1, q2, kspec, kspec],
        out_specs=[ospec(nk), ospec(nk), ospec(nk // 2), ospec(nk // 2)],
        out_shape=[oshape(nk), oshape(nk), oshape(nk // 2), oshape(nk // 2)],
        scratch_shapes=[keys_buf] * 6 + [pltpu.VMEM((2, PEER_TOPK, 1, tn), F32)]
        + [pltpu.VMEM((SUBLANES * len(_CAND_GROUPS), tn), F32)] * 3,
        compiler_params=_params(("parallel", "parallel")),
        name="peer_select",
    )(q, q, keys_1, keys_2)


def _experts_kernel(h2_ref, wd_ref, wut_ref, e1_in, na_in, e2_in, r2_in, x2_ref, o_ref,
                    acc_ref, z0_ref, z1_ref, act0_ref, act1_ref, e1_ref, na_ref, e2_ref, r2_ref):
    s = pl.program_id(1)
    n_chunks = pl.num_programs(1) - 2
    nk = PEER_N_KEYS
    tn = z0_ref.shape[1]
    rows_per_chunk = EXPERT_CHUNK // nk

    @pl.when(s == 0)
    def _():
        for ref in (acc_ref, z0_ref, z1_ref, act0_ref, act1_ref):
            ref[...] = jnp.zeros_like(ref)
        e2_ref[...] = e2_in[...]
        r2_ref[...] = r2_in[...]

    chunk = jnp.clip(s - 1, 0, n_chunks - 1)
    r0 = (chunk % (SUBLANES // rows_per_chunk)) * rows_per_chunk
    for h in range(PEER_HEADS):
        for aa in range(rows_per_chunk):
            e1_ref[h, aa:aa + 1, :] = e1_in[h, pl.ds(r0 + aa, 1), :]
            na_ref[h, aa:aa + 1, :] = na_in[h, pl.ds(r0 + aa, 1), :]

    half = nk // 2
    groups = half // SUBLANES

    def gate_piece(lt, k_lo, k_hi, z_r, act_w):
        cols = slice(lt * LANES, (lt + 1) * LANES)
        g = [[jnp.zeros((2 * SUBLANES, LANES), BF16) for _ in range(k_lo, k_hi)]
             for _ in range(rows_per_chunk)]
        for h in range(PEER_HEADS):
            as_pair = lambda ref, aa: _as_bf16_pairs(
                jnp.broadcast_to(ref[h, aa:aa + 1, cols], (SUBLANES, LANES)))
            e1 = [as_pair(e1_ref, aa) for aa in range(rows_per_chunk)]
            n_a = [as_pair(na_ref, aa) for aa in range(rows_per_chunk)]
            for k in range(k_lo, k_hi):
                grp = slice(k * SUBLANES, (k + 1) * SUBLANES)
                r2 = _as_bf16_pairs(r2_ref[h, grp, cols])
                e2 = _as_bf16_pairs(e2_ref[h, grp, cols])
                for aa in range(rows_per_chunk):
                    g[aa][k - k_lo] = (g[aa][k - k_lo]
                                       + jnp.where(r2 < n_a[aa], e2, jnp.zeros_like(e2)) * e1[aa])
        for aa in range(rows_per_chunk):
            for k in range(k_lo, k_hi):
                g_lo, g_hi = _unpack_pairs(g[aa][k - k_lo])
                for g_half, base in ((g_lo, aa * nk), (g_hi, aa * nk + half)):
                    rows = slice(base + k * SUBLANES, base + (k + 1) * SUBLANES)
                    z = z_r[rows, cols]
                    act = z * (1.0 + lax.erf(z * (2.0 ** -0.5)))
                    act_w[rows, cols] = (act * g_half).astype(BF16)

    def step(z_w, z_r, act_w, act_r):
        d = acc_ref.shape[0]
        mxu = []
        for p in range(MXU_PIECES):
            pc = slice(p * tn // MXU_PIECES, (p + 1) * tn // MXU_PIECES)
            pw = slice(pc.start // 2, pc.stop // 2)

            def z_piece(pc=pc, pw=pw):
                z_w[:, pc] = _dot_nt(_rows_bf16(wd_ref[...]), _rows_bf16(h2_ref[pw, :]))

            mxu.append(z_piece)
            for r in range(UP_ROW_PIECES):
                rs = slice(r * d // UP_ROW_PIECES, (r + 1) * d // UP_ROW_PIECES)
                rw = slice(rs.start // 2, rs.stop // 2)

                def up_piece(pc=pc, rs=rs, rw=rw):
                    acc_ref[rs, pc] += _dot(_rows_bf16(wut_ref[rw, :]), act_r[:, pc])

                mxu.append(up_piece)
        gates = [(lt, k, k + GATE_GROUPS) for lt in range(tn // LANES)
                 for k in range(0, groups, GATE_GROUPS)]
        per_mxu = -(-len(gates) // len(mxu))
        for m, piece in enumerate(mxu):
            piece()
            for lt, k_lo, k_hi in gates[m * per_mxu:(m + 1) * per_mxu]:
                gate_piece(lt, k_lo, k_hi, z_r, act_w)
        for lt, k_lo, k_hi in gates[len(mxu) * per_mxu:]:
            gate_piece(lt, k_lo, k_hi, z_r, act_w)

    @pl.when(s % 2 == 0)
    def _():
        step(z0_ref, z1_ref, act1_ref, act0_ref)

    @pl.when(s % 2 == 1)
    def _():
        step(z1_ref, z0_ref, act0_ref, act1_ref)

    @pl.when(s == pl.num_programs(1) - 1)
    def _():
        o_ref[...] = x2_ref[...] + acc_ref[...].T


def _experts(h2, wd, wut, e1, na, e2, r2, x2):
    t, d = x2.shape
    tn = EXPERT_TN
    ne = 2 * wd.shape[0]
    nk = PEER_N_KEYS
    assert t % tn == 0 and ne % EXPERT_CHUNK == 0 and EXPERT_CHUNK % nk == 0
    n_chunks = ne // EXPERT_CHUNK
    assert n_chunks % 2 == 0
    tok = pl.BlockSpec((tn, d), lambda i, s: (i, 0))
    gate = pl.BlockSpec((PEER_HEADS, nk // 2, tn), lambda i, s: (0, 0, i))
    rpc = EXPERT_CHUNK // nk
    assert SUBLANES % rpc == 0
    chunks_per_group = SUBLANES // rpc
    row = pl.BlockSpec((PEER_HEADS, None, SUBLANES, tn),
                       lambda i, s: (0, jnp.clip(s - 1, 0, n_chunks - 1) // chunks_per_group, 0, i))
    e1 = e1.reshape(PEER_HEADS, nk // SUBLANES, SUBLANES, t)
    na = na.reshape(PEER_HEADS, nk // SUBLANES, SUBLANES, t)
    return pl.pallas_call(
        _experts_kernel,
        grid=(t // tn, n_chunks + 2),
        in_specs=[pl.BlockSpec((tn // 2, d), lambda i, s: (i, 0)),
                  pl.BlockSpec((EXPERT_CHUNK // 2, d), lambda i, s: (jnp.minimum(s, n_chunks - 1), 0)),
                  pl.BlockSpec((None, d // 2, EXPERT_CHUNK),
                               lambda i, s: (jnp.clip(s - 2, 0, n_chunks - 1), 0, 0)),
                  row, row, gate, gate, tok],
        out_specs=tok,
        out_shape=jax.ShapeDtypeStruct((t, d), F32),
        scratch_shapes=[pltpu.VMEM((d, tn), F32),
                        pltpu.VMEM((EXPERT_CHUNK, tn), F32), pltpu.VMEM((EXPERT_CHUNK, tn), F32),
                        pltpu.VMEM((EXPERT_CHUNK, tn), BF16), pltpu.VMEM((EXPERT_CHUNK, tn), BF16),
                        pltpu.VMEM((PEER_HEADS, rpc, tn), jnp.int32), pltpu.VMEM((PEER_HEADS, rpc, tn), jnp.int32),
                        pltpu.VMEM((PEER_HEADS, nk // 2, tn), jnp.int32),
                        pltpu.VMEM((PEER_HEADS, nk // 2, tn), jnp.int32)],
        compiler_params=_params(("parallel", "arbitrary")),
        name="peer_experts",
    )(h2, wd, wut, e1, na, e2, r2, x2)


def _rope_tables(seq):
    pos = jnp.arange(seq, dtype=F32)
    inv_freq = ROPE_THETA ** (-jnp.arange(0, ROT_DIM, 2, dtype=F32) / ROT_DIM)
    ang = pos[:, None] * inv_freq[None, :]
    cos, sin = jnp.cos(ang), jnp.sin(ang)
    rest = ATTN_HEAD_DIM - ROT_DIM
    cos_h = jnp.concatenate([cos, cos, jnp.ones((seq, rest), F32)], axis=-1)
    sin_h = jnp.concatenate([-sin, sin, jnp.zeros((seq, rest), F32)], axis=-1)
    return jnp.tile(cos_h, (1, ATTN_HEADS)), jnp.tile(sin_h, (1, ATTN_HEADS))


def kernel(x, mix_norm_w, w_in, q_norm_w, k_norm_w, conv_w, conv_b, igate_b, fgate_b, mlstm_norm_w,
           w_attn_branch, w_mlstm_branch, merge_b, w_out, ffn_norm_w, peer_w_query, peer_keys_1,
           peer_keys_2, peer_w_down, peer_w_up):
    b, s, d = x.shape
    t = b * s
    x2d = x.reshape(t, d)

    n_main = 3 * ATTN_W + 4 * MLSTM_W
    n_gate = 2 * MLSTM_HEADS
    w_cat = jnp.concatenate(
        [w_in[:, :n_main], w_in[:, n_main + n_gate:], w_in[:, n_main:n_main + n_gate],
         jnp.zeros((d, LANES - n_gate), w_in.dtype)], axis=1).astype(BF16)
    cos_t, sin_t = _rope_tables(s)
    head_of = jnp.arange(ATTN_W) // ATTN_HEAD_DIM
    bd = (head_of[:, None] == head_of[None, :]).astype(BF16)
    qnw = jnp.tile(q_norm_w, ATTN_HEADS).reshape(1, ATTN_W)
    knw = jnp.tile(k_norm_w, ATTN_HEADS).reshape(1, ATTN_W)

    aq, ak, av, mq, mk, mv, mo, ga, gm, gt = _inproj(
        x2d, mix_norm_w.reshape(1, d), w_cat, cos_t, sin_t, bd, qnw, knw, s)

    attn = _moba(aq.reshape(b, s, ATTN_W), ak.reshape(b, s, ATTN_W), av.reshape(b, s, ATTN_W))

    gates = gt[:, :n_gate].reshape(b, s, 2, MLSTM_HEADS)
    g_col = gates.transpose(0, 3, 1, 2)
    g_row = gates.transpose(0, 3, 2, 1)
    gate_b = jnp.stack([igate_b, fgate_b]).astype(F32)
    r3 = lambda a: a.reshape(b, s, MLSTM_W)
    hm = _mlstm(r3(mq), r3(mk), r3(mv), r3(mo), g_col, g_row, gate_b,
                conv_w[:, :MLSTM_W], conv_w[:, MLSTM_W:], conv_b[:MLSTM_W].reshape(1, MLSTM_W),
                conv_b[MLSTM_W:].reshape(1, MLSTM_W), mlstm_norm_w.reshape(1, MLSTM_W))

    x2, h2, q = _merge(x2d, attn.reshape(t, ATTN_W), hm.reshape(t, MLSTM_W), ga, gm,
                       w_attn_branch.astype(BF16), w_mlstm_branch.astype(BF16), merge_b,
                       w_out.astype(BF16), ffn_norm_w.reshape(1, d), peer_w_query.astype(BF16))

    e1, na, e2, r2 = _select(q, peer_keys_1, peer_keys_2)
    wut = peer_w_up.astype(BF16).reshape(-1, EXPERT_CHUNK, d).transpose(0, 2, 1)
    out = _experts(h2, _pack_rows(peer_w_down.astype(BF16)), _pack_rows(wut), e1, na, e2, r2, x2)
    return out.reshape(b, s, d)
```

```python
import functools

import jax
import jax.numpy as jnp
from jax import lax
from jax.experimental import pallas as pl
from jax.experimental.pallas import tpu as pltpu

F32 = jnp.float32
BF16 = jnp.bfloat16

EPS = 1e-6
ATTN_HEADS = 8
ATTN_HEAD_DIM = 64
ATTN_W = ATTN_HEADS * ATTN_HEAD_DIM
ROT_DIM = ATTN_HEAD_DIM // 4
ROPE_THETA = 500000.0
MOBA_BLOCK = 256
MOBA_TOPK = 3
MLSTM_HEADS = 4
MLSTM_HEAD_DIM = 128
MLSTM_W = MLSTM_HEADS * MLSTM_HEAD_DIM
CONV_WIDTH = 4
PEER_HEADS = 8
PEER_N_KEYS = 128
PEER_HALF = 128
PEER_TOPK = 16

LANES = 128
SUBLANES = 8
NEG = -1e30
VMEM_LIMIT = 56 * 1024 * 1024

INPROJ_TM = 512
MERGE_TM = 512
MOBA_MAX_BLOCKS = 16
MOBA_UNROLL = 4
MLSTM_CHUNK = 256
MLSTM_HEADS_PER_STEP = 4
SELECT_TN = 512
EXPERT_TN = 512
EXPERT_CHUNK = 1024
MXU_PIECES = 2
UP_ROW_PIECES = 4
GATE_GROUPS = 4


def _dot(a, b):
    return jnp.dot(a, b, preferred_element_type=F32)


def _dot_nt(a, b):
    return lax.dot_general(a, b, (((1,), (1,)), ((), ())), preferred_element_type=F32)


def _dot_tn(a, b):
    return lax.dot_general(a, b, (((0,), (0,)), ((), ())), preferred_element_type=F32)


def _bf16_bits(x):
    return lax.bitcast_convert_type(x.astype(BF16).astype(F32), jnp.int32)


def _pack_pairs(lo, hi):
    return lax.shift_right_logical(_bf16_bits(lo), jnp.int32(16)) | _bf16_bits(hi)


def _as_bf16_pairs(words):
    return pltpu.bitcast(words, BF16)


def _unpack_pairs(pairs):
    words = pltpu.bitcast(pairs, jnp.int32)
    lo = lax.bitcast_convert_type(lax.shift_left(words, jnp.int32(16)), F32)
    hi = lax.bitcast_convert_type(words & jnp.int32(-65536), F32)
    return lo, hi


def _pack_rows(x):
    *lead, m, n = x.shape
    pairs = jnp.swapaxes(x.reshape(*lead, m // 2, 2, n), -1, -2)
    return lax.bitcast_convert_type(pairs, jnp.int32)


def _rows_bf16(words):
    return pltpu.bitcast(words, BF16)


def _split2(x):
    hi = x.astype(BF16)
    lo = (x - hi.astype(F32)).astype(BF16)
    return hi, lo


def _params(sem, flags=None):
    return pltpu.CompilerParams(dimension_semantics=sem, vmem_limit_bytes=VMEM_LIMIT, flags=flags)


def _inproj_kernel(x_ref, nw_ref, w_ref, cos_ref, sin_ref, bd_ref, qnw_ref, knw_ref,
                   aq_ref, ak_ref, av_ref, mq_ref, mk_ref, mv_ref, mo_ref, ga_ref, gm_ref, gt_ref):
    x = x_ref[...]
    h = x * lax.rsqrt(jnp.mean(x * x, axis=-1, keepdims=True) + EPS) * nw_ref[...]
    hb = h.astype(BF16)

    def proj(lo, width):
        return _dot(hb, w_ref[:, lo:lo + width])

    lane = lax.broadcasted_iota(jnp.int32, (x.shape[0], ATTN_W), 1) % ATTN_HEAD_DIM
    half = ROT_DIM // 2

    def qk_norm_rope(t, nw):
        hi, lo = _split2(t * t)
        ms = (_dot(hi, bd_ref[...]) + _dot(lo, bd_ref[...])) * (1.0 / ATTN_HEAD_DIM)
        tn = t * lax.rsqrt(ms + EPS) * nw
        fwd = pltpu.roll(tn, ATTN_W - half, axis=1)
        bwd = pltpu.roll(tn, half, axis=1)
        swapped = jnp.where(lane < half, fwd, jnp.where(lane < ROT_DIM, bwd, 0.0))
        return tn * cos_ref[...] + swapped * sin_ref[...]

    aq_ref[...] = qk_norm_rope(proj(0, ATTN_W), qnw_ref[...]).astype(BF16)
    ak_ref[...] = qk_norm_rope(proj(ATTN_W, ATTN_W), knw_ref[...]).astype(BF16)
    av_ref[...] = proj(2 * ATTN_W, ATTN_W).astype(BF16)
    base = 3 * ATTN_W
    mq_ref[...] = proj(base, MLSTM_W).astype(BF16)
    mk_ref[...] = proj(base + MLSTM_W, MLSTM_W).astype(BF16)
    mv_ref[...] = proj(base + 2 * MLSTM_W, MLSTM_W).astype(BF16)
    mo_ref[...] = proj(base + 3 * MLSTM_W, MLSTM_W).astype(BF16)
    base = base + 4 * MLSTM_W
    d = x.shape[1]
    ga_ref[...] = proj(base, d).astype(BF16)
    gm_ref[...] = proj(base + d, d).astype(BF16)
    gt_ref[...] = proj(base + 2 * d, LANES)


def _inproj(x2d, mix_norm_w, w_cat, cos_t, sin_t, bd, qnw, knw, seq):
    t, d = x2d.shape
    tm = INPROJ_TM
    assert t % tm == 0 and seq % tm == 0
    tiles_per_seq = seq // tm
    ncols = w_cat.shape[1]
    row = lambda w: pl.BlockSpec((tm, w), lambda i: (i, 0))
    const = lambda shape: pl.BlockSpec(shape, lambda i: (0, 0))
    pos = pl.BlockSpec((tm, ATTN_W), lambda i: (i % tiles_per_seq, 0))
    outs = [ATTN_W] * 3 + [MLSTM_W] * 4 + [d, d]
    return pl.pallas_call(
        _inproj_kernel,
        grid=(t // tm,),
        in_specs=[row(d), const((1, d)), const((d, ncols)), pos, pos, const((ATTN_W, ATTN_W)),
                  const((1, ATTN_W)), const((1, ATTN_W))],
        out_specs=[row(w) for w in outs] + [row(LANES)],
        out_shape=[jax.ShapeDtypeStruct((t, w), BF16) for w in outs]
        + [jax.ShapeDtypeStruct((t, LANES), F32)],
        compiler_params=_params(("parallel",)),
        name="inproj",
    )(x2d, mix_norm_w, w_cat, cos_t, sin_t, bd, qnw, knw)


def _moba_kernel(q_ref, k_ref, v_ref, o_ref, kmean_ref, kext_ref, vext_ref, s_ref, *, n_blocks):
    i = pl.program_id(2)
    blk = MOBA_BLOCK
    hd = ATTN_HEAD_DIM
    heads = (0, 1)
    lane = lax.broadcasted_iota(jnp.int32, (blk, LANES), 1)
    in_head = [lane < hd, lane >= hd]
    spare0 = [hd, 0]

    @pl.when(i == 0)
    def _():
        kmean_ref[...] = jnp.zeros_like(kmean_ref)
        for n in range(n_blocks):
            rows = slice(n * blk, (n + 1) * blk)
            kb = k_ref[rows, :]
            vb = v_ref[rows, :]
            kmean_ref[n:n + 1, :] = jnp.mean(kb.astype(F32), axis=0, keepdims=True)
            for h in heads:
                onehot = jnp.where(lane == spare0[h] + n, 1.0, 0.0).astype(BF16)
                kext_ref[h, rows, :] = jnp.where(in_head[h], kb, onehot)
                vext_ref[h, rows, :] = jnp.where(in_head[h], vb, jnp.ones_like(vb))

    q = q_ref[...]
    kmean_hi, kmean_lo = _split2(kmean_ref[...])
    nb = MOBA_MAX_BLOCKS
    brow = lax.broadcasted_iota(jnp.int32, (nb, blk), 0)
    qext = []
    for h in heads:
        qh = jnp.where(in_head[h], q, jnp.zeros_like(q)) * jnp.asarray(hd ** -0.5, BF16)
        g_t = (_dot_nt(kmean_hi, qh) + _dot_nt(kmean_lo, qh))[0:nb, :]
        rank = jnp.zeros((nb, blk), F32)
        for m in range(n_blocks):
            gm = g_t[m:m + 1, :]
            beats = jnp.where(gm > g_t, 1.0, jnp.where(gm == g_t, jnp.where(brow > m, 1.0, 0.0), 0.0))
            rank = rank + jnp.where(m < i, beats, 0.0)
        bias_t = jnp.where(brow < i, jnp.where(rank < MOBA_TOPK, 0.0, NEG), 0.0)
        bias = jnp.concatenate([bias_t, jnp.zeros((LANES - nb, blk), F32)], axis=0).T
        if spare0[h]:
            bias = pltpu.roll(bias, spare0[h], axis=1)
        qext.append(qh + bias.astype(BF16))

    own = pl.multiple_of(i * blk, blk)
    causal = (lax.broadcasted_iota(jnp.int32, (blk, blk), 1)
              <= lax.broadcasted_iota(jnp.int32, (blk, blk), 0))
    unroll = MOBA_UNROLL

    def scores(t, m_run):
        m_new = list(m_run)
        for u in range(unroll):
            n = unroll * t + u
            start = pl.multiple_of(n * blk, blk)
            for h in heads:
                s = _dot_nt(qext[h], kext_ref[h, pl.ds(start, blk), :])
                s_ref[h, n] = s
                m_cand = jnp.maximum(m_new[h], jnp.max(s, axis=1, keepdims=True))
                m_new[h] = m_cand if u == 0 else jnp.where(n < i, m_cand, m_new[h])
        return tuple(m_new)

    m_past = lax.fori_loop(0, (i + unroll - 1) // unroll, scores,
                           tuple(jnp.full((blk, 1), NEG, F32) for _ in heads))
    m_fin = []
    for h in heads:
        s = jnp.where(causal, _dot_nt(qext[h], kext_ref[h, pl.ds(own, blk), :]), NEG)
        s_ref[h, i] = s
        m_fin.append(jnp.maximum(m_past[h], jnp.max(s, axis=1, keepdims=True)))

    def values(t, acc):
        out = list(acc)
        for u in range(unroll):
            n = unroll * t + u
            n_read = jnp.minimum(n, i)
            start = pl.multiple_of(n_read * blk, blk)
            for h in heads:
                m_eff = m_fin[h] if u == 0 else jnp.where(n <= i, m_fin[h], -NEG)
                p = jnp.exp(s_ref[h, n_read] - m_eff).astype(BF16)
                out[h] = out[h] + _dot(p, vext_ref[h, pl.ds(start, blk), :])
        return tuple(out)

    acc = lax.fori_loop(0, (i + unroll) // unroll, values,
                        tuple(jnp.zeros((blk, LANES), F32) for _ in heads))
    outs = [a / pltpu.roll(a, hd, axis=1) for a in acc]
    o_ref[...] = jnp.where(in_head[0], outs[0], outs[1]).astype(o_ref.dtype)


def _moba(q, k, v):
    b, s, w = q.shape
    assert s % MOBA_BLOCK == 0 and w % LANES == 0
    n_blocks = s // MOBA_BLOCK
    assert n_blocks <= MOBA_MAX_BLOCKS and n_blocks % MOBA_UNROLL == 0
    qspec = pl.BlockSpec((None, MOBA_BLOCK, LANES), lambda bi, hp, i: (bi, i, hp))
    kvspec = pl.BlockSpec((None, s, LANES), lambda bi, hp, i: (bi, 0, hp))
    return pl.pallas_call(
        functools.partial(_moba_kernel, n_blocks=n_blocks),
        grid=(b, w // LANES, n_blocks),
        in_specs=[qspec, kvspec, kvspec],
        out_specs=qspec,
        out_shape=jax.ShapeDtypeStruct((b, s, w), BF16),
        scratch_shapes=[pltpu.VMEM((LANES, LANES), F32),
                        pltpu.VMEM((2, s, LANES), BF16), pltpu.VMEM((2, s, LANES), BF16),
                        pltpu.VMEM((2, n_blocks, MOBA_BLOCK, MOBA_BLOCK), F32)],
        compiler_params=_params(("parallel", "parallel", "arbitrary")),
        name="moba",
    )(q, k, v)


def _log_sigmoid(f):
    return jnp.minimum(f, 0.0) - jnp.log(1.0 + jnp.exp(-jnp.abs(f)))


def _mlstm_kernel(gb_ref, mq_ref, mk_ref, mv_ref, mo_ref, gc_ref, gr_ref, cwq_ref, cwk_ref,
                  cbq_ref, cbk_ref, nw_ref, o_ref, c_ref, n_ref, m_ref):
    group = pl.program_id(1)
    c = pl.program_id(2)
    L = MLSTM_CHUNK
    d = MLSTM_HEAD_DIM
    heads = range(MLSTM_HEADS_PER_STEP)
    lanes = [slice(h * d, (h + 1) * d) for h in heads]

    @pl.when(c == 0)
    def _():
        c_ref[...] = jnp.zeros_like(c_ref)
        n_ref[...] = jnp.zeros_like(n_ref)
        m_ref[...] = jnp.zeros_like(m_ref)

    start = pl.multiple_of(c * L, L)
    prev = pl.multiple_of(jnp.maximum(c - 1, 0) * L, L)
    row_d = lax.broadcasted_iota(jnp.int32, (L, d), 0)

    def conv_silu(u_ref, w_ref, b_ref, ls):
        u = u_ref[pl.ds(start, L), ls].astype(F32)
        pv = u_ref[pl.ds(prev, L), ls].astype(F32)
        pv = jnp.where(c > 0, pv, 0.0)
        y = u * w_ref[CONV_WIDTH - 1:CONV_WIDTH, ls] + b_ref[:, ls]
        for k in range(1, CONV_WIDTH):
            shifted = jnp.where(row_d < k, pltpu.roll(pv, k, axis=0), pltpu.roll(u, k, axis=0))
            y = y + shifted * w_ref[CONV_WIDTH - 1 - k:CONV_WIDTH - k, ls]
        return y * jax.nn.sigmoid(y)

    qc = [conv_silu(mq_ref, cwq_ref, cbq_ref, ls) for ls in lanes]
    kc = [conv_silu(mk_ref, cwk_ref, cbk_ref, ls) * (d ** -0.5) for ls in lanes]
    qb = [x.astype(BF16) for x in qc]
    kb = [x.astype(BF16) for x in kc]
    vb = [mv_ref[:, ls] for ls in lanes]

    ib = [gb_ref[0, group * MLSTM_HEADS_PER_STEP + h] for h in heads]
    fb = [gb_ref[1, group * MLSTM_HEADS_PER_STEP + h] for h in heads]
    gcol = [gc_ref[h] for h in heads]
    grow = [gr_ref[h] for h in heads]
    ig_col = [gcol[h][:, 0:1] + ib[h] for h in heads]
    ig_row = [grow[h][0:1, :] + ib[h] for h in heads]
    lf_col = [_log_sigmoid(gcol[h][:, 1:2] + fb[h]) for h in heads]
    lf_row = [_log_sigmoid(grow[h][1:2, :] + fb[h]) for h in heads]

    r = lax.broadcasted_iota(jnp.int32, (L, L), 0)
    s = lax.broadcasted_iota(jnp.int32, (L, L), 1)
    causal = s <= r
    tri_l = jnp.where(causal, 1.0, 0.0).astype(BF16)
    tri_u = jnp.where(r <= s, 1.0, 0.0).astype(BF16)
    lfc = [_split2(jnp.broadcast_to(lf_col[h], (L, L))) for h in heads]
    lfr = [_split2(jnp.broadcast_to(lf_row[h], (L, L))) for h in heads]
    b_t = [_dot(tri_l, lfc[h][0]) + _dot(tri_l, lfc[h][1]) for h in heads]
    b_s = [_dot(lfr[h][0], tri_u) + _dot(lfr[h][1], tri_u) for h in heads]

    m_prev = [m_ref[h] for h in heads]
    b_col = [b_t[h][:, 0:1] for h in heads]
    log_d = [jnp.where(causal, b_t[h] - b_s[h] + ig_row[h], NEG) for h in heads]
    m_inter = [b_col[h] + m_prev[h] for h in heads]
    m_t = [jnp.maximum(m_inter[h], jnp.max(log_d[h], axis=1, keepdims=True)) for h in heads]
    a = [jnp.exp(log_d[h] - m_t[h]) * _dot_nt(qb[h], kb[h]) for h in heads]
    w_inter = [jnp.exp(m_inter[h] - m_t[h]) for h in heads]
    c_prev = [c_ref[h] for h in heads]
    n_prev = [n_ref[h] for h in heads]
    num = [_dot(a[h].astype(BF16), vb[h]) + w_inter[h] * _dot(qb[h], c_prev[h].astype(BF16)) for h in heads]
    den = [jnp.sum(a[h], axis=1, keepdims=True)
           + w_inter[h] * jnp.sum(qc[h] * n_prev[h], axis=1, keepdims=True) for h in heads]
    hs = [num[h] / jnp.maximum(jnp.abs(den[h]), jnp.exp(-m_t[h])) for h in heads]

    m_new = [m_t[h][L - 1:L, :] for h in heads]
    b_last = [b_col[h][L - 1:L, :] for h in heads]
    ws_col = [jnp.exp(b_last[h] - b_col[h] + ig_col[h] - m_new[h]) for h in heads]
    decay = [jnp.exp(b_last[h] + m_prev[h] - m_new[h]) for h in heads]
    for h in heads:
        c_ref[h] = decay[h] * c_prev[h] + _dot_tn(kb[h], (ws_col[h] * vb[h].astype(F32)).astype(BF16))
        n_ref[h] = decay[h] * n_prev[h] + jnp.sum(ws_col[h] * kc[h], axis=0, keepdims=True)
        m_ref[h] = m_new[h]

    for h, ls in zip(heads, lanes):
        hn = hs[h] * lax.rsqrt(jnp.mean(hs[h] * hs[h], axis=-1, keepdims=True) + EPS) * nw_ref[:, ls]
        o_ref[:, ls] = (jax.nn.sigmoid(mo_ref[:, ls].astype(F32)) * hn).astype(o_ref.dtype)


def _mlstm(mq, mk, mv, mo, g_col, g_row, gate_b, conv_wq, conv_wk, conv_bq, conv_bk, norm_w):
    b, s, w = mq.shape
    L = MLSTM_CHUNK
    d = MLSTM_HEAD_DIM
    hps = MLSTM_HEADS_PER_STEP
    assert s % L == 0 and (w // d) % hps == 0
    seq = pl.BlockSpec((None, s, hps * d), lambda bi, g, c: (bi, 0, g))
    chunk = pl.BlockSpec((None, L, hps * d), lambda bi, g, c: (bi, c, g))
    per_group = lambda rows: pl.BlockSpec((rows, hps * d), lambda bi, g, c: (0, g))
    return pl.pallas_call(
        _mlstm_kernel,
        grid=(b, w // d // hps, s // L),
        in_specs=[pl.BlockSpec(memory_space=pltpu.SMEM), seq, seq, chunk, chunk,
                  pl.BlockSpec((None, hps, L, 2), lambda bi, g, c: (bi, g, c, 0)),
                  pl.BlockSpec((None, hps, 2, L), lambda bi, g, c: (bi, g, 0, c)),
                  per_group(CONV_WIDTH), per_group(CONV_WIDTH), per_group(1), per_group(1), per_group(1)],
        out_specs=chunk,
        out_shape=jax.ShapeDtypeStruct((b, s, w), BF16),
        scratch_shapes=[pltpu.VMEM((hps, d, d), F32), pltpu.VMEM((hps, 1, d), F32),
                        pltpu.VMEM((hps, 1, 1), F32)],
        compiler_params=_params(("parallel", "parallel", "arbitrary")),
        name="mlstm",
    )(gate_b, mq, mk, mv, mo, g_col, g_row, conv_wq, conv_wk, conv_bq, conv_bk, norm_w)


def _merge_kernel(x_ref, attn_ref, hm_ref, ga_ref, gm_ref, wab_ref, wmb_ref, mb_ref, wout_ref,
                  fnw_ref, wq_ref, x2_ref, h2_ref, q_ref):
    ya = _dot(attn_ref[...], wab_ref[...])
    ym = _dot(hm_ref[...], wmb_ref[...])
    y = (jax.nn.sigmoid(ga_ref[...].astype(F32) + mb_ref[0:1, :]) * ya
         + jax.nn.sigmoid(gm_ref[...].astype(F32) + mb_ref[1:2, :]) * ym)
    x2 = x_ref[...] + _dot(y.astype(BF16), wout_ref[...])
    x2_ref[...] = x2
    h2 = x2 * lax.rsqrt(jnp.mean(x2 * x2, axis=-1, keepdims=True) + EPS) * fnw_ref[...]
    h2b = h2.astype(BF16)
    h2_ref[...] = pltpu.bitcast(h2b, jnp.int32)
    q_ref[...] = _dot(h2b, wq_ref[...])


def _merge(x2d, attn, hm, ga, gm, wab, wmb, merge_b, wout, fnw, wq):
    t, d = x2d.shape
    tm = MERGE_TM
    nq = wq.shape[1]
    row = lambda w: pl.BlockSpec((tm, w), lambda i: (i, 0))
    const = lambda a: pl.BlockSpec(a.shape, lambda i: (0, 0))
    return pl.pallas_call(
        _merge_kernel,
        grid=(t // tm,),
        in_specs=[row(d), row(ATTN_W), row(MLSTM_W), row(d), row(d), const(wab), const(wmb),
                  const(merge_b), const(wout), const(fnw), const(wq)],
        out_specs=[row(d), pl.BlockSpec((tm // 2, d), lambda i: (i, 0)), row(nq)],
        out_shape=[jax.ShapeDtypeStruct((t, d), F32), jax.ShapeDtypeStruct((t // 2, d), jnp.int32),
                   jax.ShapeDtypeStruct((t, nq), F32)],
        compiler_params=_params(("parallel",)),
        name="merge",
    )(x2d, attn, hm, ga, gm, wab, wmb, merge_b, wout, fnw, wq)


_CAND_GROUPS = (
    ((0, 0, 0, 8),), ((0, 8, 0, 8),), ((1, 0, 0, 8),),
    ((2, 0, 0, 5), (4, 0, 5, 3)),
    ((3, 0, 0, 4), (5, 0, 4, 2), (6, 0, 6, 2)),
    ((7, 0, 0, 2),) + tuple((i, 0, i - 6, 1) for i in range(8, 14)),
    ((14, 0, 0, 1), (15, 0, 1, 1)),
)


def _select_kernel(q1_ref, q2_ref, k1_ref, k2_ref, e1_ref, na_ref, e2_ref, r2_ref,
                   s1_ref, s2_ref, w1_ref, w2_ref, rk1_ref, rk2_ref, v_ref, c0_ref, c_ref, e_ref):
    tn = q1_ref.shape[0]
    nk = PEER_N_KEYS
    topk = PEER_TOPK
    n_lt = tn // LANES
    lane_tiles = [slice(lt * LANES, (lt + 1) * LANES) for lt in range(n_lt)]
    halves = ((s1_ref, w1_ref, rk1_ref), (s2_ref, w2_ref, rk2_ref))

    def scores(k_ref, q_ref):
        kh, kl = _split2(k_ref[...])
        qh, ql = _split2(q_ref[...])
        return _dot_nt(kh, qh) + _dot_nt(kh, ql) + _dot_nt(kl, qh)

    s1_ref[...] = scores(k1_ref, q1_ref)
    s2_ref[...] = scores(k2_ref, q2_ref)
    key = lax.broadcasted_iota(jnp.int32, (nk, LANES), 0).astype(F32)

    def extract(break_ties):
        for s_ref, w_ref, rk_ref in halves:
            w_ref[...] = s_ref[...]
            rk_ref[...] = jnp.full(rk_ref.shape, float(topk), F32)
        for it in range(topk):
            for half, (_, w_ref, rk_ref) in enumerate(halves):
                for cols in lane_tiles:
                    s_cur = w_ref[:, cols]
                    m = jnp.max(s_cur, axis=0, keepdims=True)
                    if break_ties:
                        hit = key == jnp.min(jnp.where(s_cur == m, key, float(nk)), axis=0, keepdims=True)
                    else:
                        hit = s_cur == m
                    w_ref[:, cols] = jnp.where(hit, -jnp.inf, s_cur)
                    rk_ref[:, cols] = jnp.where(hit, float(it), rk_ref[:, cols])
                    v_ref[half, it, :, cols] = m

    extract(break_ties=False)
    off_count = jnp.zeros((1, LANES), F32)
    for _, _, rk_ref in halves:
        for cols in lane_tiles:
            taken = jnp.sum(jnp.where(rk_ref[:, cols] < float(topk), 1.0, 0.0), axis=0, keepdims=True)
            off_count = off_count + jnp.where(taken == float(topk), 0.0, 1.0)

    @pl.when(jnp.max(off_count) > 0.0)
    def _():
        extract(break_ties=True)

    row8 = lax.broadcasted_iota(jnp.int32, (SUBLANES, LANES), 0)
    seg_mask = {seg: (row8 >= seg[2]) & (row8 < seg[2] + seg[3]) for grp in _CAND_GROUPS for seg in grp}
    flats = []
    for grp in _CAND_GROUPS:
        f = jnp.full((SUBLANES, LANES), 1e9, F32)
        for seg in grp:
            i, j0, r0, _ = seg
            f = jnp.where(seg_mask[seg], (row8 + (i * topk + j0 - r0)).astype(F32), f)
        flats.append(f)
    flat = jnp.concatenate(flats, axis=0)
    for cols in lane_tiles:
        v1 = [v_ref[0, i, :, cols] for i in range(topk)]
        v2 = [v_ref[1, j, :, cols] for j in range(topk)]
        v2_groups = []
        for j0 in (0, SUBLANES):
            dense = jnp.zeros((SUBLANES, LANES), F32)
            for r in range(SUBLANES):
                dense = jnp.where(row8 == r, v2[j0 + r], dense)
            v2_groups.append(dense)
        top = v1[0] + v2[0]
        for g, grp in enumerate(_CAND_GROUPS):
            cand = jnp.full((SUBLANES, LANES), -jnp.inf, F32)
            for seg in grp:
                i, j0, r0, n = seg
                if n == 1 and j0 == 0:
                    piece = v1[i] + v2[0]
                else:
                    src = v2_groups[j0 // SUBLANES]
                    piece = v1[i] + (pltpu.roll(src, r0, axis=0) if r0 else src)
                cand = jnp.where(seg_mask[seg], piece, cand)
            rows = slice(g * SUBLANES, (g + 1) * SUBLANES)
            c0_ref[rows, cols] = cand
            e_ref[rows, cols] = jnp.exp(cand - top)

    def pick(break_ties):
        c_ref[...] = c0_ref[...]
        for _ in range(topk):
            for cols in lane_tiles:
                c_cur = c_ref[:, cols]
                m = jnp.max(c_cur, axis=0, keepdims=True)
                if break_ties:
                    hit = flat == jnp.min(jnp.where(c_cur == m, flat, 1e9), axis=0, keepdims=True)
                else:
                    hit = c_cur == m
                c_ref[:, cols] = jnp.where(hit, -jnp.inf, c_cur)

    pick(break_ties=False)
    off_count = jnp.zeros((1, LANES), F32)
    for cols in lane_tiles:
        taken = jnp.sum(jnp.where(flat < 1e8, jnp.where(c_ref[:, cols] == -jnp.inf, 1.0, 0.0), 0.0),
                        axis=0, keepdims=True)
        off_count = off_count + jnp.where(taken == float(topk), 0.0, 1.0)

    @pl.when(jnp.max(off_count) > 0.0)
    def _():
        pick(break_ties=True)

    for lt, cols in enumerate(lane_tiles):
        chosen = jnp.where(flat < 1e8, jnp.where(c_ref[:, cols] == -jnp.inf, 1.0, 0.0), 0.0)
        z = jnp.sum(chosen * e_ref[:, cols], axis=0, keepdims=True)
        counts = [jnp.zeros((1, LANES), F32) for _ in range(topk)]
        for g, grp in enumerate(_CAND_GROUPS):
            ch = chosen[g * SUBLANES:(g + 1) * SUBLANES, :]
            for seg in grp:
                counts[seg[0]] = counts[seg[0]] + jnp.sum(
                    jnp.where(seg_mask[seg], ch, 0.0), axis=0, keepdims=True)
        v1_top = v_ref[0, 0, :, cols]
        v2_top = v_ref[1, 0, :, cols]

        r1 = rk1_ref[:, cols]
        r2 = rk2_ref[:, cols]
        n_a = jnp.zeros((nk, LANES), F32)
        for i in range(topk):
            n_a = jnp.where(r1 == float(i), counts[i], n_a)
        in1 = r1 < float(topk)
        in2 = r2 < float(topk)
        e1 = jnp.where(in1, jnp.exp(s1_ref[:, cols] - v1_top) * (0.5 / z), 0.0)
        e2 = jnp.where(in2, jnp.exp(s2_ref[:, cols] - v2_top), 0.0)
        e1_ref[:, cols] = _pack_pairs(e1, e1)
        na_ref[:, cols] = _pack_pairs(n_a, n_a)
        e2_ref[:, cols] = _pack_pairs(e2[:nk // 2], e2[nk // 2:])
        r2_ref[:, cols] = _pack_pairs(r2[:nk // 2], r2[nk // 2:])


def _select(q, keys_1, keys_2):
    t = q.shape[0]
    tn = SELECT_TN
    assert t % tn == 0
    nk = PEER_N_KEYS
    q1 = pl.BlockSpec((tn, PEER_HALF), lambda i, h: (i, 2 * h))
    q2 = pl.BlockSpec((tn, PEER_HALF), lambda i, h: (i, 2 * h + 1))
    kspec = pl.BlockSpec((None, nk, PEER_HALF), lambda i, h: (h, 0, 0))
    ospec = lambda rows: pl.BlockSpec((None, rows, tn), lambda i, h: (h, 0, i))
    oshape = lambda rows: jax.ShapeDtypeStruct((PEER_HEADS, rows, t), jnp.int32)
    keys_buf = pltpu.VMEM((nk, tn), F32)
    return pl.pallas_call(
        _select_kernel,
        grid=(t // tn, PEER_HEADS),
        in_specs=[q1, q2, kspec, kspec],
        out_specs=[ospec(nk), ospec(nk), ospec(nk // 2), ospec(nk // 2)],
        out_shape=[oshape(nk), oshape(nk), oshape(nk // 2), oshape(nk // 2)],
        scratch_shapes=[keys_buf] * 6 + [pltpu.VMEM((2, PEER_TOPK, 1, tn), F32)]
        + [pltpu.VMEM((SUBLANES * len(_CAND_GROUPS), tn), F32)] * 3,
        compiler_params=_params(("parallel", "parallel")),
        name="peer_select",
    )(q, q, keys_1, keys_2)


def _experts_kernel(h2_ref, wd_ref, wut_ref, e1_in, na_in, e2_in, r2_in, x2_ref, o_ref,
                    acc_ref, z0_ref, z1_ref, act0_ref, act1_ref, e1_ref, na_ref, e2_ref, r2_ref):
    s = pl.program_id(1)
    n_chunks = pl.num_programs(1) - 2
    nk = PEER_N_KEYS
    tn = z0_ref.shape[1]
    rows_per_chunk = EXPERT_CHUNK // nk

    @pl.when(s == 0)
    def _():
        acc_ref[...] = jnp.zeros_like(acc_ref)
        e2_ref[...] = e2_in[...]
        r2_ref[...] = r2_in[...]

    chunk = jnp.clip(s - 1, 0, n_chunks - 1)
    r0 = (chunk % (SUBLANES // rows_per_chunk)) * rows_per_chunk
    for h in range(PEER_HEADS):
        for aa in range(rows_per_chunk):
            e1_ref[h, aa:aa + 1, :] = e1_in[h, pl.ds(r0 + aa, 1), :]
            na_ref[h, aa:aa + 1, :] = na_in[h, pl.ds(r0 + aa, 1), :]

    half = nk // 2
    groups = half // SUBLANES

    def gate_piece(lt, k_lo, k_hi, z_r, act_w):
        cols = slice(lt * LANES, (lt + 1) * LANES)
        g = [[jnp.zeros((2 * SUBLANES, LANES), BF16) for _ in range(k_lo, k_hi)]
             for _ in range(rows_per_chunk)]
        for h in range(PEER_HEADS):
            as_pair = lambda ref, aa: _as_bf16_pairs(
                jnp.broadcast_to(ref[h, aa:aa + 1, cols], (SUBLANES, LANES)))
            e1 = [as_pair(e1_ref, aa) for aa in range(rows_per_chunk)]
            n_a = [as_pair(na_ref, aa) for aa in range(rows_per_chunk)]
            for k in range(k_lo, k_hi):
                grp = slice(k * SUBLANES, (k + 1) * SUBLANES)
                r2 = _as_bf16_pairs(r2_ref[h, grp, cols])
                e2 = _as_bf16_pairs(e2_ref[h, grp, cols])
                for aa in range(rows_per_chunk):
                    g[aa][k - k_lo] = (g[aa][k - k_lo]
                                       + jnp.where(r2 < n_a[aa], e2, jnp.zeros_like(e2)) * e1[aa])
        for aa in range(rows_per_chunk):
            for k in range(k_lo, k_hi):
                g_lo, g_hi = _unpack_pairs(g[aa][k - k_lo])
                for g_half, base in ((g_lo, aa * nk), (g_hi, aa * nk + half)):
                    rows = slice(base + k * SUBLANES, base + (k + 1) * SUBLANES)
                    z = z_r[rows, cols]
                    act = z * (1.0 + lax.erf(z * (2.0 ** -0.5)))
                    act_w[rows, cols] = (act * g_half).astype(BF16)

    def step(parity, do_z, do_gate, do_up):
        z_w, z_r = (z0_ref, z1_ref) if parity == 0 else (z1_ref, z0_ref)
        act_w, act_r = (act1_ref, act0_ref) if parity == 0 else (act0_ref, act1_ref)
        d = acc_ref.shape[0]
        mxu = []
        for p in range(MXU_PIECES):
            pc = slice(p * tn // MXU_PIECES, (p + 1) * tn // MXU_PIECES)
            pw = slice(pc.start // 2, pc.stop // 2)

            def z_piece(pc=pc, pw=pw):
                z_w[:, pc] = _dot_nt(_rows_bf16(wd_ref[...]), _rows_bf16(h2_ref[pw, :]))

            if do_z:
                mxu.append(z_piece)
            for r in range(UP_ROW_PIECES if do_up else 0):
                rs = slice(r * d // UP_ROW_PIECES, (r + 1) * d // UP_ROW_PIECES)
                rw = slice(rs.start // 2, rs.stop // 2)

                def up_piece(pc=pc, rs=rs, rw=rw):
                    acc_ref[rs, pc] += _dot(_rows_bf16(wut_ref[rw, :]), act_r[:, pc])

                mxu.append(up_piece)
        gates = [(lt, k, k + GATE_GROUPS) for lt in range(tn // LANES)
                 for k in range(0, groups, GATE_GROUPS)] if do_gate else []
        per_mxu = -(-len(gates) // len(mxu))
        for m, piece in enumerate(mxu):
            piece()
            for lt, k_lo, k_hi in gates[m * per_mxu:(m + 1) * per_mxu]:
                gate_piece(lt, k_lo, k_hi, z_r, act_w)
        for lt, k_lo, k_hi in gates[len(mxu) * per_mxu:]:
            gate_piece(lt, k_lo, k_hi, z_r, act_w)

    last = n_chunks + 1
    variants = ((s == 0, 0, (True, False, False)),
                (s == 1, 1, (True, True, False)),
                ((s >= 2) & (s < n_chunks) & (s % 2 == 0), 0, (True, True, True)),
                ((s >= 2) & (s < n_chunks) & (s % 2 == 1), 1, (True, True, True)),
                (s == n_chunks, n_chunks % 2, (False, True, True)),
                (s == last, last % 2, (False, False, True)))
    for cond, parity, stages in variants:
        pl.when(cond)(functools.partial(step, parity, *stages))

    @pl.when(s == pl.num_programs(1) - 1)
    def _():
        o_ref[...] = x2_ref[...] + acc_ref[...].T


def _experts(h2, wd, wut, e1, na, e2, r2, x2):
    t, d = x2.shape
    tn = EXPERT_TN
    ne = 2 * wd.shape[0]
    nk = PEER_N_KEYS
    assert t % tn == 0 and ne % EXPERT_CHUNK == 0 and EXPERT_CHUNK % nk == 0
    n_chunks = ne // EXPERT_CHUNK
    assert n_chunks % 2 == 0
    tok = pl.BlockSpec((tn, d), lambda i, s: (i, 0))
    gate = pl.BlockSpec((PEER_HEADS, nk // 2, tn), lambda i, s: (0, 0, i))
    rpc = EXPERT_CHUNK // nk
    assert SUBLANES % rpc == 0
    chunks_per_group = SUBLANES // rpc
    row = pl.BlockSpec((PEER_HEADS, None, SUBLANES, tn),
                       lambda i, s: (0, jnp.clip(s - 1, 0, n_chunks - 1) // chunks_per_group, 0, i))
    e1 = e1.reshape(PEER_HEADS, nk // SUBLANES, SUBLANES, t)
    na = na.reshape(PEER_HEADS, nk // SUBLANES, SUBLANES, t)
    return pl.pallas_call(
        _experts_kernel,
        grid=(t // tn, n_chunks + 2),
        in_specs=[pl.BlockSpec((tn // 2, d), lambda i, s: (i, 0)),
                  pl.BlockSpec((EXPERT_CHUNK // 2, d), lambda i, s: (jnp.minimum(s, n_chunks - 1), 0)),
                  pl.BlockSpec((None, d // 2, EXPERT_CHUNK),
                               lambda i, s: (jnp.clip(s - 2, 0, n_chunks - 1), 0, 0)),
                  row, row, gate, gate, tok],
        out_specs=tok,
        out_shape=jax.ShapeDtypeStruct((t, d), F32),
        scratch_shapes=[pltpu.VMEM((d, tn), F32),
                        pltpu.VMEM((EXPERT_CHUNK, tn), F32), pltpu.VMEM((EXPERT_CHUNK, tn), F32),
                        pltpu.VMEM((EXPERT_CHUNK, tn), BF16), pltpu.VMEM((EXPERT_CHUNK, tn), BF16),
                        pltpu.VMEM((PEER_HEADS, rpc, tn), jnp.int32), pltpu.VMEM((PEER_HEADS, rpc, tn), jnp.int32),
                        pltpu.VMEM((PEER_HEADS, nk // 2, tn), jnp.int32),
                        pltpu.VMEM((PEER_HEADS, nk // 2, tn), jnp.int32)],
        compiler_params=_params(("parallel", "arbitrary")),
        name="peer_experts",
    )(h2, wd, wut, e1, na, e2, r2, x2)


def _rope_tables(seq):
    pos = jnp.arange(seq, dtype=F32)
    inv_freq = ROPE_THETA ** (-jnp.arange(0, ROT_DIM, 2, dtype=F32) / ROT_DIM)
    ang = pos[:, None] * inv_freq[None, :]
    cos, sin = jnp.cos(ang), jnp.sin(ang)
    rest = ATTN_HEAD_DIM - ROT_DIM
    cos_h = jnp.concatenate([cos, cos, jnp.ones((seq, rest), F32)], axis=-1)
    sin_h = jnp.concatenate([-sin, sin, jnp.zeros((seq, rest), F32)], axis=-1)
    return jnp.tile(cos_h, (1, ATTN_HEADS)), jnp.tile(sin_h, (1, ATTN_HEADS))


def kernel(x, mix_norm_w, w_in, q_norm_w, k_norm_w, conv_w, conv_b, igate_b, fgate_b, mlstm_norm_w,
           w_attn_branch, w_mlstm_branch, merge_b, w_out, ffn_norm_w, peer_w_query, peer_keys_1,
           peer_keys_2, peer_w_down, peer_w_up):
    b, s, d = x.shape
    t = b * s
    x2d = x.reshape(t, d)

    n_main = 3 * ATTN_W + 4 * MLSTM_W
    n_gate = 2 * MLSTM_HEADS
    w_cat = jnp.concatenate(
        [w_in[:, :n_main], w_in[:, n_main + n_gate:], w_in[:, n_main:n_main + n_gate],
         jnp.zeros((d, LANES - n_gate), w_in.dtype)], axis=1).astype(BF16)
    cos_t, sin_t = _rope_tables(s)
    head_of = jnp.arange(ATTN_W) // ATTN_HEAD_DIM
    bd = (head_of[:, None] == head_of[None, :]).astype(BF16)
    qnw = jnp.tile(q_norm_w, ATTN_HEADS).reshape(1, ATTN_W)
    knw = jnp.tile(k_norm_w, ATTN_HEADS).reshape(1, ATTN_W)

    aq, ak, av, mq, mk, mv, mo, ga, gm, gt = _inproj(
        x2d, mix_norm_w.reshape(1, d), w_cat, cos_t, sin_t, bd, qnw, knw, s)

    attn = _moba(aq.reshape(b, s, ATTN_W), ak.reshape(b, s, ATTN_W), av.reshape(b, s, ATTN_W))

    gates = gt[:, :n_gate].reshape(b, s, 2, MLSTM_HEADS)
    g_col = gates.transpose(0, 3, 1, 2)
    g_row = gates.transpose(0, 3, 2, 1)
    gate_b = jnp.stack([igate_b, fgate_b]).astype(F32)
    r3 = lambda a: a.reshape(b, s, MLSTM_W)
    hm = _mlstm(r3(mq), r3(mk), r3(mv), r3(mo), g_col, g_row, gate_b,
                conv_w[:, :MLSTM_W], conv_w[:, MLSTM_W:], conv_b[:MLSTM_W].reshape(1, MLSTM_W),
                conv_b[MLSTM_W:].reshape(1, MLSTM_W), mlstm_norm_w.reshape(1, MLSTM_W))

    x2, h2, q = _merge(x2d, attn.reshape(t, ATTN_W), hm.reshape(t, MLSTM_W), ga, gm,
                       w_attn_branch.astype(BF16), w_mlstm_branch.astype(BF16), merge_b,
                       w_out.astype(BF16), ffn_norm_w.reshape(1, d), peer_w_query.astype(BF16))

    e1, na, e2, r2 = _select(q, peer_keys_1, peer_keys_2)
    wut = peer_w_up.astype(BF16).reshape(-1, EXPERT_CHUNK, d).transpose(0, 2, 1)
    out = _experts(h2, _pack_rows(peer_w_down.astype(BF16)), _pack_rows(wut), e1, na, e2, r2, x2)
    return out.reshape(b, s, d)
```

```python
import functools

import jax
import jax.numpy as jnp
from jax import lax
from jax.experimental import pallas as pl
from jax.experimental.pallas import tpu as pltpu

F32 = jnp.float32
BF16 = jnp.bfloat16

EPS = 1e-6
ATTN_HEADS = 8
ATTN_HEAD_DIM = 64
ATTN_W = ATTN_HEADS * ATTN_HEAD_DIM
ROT_DIM = ATTN_HEAD_DIM // 4
ROPE_THETA = 500000.0
MOBA_BLOCK = 256
MOBA_TOPK = 3
MLSTM_HEADS = 4
MLSTM_HEAD_DIM = 128
MLSTM_W = MLSTM_HEADS * MLSTM_HEAD_DIM
CONV_WIDTH = 4
PEER_HEADS = 8
PEER_N_KEYS = 128
PEER_HALF = 128
PEER_TOPK = 16

LANES = 128
SUBLANES = 8
NEG = -1e30
VMEM_LIMIT = 56 * 1024 * 1024

INPROJ_TM = 512
MERGE_TM = 512
MOBA_MAX_BLOCKS = 16
MOBA_UNROLL = 4
MLSTM_CHUNK = 256
MLSTM_HEADS_PER_STEP = 4
SELECT_TN = 512
EXPERT_TN = 512
EXPERT_CHUNK = 1024
MXU_PIECES = 2
UP_ROW_PIECES = 4
GATE_GROUPS = 4


def _dot(a, b):
    return jnp.dot(a, b, preferred_element_type=F32)


def _dot_nt(a, b):
    return lax.dot_general(a, b, (((1,), (1,)), ((), ())), preferred_element_type=F32)


def _dot_tn(a, b):
    return lax.dot_general(a, b, (((0,), (0,)), ((), ())), preferred_element_type=F32)


def _bf16_bits(x):
    return lax.bitcast_convert_type(x.astype(BF16).astype(F32), jnp.int32)


def _pack_pairs(lo, hi):
    return lax.shift_right_logical(_bf16_bits(lo), jnp.int32(16)) | _bf16_bits(hi)


def _as_bf16_pairs(words):
    return pltpu.bitcast(words, BF16)


def _unpack_pairs(pairs):
    words = pltpu.bitcast(pairs, jnp.int32)
    lo = lax.bitcast_convert_type(lax.shift_left(words, jnp.int32(16)), F32)
    hi = lax.bitcast_convert_type(words & jnp.int32(-65536), F32)
    return lo, hi


def _pack_rows(x):
    *lead, m, n = x.shape
    pairs = jnp.swapaxes(x.reshape(*lead, m // 2, 2, n), -1, -2)
    return lax.bitcast_convert_type(pairs, jnp.int32)


def _rows_bf16(words):
    return pltpu.bitcast(words, BF16)


def _split2(x):
    hi = x.astype(BF16)
    lo = (x - hi.astype(F32)).astype(BF16)
    return hi, lo


def _params(sem, flags=None):
    return pltpu.CompilerParams(dimension_semantics=sem, vmem_limit_bytes=VMEM_LIMIT, flags=flags)


def _inproj_kernel(x_ref, nw_ref, w_ref, cos_ref, sin_ref, bd_ref, qnw_ref, knw_ref,
                   aq_ref, ak_ref, av_ref, mq_ref, mk_ref, mv_ref, mo_ref, ga_ref, gm_ref, gt_ref):
    x = x_ref[...]
    h = x * lax.rsqrt(jnp.mean(x * x, axis=-1, keepdims=True) + EPS) * nw_ref[...]
    hb = h.astype(BF16)

    def proj(lo, width):
        return _dot(hb, w_ref[:, lo:lo + width])

    lane = lax.broadcasted_iota(jnp.int32, (x.shape[0], ATTN_W), 1) % ATTN_HEAD_DIM
    half = ROT_DIM // 2

    def qk_norm_rope(t, nw):
        hi, lo = _split2(t * t)
        ms = (_dot(hi, bd_ref[...]) + _dot(lo, bd_ref[...])) * (1.0 / ATTN_HEAD_DIM)
        tn = t * lax.rsqrt(ms + EPS) * nw
        fwd = pltpu.roll(tn, ATTN_W - half, axis=1)
        bwd = pltpu.roll(tn, half, axis=1)
        swapped = jnp.where(lane < half, fwd, jnp.where(lane < ROT_DIM, bwd, 0.0))
        return tn * cos_ref[...] + swapped * sin_ref[...]

    aq_ref[...] = qk_norm_rope(proj(0, ATTN_W), qnw_ref[...]).astype(BF16)
    ak_ref[...] = qk_norm_rope(proj(ATTN_W, ATTN_W), knw_ref[...]).astype(BF16)
    av_ref[...] = proj(2 * ATTN_W, ATTN_W).astype(BF16)
    base = 3 * ATTN_W
    mq_ref[...] = proj(base, MLSTM_W).astype(BF16)
    mk_ref[...] = proj(base + MLSTM_W, MLSTM_W).astype(BF16)
    mv_ref[...] = proj(base + 2 * MLSTM_W, MLSTM_W).astype(BF16)
    mo_ref[...] = proj(base + 3 * MLSTM_W, MLSTM_W).astype(BF16)
    base = base + 4 * MLSTM_W
    d = x.shape[1]
    ga_ref[...] = proj(base, d).astype(BF16)
    gm_ref[...] = proj(base + d, d).astype(BF16)
    gt_ref[...] = proj(base + 2 * d, LANES)


def _inproj(x2d, mix_norm_w, w_cat, cos_t, sin_t, bd, qnw, knw, seq):
    t, d = x2d.shape
    tm = INPROJ_TM
    assert t % tm == 0 and seq % tm == 0
    tiles_per_seq = seq // tm
    ncols = w_cat.shape[1]
    row = lambda w: pl.BlockSpec((tm, w), lambda i: (i, 0))
    const = lambda shape: pl.BlockSpec(shape, lambda i: (0, 0))
    pos = pl.BlockSpec((tm, ATTN_W), lambda i: (i % tiles_per_seq, 0))
    outs = [ATTN_W] * 3 + [MLSTM_W] * 4 + [d, d]
    return pl.pallas_call(
        _inproj_kernel,
        grid=(t // tm,),
        in_specs=[row(d), const((1, d)), const((d, ncols)), pos, pos, const((ATTN_W, ATTN_W)),
                  const((1, ATTN_W)), const((1, ATTN_W))],
        out_specs=[row(w) for w in outs] + [row(LANES)],
        out_shape=[jax.ShapeDtypeStruct((t, w), BF16) for w in outs]
        + [jax.ShapeDtypeStruct((t, LANES), F32)],
        compiler_params=_params(("parallel",)),
        name="inproj",
    )(x2d, mix_norm_w, w_cat, cos_t, sin_t, bd, qnw, knw)


def _moba_kernel(q_ref, k_ref, v_ref, o_ref, kmean_ref, kext_ref, vext_ref, s_ref, mx_ref, *, n_blocks):
    i = pl.program_id(2)
    blk = MOBA_BLOCK
    hd = ATTN_HEAD_DIM
    heads = (0, 1)
    lane = lax.broadcasted_iota(jnp.int32, (blk, LANES), 1)
    in_head = [lane < hd, lane >= hd]
    spare0 = [hd, 0]

    @pl.when(i == 0)
    def _():
        kmean_ref[...] = jnp.zeros_like(kmean_ref)
        for n in range(n_blocks):
            rows = slice(n * blk, (n + 1) * blk)
            kb = k_ref[rows, :]
            vb = v_ref[rows, :]
            kmean_ref[n:n + 1, :] = jnp.mean(kb.astype(F32), axis=0, keepdims=True)
            for h in heads:
                onehot = jnp.where(lane == spare0[h] + n, 1.0, 0.0).astype(BF16)
                kext_ref[h, rows, :] = jnp.where(in_head[h], kb, onehot)
                vext_ref[h, rows, :] = jnp.where(in_head[h], vb, jnp.ones_like(vb))

    q = q_ref[...]
    kmean_hi, kmean_lo = _split2(kmean_ref[...])
    nb = MOBA_MAX_BLOCKS
    brow = lax.broadcasted_iota(jnp.int32, (nb, blk), 0)
    qext = []
    for h in heads:
        qh = jnp.where(in_head[h], q, jnp.zeros_like(q)) * jnp.asarray(hd ** -0.5, BF16)
        g_t = (_dot_nt(kmean_hi, qh) + _dot_nt(kmean_lo, qh))[0:nb, :]
        rank = jnp.zeros((nb, blk), F32)
        for m in range(n_blocks):
            gm = g_t[m:m + 1, :]
            beats = jnp.where(gm > g_t, 1.0, jnp.where(gm == g_t, jnp.where(brow > m, 1.0, 0.0), 0.0))
            rank = rank + jnp.where(m < i, beats, 0.0)
        bias_t = jnp.where(brow < i, jnp.where(rank < MOBA_TOPK, 0.0, NEG), 0.0)
        bias = jnp.concatenate([bias_t, jnp.zeros((LANES - nb, blk), F32)], axis=0).T
        if spare0[h]:
            bias = pltpu.roll(bias, spare0[h], axis=1)
        qext.append(qh + bias.astype(BF16))

    own = pl.multiple_of(i * blk, blk)
    causal = (lax.broadcasted_iota(jnp.int32, (blk, blk), 1)
              <= lax.broadcasted_iota(jnp.int32, (blk, blk), 0))
    unroll = MOBA_UNROLL

    def fold(s):
        return jnp.maximum(s[:, :LANES], s[:, LANES:])

    for h in heads:
        mx_ref[h] = jnp.full((blk, LANES), NEG, F32)

    def scores(t, carry):
        for u in range(unroll):
            n = unroll * t + u
            start = pl.multiple_of(n * blk, blk)
            for h in heads:
                s = _dot_nt(qext[h], kext_ref[h, pl.ds(start, blk), :])
                s_ref[h, n] = s
                m_old = mx_ref[h]
                m_cand = jnp.maximum(m_old, fold(s))
                mx_ref[h] = m_cand if u == 0 else jnp.where(n < i, m_cand, m_old)
        return carry

    lax.fori_loop(0, (i + unroll - 1) // unroll, scores, 0)
    m_fin = []
    for h in heads:
        s = jnp.where(causal, _dot_nt(qext[h], kext_ref[h, pl.ds(own, blk), :]), NEG)
        s_ref[h, i] = s
        m_fin.append(jnp.max(jnp.maximum(mx_ref[h], fold(s)), axis=1, keepdims=True))

    def values(t, acc):
        out = list(acc)
        for u in range(unroll):
            n = unroll * t + u
            n_read = jnp.minimum(n, i)
            start = pl.multiple_of(n_read * blk, blk)
            for h in heads:
                m_eff = m_fin[h] if u == 0 else jnp.where(n <= i, m_fin[h], -NEG)
                p = jnp.exp(s_ref[h, n_read] - m_eff).astype(BF16)
                out[h] = out[h] + _dot(p, vext_ref[h, pl.ds(start, blk), :])
        return tuple(out)

    acc = lax.fori_loop(0, (i + unroll) // unroll, values,
                        tuple(jnp.zeros((blk, LANES), F32) for _ in heads))
    outs = [a / pltpu.roll(a, hd, axis=1) for a in acc]
    o_ref[...] = jnp.where(in_head[0], outs[0], outs[1]).astype(o_ref.dtype)


def _moba(q, k, v):
    b, s, w = q.shape
    assert s % MOBA_BLOCK == 0 and w % LANES == 0
    n_blocks = s // MOBA_BLOCK
    assert n_blocks <= MOBA_MAX_BLOCKS and n_blocks % MOBA_UNROLL == 0
    qspec = pl.BlockSpec((None, MOBA_BLOCK, LANES), lambda bi, hp, i: (bi, i, hp))
    kvspec = pl.BlockSpec((None, s, LANES), lambda bi, hp, i: (bi, 0, hp))
    return pl.pallas_call(
        functools.partial(_moba_kernel, n_blocks=n_blocks),
        grid=(b, w // LANES, n_blocks),
        in_specs=[qspec, kvspec, kvspec],
        out_specs=qspec,
        out_shape=jax.ShapeDtypeStruct((b, s, w), BF16),
        scratch_shapes=[pltpu.VMEM((LANES, LANES), F32),
                        pltpu.VMEM((2, s, LANES), BF16), pltpu.VMEM((2, s, LANES), BF16),
                        pltpu.VMEM((2, n_blocks, MOBA_BLOCK, MOBA_BLOCK), F32),
                        pltpu.VMEM((2, MOBA_BLOCK, LANES), F32)],
        compiler_params=_params(("parallel", "parallel", "arbitrary")),
        name="moba",
    )(q, k, v)


def _log_sigmoid(f):
    return jnp.minimum(f, 0.0) - jnp.log(1.0 + jnp.exp(-jnp.abs(f)))


def _mlstm_kernel(gb_ref, mq_ref, mk_ref, mv_ref, mo_ref, gc_ref, gr_ref, cwq_ref, cwk_ref,
                  cbq_ref, cbk_ref, nw_ref, o_ref, c_ref, n_ref, m_ref):
    group = pl.program_id(1)
    c = pl.program_id(2)
    L = MLSTM_CHUNK
    d = MLSTM_HEAD_DIM
    heads = range(MLSTM_HEADS_PER_STEP)
    lanes = [slice(h * d, (h + 1) * d) for h in heads]

    @pl.when(c == 0)
    def _():
        c_ref[...] = jnp.zeros_like(c_ref)
        n_ref[...] = jnp.zeros_like(n_ref)
        m_ref[...] = jnp.zeros_like(m_ref)

    start = pl.multiple_of(c * L, L)
    prev = pl.multiple_of(jnp.maximum(c - 1, 0) * L, L)
    row_d = lax.broadcasted_iota(jnp.int32, (L, d), 0)

    def conv_silu(u_ref, w_ref, b_ref, ls):
        u = u_ref[pl.ds(start, L), ls].astype(F32)
        pv = u_ref[pl.ds(prev, L), ls].astype(F32)
        pv = jnp.where(c > 0, pv, 0.0)
        y = u * w_ref[CONV_WIDTH - 1:CONV_WIDTH, ls] + b_ref[:, ls]
        for k in range(1, CONV_WIDTH):
            shifted = jnp.where(row_d < k, pltpu.roll(pv, k, axis=0), pltpu.roll(u, k, axis=0))
            y = y + shifted * w_ref[CONV_WIDTH - 1 - k:CONV_WIDTH - k, ls]
        return y * jax.nn.sigmoid(y)

    qc = [conv_silu(mq_ref, cwq_ref, cbq_ref, ls) for ls in lanes]
    kc = [conv_silu(mk_ref, cwk_ref, cbk_ref, ls) * (d ** -0.5) for ls in lanes]
    qb = [x.astype(BF16) for x in qc]
    kb = [x.astype(BF16) for x in kc]
    vb = [mv_ref[:, ls] for ls in lanes]

    ib = [gb_ref[0, group * MLSTM_HEADS_PER_STEP + h] for h in heads]
    fb = [gb_ref[1, group * MLSTM_HEADS_PER_STEP + h] for h in heads]
    gcol = [gc_ref[h] for h in heads]
    grow = [gr_ref[h] for h in heads]
    ig_col = [gcol[h][:, 0:1] + ib[h] for h in heads]
    ig_row = [grow[h][0:1, :] + ib[h] for h in heads]
    lf_col = [_log_sigmoid(gcol[h][:, 1:2] + fb[h]) for h in heads]
    lf_row = [_log_sigmoid(grow[h][1:2, :] + fb[h]) for h in heads]

    r = lax.broadcasted_iota(jnp.int32, (L, L), 0)
    s = lax.broadcasted_iota(jnp.int32, (L, L), 1)
    causal = s <= r
    tri_l = jnp.where(causal, 1.0, 0.0).astype(BF16)
    tri_u = jnp.where(r <= s, 1.0, 0.0).astype(BF16)
    lfc = [_split2(jnp.broadcast_to(lf_col[h], (L, L))) for h in heads]
    lfr = [_split2(jnp.broadcast_to(lf_row[h], (L, L))) for h in heads]
    b_t = [_dot(tri_l, lfc[h][0]) + _dot(tri_l, lfc[h][1]) for h in heads]
    b_s = [_dot(lfr[h][0], tri_u) + _dot(lfr[h][1], tri_u) for h in heads]

    m_prev = [m_ref[h] for h in heads]
    b_col = [b_t[h][:, 0:1] for h in heads]
    log_d = [jnp.where(causal, b_t[h] - b_s[h] + ig_row[h], NEG) for h in heads]
    m_inter = [b_col[h] + m_prev[h] for h in heads]
    m_t = [jnp.maximum(m_inter[h], jnp.max(log_d[h], axis=1, keepdims=True)) for h in heads]
    a = [jnp.exp(log_d[h] - m_t[h]) * _dot_nt(qb[h], kb[h]) for h in heads]
    w_inter = [jnp.exp(m_inter[h] - m_t[h]) for h in heads]
    c_prev = [c_ref[h] for h in heads]
    n_prev = [n_ref[h] for h in heads]
    num = [_dot(a[h].astype(BF16), vb[h]) + w_inter[h] * _dot(qb[h], c_prev[h].astype(BF16)) for h in heads]
    den = [jnp.sum(a[h], axis=1, keepdims=True)
           + w_inter[h] * jnp.sum(qc[h] * n_prev[h], axis=1, keepdims=True) for h in heads]
    hs = [num[h] / jnp.maximum(jnp.abs(den[h]), jnp.exp(-m_t[h])) for h in heads]

    m_new = [m_t[h][L - 1:L, :] for h in heads]
    b_last = [b_col[h][L - 1:L, :] for h in heads]
    ws_col = [jnp.exp(b_last[h] - b_col[h] + ig_col[h] - m_new[h]) for h in heads]
    decay = [jnp.exp(b_last[h] + m_prev[h] - m_new[h]) for h in heads]
    for h in heads:
        c_ref[h] = decay[h] * c_prev[h] + _dot_tn(kb[h], (ws_col[h] * vb[h].astype(F32)).astype(BF16))
        n_ref[h] = decay[h] * n_prev[h] + jnp.sum(ws_col[h] * kc[h], axis=0, keepdims=True)
        m_ref[h] = m_new[h]

    for h, ls in zip(heads, lanes):
        hn = hs[h] * lax.rsqrt(jnp.mean(hs[h] * hs[h], axis=-1, keepdims=True) + EPS) * nw_ref[:, ls]
        o_ref[:, ls] = (jax.nn.sigmoid(mo_ref[:, ls].astype(F32)) * hn).astype(o_ref.dtype)


def _mlstm(mq, mk, mv, mo, g_col, g_row, gate_b, conv_wq, conv_wk, conv_bq, conv_bk, norm_w):
    b, s, w = mq.shape
    L = MLSTM_CHUNK
    d = MLSTM_HEAD_DIM
    hps = MLSTM_HEADS_PER_STEP
    assert s % L == 0 and (w // d) % hps == 0
    seq = pl.BlockSpec((None, s, hps * d), lambda bi, g, c: (bi, 0, g))
    chunk = pl.BlockSpec((None, L, hps * d), lambda bi, g, c: (bi, c, g))
    per_group = lambda rows: pl.BlockSpec((rows, hps * d), lambda bi, g, c: (0, g))
    return pl.pallas_call(
        _mlstm_kernel,
        grid=(b, w // d // hps, s // L),
        in_specs=[pl.BlockSpec(memory_space=pltpu.SMEM), seq, seq, chunk, chunk,
                  pl.BlockSpec((None, hps, L, 2), lambda bi, g, c: (bi, g, c, 0)),
                  pl.BlockSpec((None, hps, 2, L), lambda bi, g, c: (bi, g, 0, c)),
                  per_group(CONV_WIDTH), per_group(CONV_WIDTH), per_group(1), per_group(1), per_group(1)],
        out_specs=chunk,
        out_shape=jax.ShapeDtypeStruct((b, s, w), BF16),
        scratch_shapes=[pltpu.VMEM((hps, d, d), F32), pltpu.VMEM((hps, 1, d), F32),
                        pltpu.VMEM((hps, 1, 1), F32)],
        compiler_params=_params(("parallel", "parallel", "arbitrary")),
        name="mlstm",
    )(gate_b, mq, mk, mv, mo, g_col, g_row, conv_wq, conv_wk, conv_bq, conv_bk, norm_w)


def _merge_kernel(x_ref, attn_ref, hm_ref, ga_ref, gm_ref, wab_ref, wmb_ref, mb_ref, wout_ref,
                  fnw_ref, wq_ref, x2_ref, h2_ref, q_ref):
    ya = _dot(attn_ref[...], wab_ref[...])
    ym = _dot(hm_ref[...], wmb_ref[...])
    y = (jax.nn.sigmoid(ga_ref[...].astype(F32) + mb_ref[0:1, :]) * ya
         + jax.nn.sigmoid(gm_ref[...].astype(F32) + mb_ref[1:2, :]) * ym)
    x2 = x_ref[...] + _dot(y.astype(BF16), wout_ref[...])
    x2_ref[...] = x2
    h2 = x2 * lax.rsqrt(jnp.mean(x2 * x2, axis=-1, keepdims=True) + EPS) * fnw_ref[...]
    h2b = h2.astype(BF16)
    h2_ref[...] = pltpu.bitcast(h2b, jnp.int32)
    q_ref[...] = _dot(h2b, wq_ref[...])


def _merge(x2d, attn, hm, ga, gm, wab, wmb, merge_b, wout, fnw, wq):
    t, d = x2d.shape
    tm = MERGE_TM
    nq = wq.shape[1]
    row = lambda w: pl.BlockSpec((tm, w), lambda i: (i, 0))
    const = lambda a: pl.BlockSpec(a.shape, lambda i: (0, 0))
    return pl.pallas_call(
        _merge_kernel,
        grid=(t // tm,),
        in_specs=[row(d), row(ATTN_W), row(MLSTM_W), row(d), row(d), const(wab), const(wmb),
                  const(merge_b), const(wout), const(fnw), const(wq)],
        out_specs=[row(d), pl.BlockSpec((tm // 2, d), lambda i: (i, 0)), row(nq)],
        out_shape=[jax.ShapeDtypeStruct((t, d), F32), jax.ShapeDtypeStruct((t // 2, d), jnp.int32),
                   jax.ShapeDtypeStruct((t, nq), F32)],
        compiler_params=_params(("parallel",)),
        name="merge",
    )(x2d, attn, hm, ga, gm, wab, wmb, merge_b, wout, fnw, wq)


_CAND_GROUPS = (
    ((0, 0, 0, 8),), ((0, 8, 0, 8),), ((1, 0, 0, 8),),
    ((2, 0, 0, 5), (4, 0, 5, 3)),
    ((3, 0, 0, 4), (5, 0, 4, 2), (6, 0, 6, 2)),
    ((7, 0, 0, 2),) + tuple((i, 0, i - 6, 1) for i in range(8, 14)),
    ((14, 0, 0, 1), (15, 0, 1, 1)),
)


def _select_kernel(q1_ref, q2_ref, k1_ref, k2_ref, e1_ref, na_ref, e2_ref, r2_ref,
                   s1_ref, s2_ref, w1_ref, w2_ref, rk1_ref, rk2_ref, v_ref, c0_ref, c_ref, e_ref):
    tn = q1_ref.shape[0]
    nk = PEER_N_KEYS
    topk = PEER_TOPK
    n_lt = tn // LANES
    lane_tiles = [slice(lt * LANES, (lt + 1) * LANES) for lt in range(n_lt)]
    halves = ((s1_ref, w1_ref, rk1_ref), (s2_ref, w2_ref, rk2_ref))

    def scores(k_ref, q_ref):
        kh, kl = _split2(k_ref[...])
        qh, ql = _split2(q_ref[...])
        return _dot_nt(kh, qh) + _dot_nt(kh, ql) + _dot_nt(kl, qh)

    s1_ref[...] = scores(k1_ref, q1_ref)
    s2_ref[...] = scores(k2_ref, q2_ref)
    key = lax.broadcasted_iota(jnp.int32, (nk, LANES), 0).astype(F32)

    def extract(break_ties):
        for s_ref, w_ref, rk_ref in halves:
            w_ref[...] = s_ref[...]
            rk_ref[...] = jnp.full(rk_ref.shape, float(topk), F32)
        for it in range(topk):
            for half, (_, w_ref, rk_ref) in enumerate(halves):
                for cols in lane_tiles:
                    s_cur = w_ref[:, cols]
                    m = jnp.max(s_cur, axis=0, keepdims=True)
                    if break_ties:
                        hit = key == jnp.min(jnp.where(s_cur == m, key, float(nk)), axis=0, keepdims=True)
                    else:
                        hit = s_cur == m
                    w_ref[:, cols] = jnp.where(hit, -jnp.inf, s_cur)
                    rk_ref[:, cols] = jnp.where(hit, float(it), rk_ref[:, cols])
                    v_ref[half, it, :, cols] = m

    extract(break_ties=False)
    off_count = jnp.zeros((1, LANES), F32)
    for _, _, rk_ref in halves:
        for cols in lane_tiles:
            taken = jnp.sum(jnp.where(rk_ref[:, cols] < float(topk), 1.0, 0.0), axis=0, keepdims=True)
            off_count = off_count + jnp.where(taken == float(topk), 0.0, 1.0)

    @pl.when(jnp.max(off_count) > 0.0)
    def _():
        extract(break_ties=True)

    row8 = lax.broadcasted_iota(jnp.int32, (SUBLANES, LANES), 0)
    seg_mask = {seg: (row8 >= seg[2]) & (row8 < seg[2] + seg[3]) for grp in _CAND_GROUPS for seg in grp}
    flats = []
    for grp in _CAND_GROUPS:
        f = jnp.full((SUBLANES, LANES), 1e9, F32)
        for seg in grp:
            i, j0, r0, _ = seg
            f = jnp.where(seg_mask[seg], (row8 + (i * topk + j0 - r0)).astype(F32), f)
        flats.append(f)
    flat = jnp.concatenate(flats, axis=0)
    for cols in lane_tiles:
        v1 = [v_ref[0, i, :, cols] for i in range(topk)]
        v2 = [v_ref[1, j, :, cols] for j in range(topk)]
        v2_groups = []
        for j0 in (0, SUBLANES):
            dense = jnp.zeros((SUBLANES, LANES), F32)
            for r in range(SUBLANES):
                dense = jnp.where(row8 == r, v2[j0 + r], dense)
            v2_groups.append(dense)
        top = v1[0] + v2[0]
        for g, grp in enumerate(_CAND_GROUPS):
            cand = jnp.full((SUBLANES, LANES), -jnp.inf, F32)
            for seg in grp:
                i, j0, r0, n = seg
                if n == 1 and j0 == 0:
                    piece = v1[i] + v2[0]
                else:
                    src = v2_groups[j0 // SUBLANES]
                    piece = v1[i] + (pltpu.roll(src, r0, axis=0) if r0 else src)
                cand = jnp.where(seg_mask[seg], piece, cand)
            rows = slice(g * SUBLANES, (g + 1) * SUBLANES)
            c0_ref[rows, cols] = cand
            e_ref[rows, cols] = jnp.exp(cand - top)

    def pick(break_ties):
        c_ref[...] = c0_ref[...]
        for _ in range(topk):
            for cols in lane_tiles:
                c_cur = c_ref[:, cols]
                m = jnp.max(c_cur, axis=0, keepdims=True)
                if break_ties:
                    hit = flat == jnp.min(jnp.where(c_cur == m, flat, 1e9), axis=0, keepdims=True)
                else:
                    hit = c_cur == m
                c_ref[:, cols] = jnp.where(hit, -jnp.inf, c_cur)

    pick(break_ties=False)
    off_count = jnp.zeros((1, LANES), F32)
    for cols in lane_tiles:
        taken = jnp.sum(jnp.where(flat < 1e8, jnp.where(c_ref[:, cols] == -jnp.inf, 1.0, 0.0), 0.0),
                        axis=0, keepdims=True)
        off_count = off_count + jnp.where(taken == float(topk), 0.0, 1.0)

    @pl.when(jnp.max(off_count) > 0.0)
    def _():
        pick(break_ties=True)

    for lt, cols in enumerate(lane_tiles):
        chosen = jnp.where(flat < 1e8, jnp.where(c_ref[:, cols] == -jnp.inf, 1.0, 0.0), 0.0)
        z = jnp.sum(chosen * e_ref[:, cols], axis=0, keepdims=True)
        counts = [jnp.zeros((1, LANES), F32) for _ in range(topk)]
        for g, grp in enumerate(_CAND_GROUPS):
            ch = chosen[g * SUBLANES:(g + 1) * SUBLANES, :]
            for seg in grp:
                counts[seg[0]] = counts[seg[0]] + jnp.sum(
                    jnp.where(seg_mask[seg], ch, 0.0), axis=0, keepdims=True)
        v1_top = v_ref[0, 0, :, cols]
        v2_top = v_ref[1, 0, :, cols]

        r1 = rk1_ref[:, cols]
        r2 = rk2_ref[:, cols]
        n_a = jnp.zeros((nk, LANES), F32)
        for i in range(topk):
            n_a = jnp.where(r1 == float(i), counts[i], n_a)
        in1 = r1 < float(topk)
        in2 = r2 < float(topk)
        e1 = jnp.where(in1, jnp.exp(s1_ref[:, cols] - v1_top) * (0.5 / z), 0.0)
        e2 = jnp.where(in2, jnp.exp(s2_ref[:, cols] - v2_top), 0.0)
        e1_ref[:, cols] = _pack_pairs(e1, e1)
        na_ref[:, cols] = _pack_pairs(n_a, n_a)
        e2_ref[:, cols] = _pack_pairs(e2[:nk // 2], e2[nk // 2:])
        r2_ref[:, cols] = _pack_pairs(r2[:nk // 2], r2[nk // 2:])


def _select(q, keys_1, keys_2):
    t = q.shape[0]
    tn = SELECT_TN
    assert t % tn == 0
    nk = PEER_N_KEYS
    q1 = pl.BlockSpec((tn, PEER_HALF), lambda i, h: (i, 2 * h))
    q2 = pl.BlockSpec((tn, PEER_HALF), lambda i, h: (i, 2 * h + 1))
    kspec = pl.BlockSpec((None, nk, PEER_HALF), lambda i, h: (h, 0, 0))
    ospec = lambda rows: pl.BlockSpec((None, rows, tn), lambda i, h: (h, 0, i))
    oshape = lambda rows: jax.ShapeDtypeStruct((PEER_HEADS, rows, t), jnp.int32)
    keys_buf = pltpu.VMEM((nk, tn), F32)
    return pl.pallas_call(
        _select_kernel,
        grid=(t // tn, PEER_HEADS),
        in_specs=[q1, q2, kspec, kspec],
        out_specs=[ospec(nk), ospec(nk), ospec(nk // 2), ospec(nk // 2)],
        out_shape=[oshape(nk), oshape(nk), oshape(nk // 2), oshape(nk // 2)],
        scratch_shapes=[keys_buf] * 6 + [pltpu.VMEM((2, PEER_TOPK, 1, tn), F32)]
        + [pltpu.VMEM((SUBLANES * len(_CAND_GROUPS), tn), F32)] * 3,
        compiler_params=_params(("parallel", "parallel")),
        name="peer_select",
    )(q, q, keys_1, keys_2)


def _experts_kernel(h2_ref, wd_ref, wut_ref, e1_in, na_in, e2_in, r2_in, x2_ref, o_ref,
                    acc_ref, z0_ref, z1_ref, act0_ref, act1_ref, e1_ref, na_ref, e2_ref, r2_ref):
    s = pl.program_id(1)
    n_chunks = pl.num_programs(1) - 2
    nk = PEER_N_KEYS
    tn = z0_ref.shape[1]
    rows_per_chunk = EXPERT_CHUNK // nk

    @pl.when(s == 0)
    def _():
        acc_ref[...] = jnp.zeros_like(acc_ref)
        e2_ref[...] = e2_in[...]
        r2_ref[...] = r2_in[...]

    chunk = jnp.clip(s - 1, 0, n_chunks - 1)
    r0 = (chunk % (SUBLANES // rows_per_chunk)) * rows_per_chunk
    for h in range(PEER_HEADS):
        for aa in range(rows_per_chunk):
            e1_ref[h, aa:aa + 1, :] = e1_in[h, pl.ds(r0 + aa, 1), :]
            na_ref[h, aa:aa + 1, :] = na_in[h, pl.ds(r0 + aa, 1), :]

    half = nk // 2
    groups = half // SUBLANES

    def gate_piece(lt, k_lo, k_hi, z_r, act_w):
        cols = slice(lt * LANES, (lt + 1) * LANES)
        g = [[jnp.zeros((2 * SUBLANES, LANES), BF16) for _ in range(k_lo, k_hi)]
             for _ in range(rows_per_chunk)]
        for h in range(PEER_HEADS):
            as_pair = lambda ref, aa: _as_bf16_pairs(
                jnp.broadcast_to(ref[h, aa:aa + 1, cols], (SUBLANES, LANES)))
            e1 = [as_pair(e1_ref, aa) for aa in range(rows_per_chunk)]
            n_a = [as_pair(na_ref, aa) for aa in range(rows_per_chunk)]
            for k in range(k_lo, k_hi):
                grp = slice(k * SUBLANES, (k + 1) * SUBLANES)
                r2 = _as_bf16_pairs(r2_ref[h, grp, cols])
                e2 = _as_bf16_pairs(e2_ref[h, grp, cols])
                for aa in range(rows_per_chunk):
                    g[aa][k - k_lo] = (g[aa][k - k_lo]
                                       + jnp.where(r2 < n_a[aa], e2, jnp.zeros_like(e2)) * e1[aa])
        for aa in range(rows_per_chunk):
            for k in range(k_lo, k_hi):
                g_lo, g_hi = _unpack_pairs(g[aa][k - k_lo])
                for g_half, base in ((g_lo, aa * nk), (g_hi, aa * nk + half)):
                    rows = slice(base + k * SUBLANES, base + (k + 1) * SUBLANES)
                    z = z_r[rows, cols]
                    act = z * (1.0 + lax.erf(z * (2.0 ** -0.5)))
                    act_w[rows, cols] = (act * g_half).astype(BF16)

    def step(parity, do_z, do_gate, do_up):
        z_w, z_r = (z0_ref, z1_ref) if parity == 0 else (z1_ref, z0_ref)
        act_w, act_r = (act1_ref, act0_ref) if parity == 0 else (act0_ref, act1_ref)
        d = acc_ref.shape[0]
        mxu = []
        for p in range(MXU_PIECES):
            pc = slice(p * tn // MXU_PIECES, (p + 1) * tn // MXU_PIECES)
            pw = slice(pc.start // 2, pc.stop // 2)

            def z_piece(pc=pc, pw=pw):
                z_w[:, pc] = _dot_nt(_rows_bf16(wd_ref[...]), _rows_bf16(h2_ref[pw, :]))

            if do_z:
                mxu.append(z_piece)
            for r in range(UP_ROW_PIECES if do_up else 0):
                rs = slice(r * d // UP_ROW_PIECES, (r + 1) * d // UP_ROW_PIECES)
                rw = slice(rs.start // 2, rs.stop // 2)

                def up_piece(pc=pc, rs=rs, rw=rw):
                    acc_ref[rs, pc] += _dot(_rows_bf16(wut_ref[rw, :]), act_r[:, pc])

                mxu.append(up_piece)
        gates = [(lt, k, k + GATE_GROUPS) for lt in range(tn // LANES)
                 for k in range(0, groups, GATE_GROUPS)] if do_gate else []
        per_mxu = -(-len(gates) // len(mxu))
        for m, piece in enumerate(mxu):
            piece()
            for lt, k_lo, k_hi in gates[m * per_mxu:(m + 1) * per_mxu]:
                gate_piece(lt, k_lo, k_hi, z_r, act_w)
        for lt, k_lo, k_hi in gates[len(mxu) * per_mxu:]:
            gate_piece(lt, k_lo, k_hi, z_r, act_w)

    last = n_chunks + 1
    variants = ((s == 0, 0, (True, False, False)),
                (s == 1, 1, (True, True, False)),
                ((s >= 2) & (s < n_chunks) & (s % 2 == 0), 0, (True, True, True)),
                ((s >= 2) & (s < n_chunks) & (s % 2 == 1), 1, (True, True, True)),
                (s == n_chunks, n_chunks % 2, (False, True, True)),
                (s == last, last % 2, (False, False, True)))
    for cond, parity, stages in variants:
        pl.when(cond)(functools.partial(step, parity, *stages))

    @pl.when(s == pl.num_programs(1) - 1)
    def _():
        o_ref[...] = x2_ref[...] + acc_ref[...].T


def _experts(h2, wd, wut, e1, na, e2, r2, x2):
    t, d = x2.shape
    tn = EXPERT_TN
    ne = 2 * wd.shape[0]
    nk = PEER_N_KEYS
    assert t % tn == 0 and ne % EXPERT_CHUNK == 0 and EXPERT_CHUNK % nk == 0
    n_chunks = ne // EXPERT_CHUNK
    assert n_chunks % 2 == 0
    tok = pl.BlockSpec((tn, d), lambda i, s: (i, 0))
    gate = pl.BlockSpec((PEER_HEADS, nk // 2, tn), lambda i, s: (0, 0, i))
    rpc = EXPERT_CHUNK // nk
    assert SUBLANES % rpc == 0
    chunks_per_group = SUBLANES // rpc
    row = pl.BlockSpec((PEER_HEADS, None, SUBLANES, tn),
                       lambda i, s: (0, jnp.clip(s - 1, 0, n_chunks - 1) // chunks_per_group, 0, i))
    e1 = e1.reshape(PEER_HEADS, nk // SUBLANES, SUBLANES, t)
    na = na.reshape(PEER_HEADS, nk // SUBLANES, SUBLANES, t)
    return pl.pallas_call(
        _experts_kernel,
        grid=(t // tn, n_chunks + 2),
        in_specs=[pl.BlockSpec((tn // 2, d), lambda i, s: (i, 0)),
                  pl.BlockSpec((EXPERT_CHUNK // 2, d), lambda i, s: (jnp.minimum(s, n_chunks - 1), 0)),
                  pl.BlockSpec((None, d // 2, EXPERT_CHUNK),
                               lambda i, s: (jnp.clip(s - 2, 0, n_chunks - 1), 0, 0)),
                  row, row, gate, gate, tok],
        out_specs=tok,
        out_shape=jax.ShapeDtypeStruct((t, d), F32),
        scratch_shapes=[pltpu.VMEM((d, tn), F32),
                        pltpu.VMEM((EXPERT_CHUNK, tn), F32), pltpu.VMEM((EXPERT_CHUNK, tn), F32),
                        pltpu.VMEM((EXPERT_CHUNK, tn), BF16), pltpu.VMEM((EXPERT_CHUNK, tn), BF16),
                        pltpu.VMEM((PEER_HEADS, rpc, tn), jnp.int32), pltpu.VMEM((PEER_HEADS, rpc, tn), jnp.int32),
                        pltpu.VMEM((PEER_HEADS, nk // 2, tn), jnp.int32),
                        pltpu.VMEM((PEER_HEADS, nk // 2, tn), jnp.int32)],
        compiler_params=_params(("parallel", "arbitrary")),
        name="peer_experts",
    )(h2, wd, wut, e1, na, e2, r2, x2)


def _rope_tables(seq):
    pos = jnp.arange(seq, dtype=F32)
    inv_freq = ROPE_THETA ** (-jnp.arange(0, ROT_DIM, 2, dtype=F32) / ROT_DIM)
    ang = pos[:, None] * inv_freq[None, :]
    cos, sin = jnp.cos(ang), jnp.sin(ang)
    rest = ATTN_HEAD_DIM - ROT_DIM
    cos_h = jnp.concatenate([cos, cos, jnp.ones((seq, rest), F32)], axis=-1)
    sin_h = jnp.concatenate([-sin, sin, jnp.zeros((seq, rest), F32)], axis=-1)
    return jnp.tile(cos_h, (1, ATTN_HEADS)), jnp.tile(sin_h, (1, ATTN_HEADS))


def kernel(x, mix_norm_w, w_in, q_norm_w, k_norm_w, conv_w, conv_b, igate_b, fgate_b, mlstm_norm_w,
           w_attn_branch, w_mlstm_branch, merge_b, w_out, ffn_norm_w, peer_w_query, peer_keys_1,
           peer_keys_2, peer_w_down, peer_w_up):
    b, s, d = x.shape
    t = b * s
    x2d = x.reshape(t, d)

    n_main = 3 * ATTN_W + 4 * MLSTM_W
    n_gate = 2 * MLSTM_HEADS
    w_cat = jnp.concatenate(
        [w_in[:, :n_main], w_in[:, n_main + n_gate:], w_in[:, n_main:n_main + n_gate],
         jnp.zeros((d, LANES - n_gate), w_in.dtype)], axis=1).astype(BF16)
    cos_t, sin_t = _rope_tables(s)
    head_of = jnp.arange(ATTN_W) // ATTN_HEAD_DIM
    bd = (head_of[:, None] == head_of[None, :]).astype(BF16)
    qnw = jnp.tile(q_norm_w, ATTN_HEADS).reshape(1, ATTN_W)
    knw = jnp.tile(k_norm_w, ATTN_HEADS).reshape(1, ATTN_W)

    aq, ak, av, mq, mk, mv, mo, ga, gm, gt = _inproj(
        x2d, mix_norm_w.reshape(1, d), w_cat, cos_t, sin_t, bd, qnw, knw, s)

    attn = _moba(aq.reshape(b, s, ATTN_W), ak.reshape(b, s, ATTN_W), av.reshape(b, s, ATTN_W))

    gates = gt[:, :n_gate].reshape(b, s, 2, MLSTM_HEADS)
    g_col = gates.transpose(0, 3, 1, 2)
    g_row = gates.transpose(0, 3, 2, 1)
    gate_b = jnp.stack([igate_b, fgate_b]).astype(F32)
    r3 = lambda a: a.reshape(b, s, MLSTM_W)
    hm = _mlstm(r3(mq), r3(mk), r3(mv), r3(mo), g_col, g_row, gate_b,
                conv_w[:, :MLSTM_W], conv_w[:, MLSTM_W:], conv_b[:MLSTM_W].reshape(1, MLSTM_W),
                conv_b[MLSTM_W:].reshape(1, MLSTM_W), mlstm_norm_w.reshape(1, MLSTM_W))

    x2, h2, q = _merge(x2d, attn.reshape(t, ATTN_W), hm.reshape(t, MLSTM_W), ga, gm,
                       w_attn_branch.astype(BF16), w_mlstm_branch.astype(BF16), merge_b,
                       w_out.astype(BF16), ffn_norm_w.reshape(1, d), peer_w_query.astype(BF16))

    e1, na, e2, r2 = _select(q, peer_keys_1, peer_keys_2)
    wut = peer_w_up.astype(BF16).reshape(-1, EXPERT_CHUNK, d).transpose(0, 2, 1)
    out = _experts(h2, _pack_rows(peer_w_down.astype(BF16)), _pack_rows(wut), e1, na, e2, r2, x2)
    return out.reshape(b, s, d)
```

```python
import functools

import jax
import jax.numpy as jnp
from jax import lax
from jax.experimental import pallas as pl
from jax.experimental.pallas import tpu as pltpu

F32 = jnp.float32
BF16 = jnp.bfloat16

EPS = 1e-6
ATTN_HEADS = 8
ATTN_HEAD_DIM = 64
ATTN_W = ATTN_HEADS * ATTN_HEAD_DIM
ROT_DIM = ATTN_HEAD_DIM // 4
ROPE_THETA = 500000.0
MOBA_BLOCK = 256
MOBA_TOPK = 3
MLSTM_HEADS = 4
MLSTM_HEAD_DIM = 128
MLSTM_W = MLSTM_HEADS * MLSTM_HEAD_DIM
CONV_WIDTH = 4
PEER_HEADS = 8
PEER_N_KEYS = 128
PEER_HALF = 128
PEER_TOPK = 16

LANES = 128
SUBLANES = 8
NEG = -1e30
VMEM_LIMIT = 56 * 1024 * 1024

INPROJ_TM = 512
MERGE_TM = 512
MOBA_MAX_BLOCKS = 16
MOBA_UNROLL = 4
MLSTM_CHUNK = 256
MLSTM_HEADS_PER_STEP = 4
SELECT_TN = 512
EXPERT_TN = 512
EXPERT_CHUNK = 1024
MXU_PIECES = 2
UP_ROW_PIECES = 4
GATE_GROUPS = 4


def _dot(a, b):
    return jnp.dot(a, b, preferred_element_type=F32)


def _dot_nt(a, b):
    return lax.dot_general(a, b, (((1,), (1,)), ((), ())), preferred_element_type=F32)


def _dot_tn(a, b):
    return lax.dot_general(a, b, (((0,), (0,)), ((), ())), preferred_element_type=F32)


def _bf16_bits(x):
    return lax.bitcast_convert_type(x.astype(BF16).astype(F32), jnp.int32)


def _pack_pairs(lo, hi):
    return lax.shift_right_logical(_bf16_bits(lo), jnp.int32(16)) | _bf16_bits(hi)


def _as_bf16_pairs(words):
    return pltpu.bitcast(words, BF16)


def _unpack_pairs(pairs):
    words = pltpu.bitcast(pairs, jnp.int32)
    lo = lax.bitcast_convert_type(lax.shift_left(words, jnp.int32(16)), F32)
    hi = lax.bitcast_convert_type(words & jnp.int32(-65536), F32)
    return lo, hi


def _pack_rows(x):
    *lead, m, n = x.shape
    pairs = jnp.swapaxes(x.reshape(*lead, m // 2, 2, n), -1, -2)
    return lax.bitcast_convert_type(pairs, jnp.int32)


def _rows_bf16(words):
    return pltpu.bitcast(words, BF16)


def _split2(x):
    hi = x.astype(BF16)
    lo = (x - hi.astype(F32)).astype(BF16)
    return hi, lo


def _params(sem, flags=None):
    return pltpu.CompilerParams(dimension_semantics=sem, vmem_limit_bytes=VMEM_LIMIT, flags=flags)


def _inproj_kernel(x_ref, nw_ref, w_ref, cos_ref, sin_ref, bd_ref, qnw_ref, knw_ref,
                   aq_ref, ak_ref, av_ref, mq_ref, mk_ref, mv_ref, mo_ref, ga_ref, gm_ref, gt_ref):
    x = x_ref[...]
    h = x * lax.rsqrt(jnp.mean(x * x, axis=-1, keepdims=True) + EPS) * nw_ref[...]
    hb = h.astype(BF16)

    def proj(lo, width):
        return _dot(hb, w_ref[:, lo:lo + width])

    lane = lax.broadcasted_iota(jnp.int32, (x.shape[0], ATTN_W), 1) % ATTN_HEAD_DIM
    half = ROT_DIM // 2

    def qk_norm_rope(t, nw):
        hi, lo = _split2(t * t)
        ms = (_dot(hi, bd_ref[...]) + _dot(lo, bd_ref[...])) * (1.0 / ATTN_HEAD_DIM)
        tn = t * lax.rsqrt(ms + EPS) * nw
        fwd = pltpu.roll(tn, ATTN_W - half, axis=1)
        bwd = pltpu.roll(tn, half, axis=1)
        swapped = jnp.where(lane < half, fwd, jnp.where(lane < ROT_DIM, bwd, 0.0))
        return tn * cos_ref[...] + swapped * sin_ref[...]

    aq_ref[...] = qk_norm_rope(proj(0, ATTN_W), qnw_ref[...]).astype(BF16)
    ak_ref[...] = qk_norm_rope(proj(ATTN_W, ATTN_W), knw_ref[...]).astype(BF16)
    av_ref[...] = proj(2 * ATTN_W, ATTN_W).astype(BF16)
    base = 3 * ATTN_W
    mq_ref[...] = proj(base, MLSTM_W).astype(BF16)
    mk_ref[...] = proj(base + MLSTM_W, MLSTM_W).astype(BF16)
    mv_ref[...] = proj(base + 2 * MLSTM_W, MLSTM_W).astype(BF16)
    mo_ref[...] = proj(base + 3 * MLSTM_W, MLSTM_W).astype(BF16)
    base = base + 4 * MLSTM_W
    d = x.shape[1]
    ga_ref[...] = proj(base, d).astype(BF16)
    gm_ref[...] = proj(base + d, d).astype(BF16)
    gt_ref[...] = proj(base + 2 * d, LANES)


def _inproj(x2d, mix_norm_w, w_cat, cos_t, sin_t, bd, qnw, knw, seq):
    t, d = x2d.shape
    tm = INPROJ_TM
    assert t % tm == 0 and seq % tm == 0
    tiles_per_seq = seq // tm
    ncols = w_cat.shape[1]
    row = lambda w: pl.BlockSpec((tm, w), lambda i: (i, 0))
    const = lambda shape: pl.BlockSpec(shape, lambda i: (0, 0))
    pos = pl.BlockSpec((tm, ATTN_W), lambda i: (i % tiles_per_seq, 0))
    outs = [ATTN_W] * 3 + [MLSTM_W] * 4 + [d, d]
    return pl.pallas_call(
        _inproj_kernel,
        grid=(t // tm,),
        in_specs=[row(d), const((1, d)), const((d, ncols)), pos, pos, const((ATTN_W, ATTN_W)),
                  const((1, ATTN_W)), const((1, ATTN_W))],
        out_specs=[row(w) for w in outs] + [row(LANES)],
        out_shape=[jax.ShapeDtypeStruct((t, w), BF16) for w in outs]
        + [jax.ShapeDtypeStruct((t, LANES), F32)],
        compiler_params=_params(("parallel",)),
        name="inproj",
    )(x2d, mix_norm_w, w_cat, cos_t, sin_t, bd, qnw, knw)


def _moba_kernel(q_ref, k_ref, v_ref, o_ref, kmean_ref, kext_ref, vext_ref, s_ref, mx_ref, *, n_blocks):
    i = pl.program_id(2)
    blk = MOBA_BLOCK
    hd = ATTN_HEAD_DIM
    heads = (0, 1)
    lane = lax.broadcasted_iota(jnp.int32, (blk, LANES), 1)
    in_head = [lane < hd, lane >= hd]
    spare0 = [hd, 0]

    @pl.when(i == 0)
    def _():
        kmean_ref[...] = jnp.zeros_like(kmean_ref)
        for n in range(n_blocks):
            rows = slice(n * blk, (n + 1) * blk)
            kb = k_ref[rows, :]
            vb = v_ref[rows, :]
            kmean_ref[n:n + 1, :] = jnp.mean(kb.astype(F32), axis=0, keepdims=True)
            for h in heads:
                onehot = jnp.where(lane == spare0[h] + n, 1.0, 0.0).astype(BF16)
                kext_ref[h, rows, :] = jnp.where(in_head[h], kb, onehot)
                vext_ref[h, rows, :] = jnp.where(in_head[h], vb, jnp.ones_like(vb))

    q = q_ref[...]
    kmean_hi, kmean_lo = _split2(kmean_ref[...])
    nb = MOBA_MAX_BLOCKS
    brow = lax.broadcasted_iota(jnp.int32, (nb, blk), 0)
    qext = []
    for h in heads:
        qh = jnp.where(in_head[h], q, jnp.zeros_like(q)) * jnp.asarray(hd ** -0.5, BF16)
        g_t = (_dot_nt(kmean_hi, qh) + _dot_nt(kmean_lo, qh))[0:nb, :]
        rank = jnp.zeros((nb, blk), F32)
        for m in range(n_blocks):
            gm = g_t[m:m + 1, :]
            beats = jnp.where(gm > g_t, 1.0, jnp.where(gm == g_t, jnp.where(brow > m, 1.0, 0.0), 0.0))
            rank = rank + jnp.where(m < i, beats, 0.0)
        bias_t = jnp.where(brow < i, jnp.where(rank < MOBA_TOPK, 0.0, NEG), 0.0)
        bias = jnp.concatenate([bias_t, jnp.zeros((LANES - nb, blk), F32)], axis=0).T
        if spare0[h]:
            bias = pltpu.roll(bias, spare0[h], axis=1)
        qext.append(qh + bias.astype(BF16))

    own = pl.multiple_of(i * blk, blk)
    causal = (lax.broadcasted_iota(jnp.int32, (blk, blk), 1)
              <= lax.broadcasted_iota(jnp.int32, (blk, blk), 0))
    unroll = MOBA_UNROLL

    def fold(s):
        return jnp.maximum(s[:, :LANES], s[:, LANES:])

    for h in heads:
        mx_ref[h] = jnp.full((blk, LANES), NEG, F32)

    def scores(t, carry):
        for u in range(unroll):
            n = unroll * t + u
            start = pl.multiple_of(n * blk, blk)
            for h in heads:
                s = _dot_nt(qext[h], kext_ref[h, pl.ds(start, blk), :])
                s_ref[h, n] = s
                m_old = mx_ref[h]
                m_cand = jnp.maximum(m_old, fold(s))
                mx_ref[h] = m_cand if u == 0 else jnp.where(n < i, m_cand, m_old)
        return carry

    lax.fori_loop(0, (i + unroll - 1) // unroll, scores, 0)
    m_fin = []
    for h in heads:
        s = jnp.where(causal, _dot_nt(qext[h], kext_ref[h, pl.ds(own, blk), :]), NEG)
        s_ref[h, i] = s
        m_fin.append(jnp.max(jnp.maximum(mx_ref[h], fold(s)), axis=1, keepdims=True))

    def values(t, acc):
        out = list(acc)
        for u in range(unroll):
            n = unroll * t + u
            n_read = jnp.minimum(n, i)
            start = pl.multiple_of(n_read * blk, blk)
            for h in heads:
                m_eff = m_fin[h] if u == 0 else jnp.where(n <= i, m_fin[h], -NEG)
                p = jnp.exp(s_ref[h, n_read] - m_eff).astype(BF16)
                out[h] = out[h] + _dot(p, vext_ref[h, pl.ds(start, blk), :])
        return tuple(out)

    acc = lax.fori_loop(0, (i + unroll) // unroll, values,
                        tuple(jnp.zeros((blk, LANES), F32) for _ in heads))
    outs = [a / pltpu.roll(a, hd, axis=1) for a in acc]
    o_ref[...] = jnp.where(in_head[0], outs[0], outs[1]).astype(o_ref.dtype)


def _moba(q, k, v):
    b, s, w = q.shape
    assert s % MOBA_BLOCK == 0 and w % LANES == 0
    n_blocks = s // MOBA_BLOCK
    assert n_blocks <= MOBA_MAX_BLOCKS and n_blocks % MOBA_UNROLL == 0
    qspec = pl.BlockSpec((None, MOBA_BLOCK, LANES), lambda bi, hp, i: (bi, i, hp))
    kvspec = pl.BlockSpec((None, s, LANES), lambda bi, hp, i: (bi, 0, hp))
    return pl.pallas_call(
        functools.partial(_moba_kernel, n_blocks=n_blocks),
        grid=(b, w // LANES, n_blocks),
        in_specs=[qspec, kvspec, kvspec],
        out_specs=qspec,
        out_shape=jax.ShapeDtypeStruct((b, s, w), BF16),
        scratch_shapes=[pltpu.VMEM((LANES, LANES), F32),
                        pltpu.VMEM((2, s, LANES), BF16), pltpu.VMEM((2, s, LANES), BF16),
                        pltpu.VMEM((2, n_blocks, MOBA_BLOCK, MOBA_BLOCK), F32),
                        pltpu.VMEM((2, MOBA_BLOCK, LANES), F32)],
        compiler_params=_params(("parallel", "parallel", "arbitrary")),
        name="moba",
    )(q, k, v)


def _log_sigmoid(f):
    return jnp.minimum(f, 0.0) - jnp.log(1.0 + jnp.exp(-jnp.abs(f)))


def _mlstm_kernel(gb_ref, mq_ref, mk_ref, mv_ref, mo_ref, gc_ref, gr_ref, cwq_ref, cwk_ref,
                  cbq_ref, cbk_ref, nw_ref, o_ref, c_ref, n_ref, m_ref):
    group = pl.program_id(1)
    c = pl.program_id(2)
    L = MLSTM_CHUNK
    d = MLSTM_HEAD_DIM
    heads = range(MLSTM_HEADS_PER_STEP)
    lanes = [slice(h * d, (h + 1) * d) for h in heads]

    @pl.when(c == 0)
    def _():
        c_ref[...] = jnp.zeros_like(c_ref)
        n_ref[...] = jnp.zeros_like(n_ref)
        m_ref[...] = jnp.zeros_like(m_ref)

    start = pl.multiple_of(c * L, L)
    prev = pl.multiple_of(jnp.maximum(c - 1, 0) * L, L)
    row_d = lax.broadcasted_iota(jnp.int32, (L, d), 0)

    def conv_silu(u_ref, w_ref, b_ref, ls):
        u = u_ref[pl.ds(start, L), ls].astype(F32)
        pv = u_ref[pl.ds(prev, L), ls].astype(F32)
        pv = jnp.where(c > 0, pv, 0.0)
        y = u * w_ref[CONV_WIDTH - 1:CONV_WIDTH, ls] + b_ref[:, ls]
        for k in range(1, CONV_WIDTH):
            shifted = jnp.where(row_d < k, pltpu.roll(pv, k, axis=0), pltpu.roll(u, k, axis=0))
            y = y + shifted * w_ref[CONV_WIDTH - 1 - k:CONV_WIDTH - k, ls]
        return y * jax.nn.sigmoid(y)

    qc = [conv_silu(mq_ref, cwq_ref, cbq_ref, ls) for ls in lanes]
    kc = [conv_silu(mk_ref, cwk_ref, cbk_ref, ls) * (d ** -0.5) for ls in lanes]
    qb = [x.astype(BF16) for x in qc]
    kb = [x.astype(BF16) for x in kc]
    vb = [mv_ref[:, ls] for ls in lanes]

    ib = [gb_ref[0, group * MLSTM_HEADS_PER_STEP + h] for h in heads]
    fb = [gb_ref[1, group * MLSTM_HEADS_PER_STEP + h] for h in heads]
    gcol = [gc_ref[h] for h in heads]
    grow = [gr_ref[h] for h in heads]
    ig_col = [gcol[h][:, 0:1] + ib[h] for h in heads]
    ig_row = [grow[h][0:1, :] + ib[h] for h in heads]
    lf_col = [_log_sigmoid(gcol[h][:, 1:2] + fb[h]) for h in heads]
    lf_row = [_log_sigmoid(grow[h][1:2, :] + fb[h]) for h in heads]

    r = lax.broadcasted_iota(jnp.int32, (L, L), 0)
    s = lax.broadcasted_iota(jnp.int32, (L, L), 1)
    causal = s <= r
    tri_l = jnp.where(causal, 1.0, 0.0).astype(BF16)
    tri_u = jnp.where(r <= s, 1.0, 0.0).astype(BF16)
    lfc = [_split2(jnp.broadcast_to(lf_col[h], (L, L))) for h in heads]
    lfr = [_split2(jnp.broadcast_to(lf_row[h], (L, L))) for h in heads]
    b_t = [_dot(tri_l, lfc[h][0]) + _dot(tri_l, lfc[h][1]) for h in heads]
    b_s = [_dot(lfr[h][0], tri_u) + _dot(lfr[h][1], tri_u) for h in heads]

    m_prev = [m_ref[h] for h in heads]
    b_col = [b_t[h][:, 0:1] for h in heads]
    log_d = [jnp.where(causal, b_t[h] - b_s[h] + ig_row[h], NEG) for h in heads]
    m_inter = [b_col[h] + m_prev[h] for h in heads]
    m_t = [jnp.maximum(m_inter[h], jnp.max(log_d[h], axis=1, keepdims=True)) for h in heads]
    a = [jnp.exp(log_d[h] - m_t[h]) * _dot_nt(qb[h], kb[h]) for h in heads]
    w_inter = [jnp.exp(m_inter[h] - m_t[h]) for h in heads]
    c_prev = [c_ref[h] for h in heads]
    n_prev = [n_ref[h] for h in heads]
    num = [_dot(a[h].astype(BF16), vb[h]) + w_inter[h] * _dot(qb[h], c_prev[h].astype(BF16)) for h in heads]
    den = [jnp.sum(a[h], axis=1, keepdims=True)
           + w_inter[h] * jnp.sum(qc[h] * n_prev[h], axis=1, keepdims=True) for h in heads]
    hs = [num[h] / jnp.maximum(jnp.abs(den[h]), jnp.exp(-m_t[h])) for h in heads]

    m_new = [m_t[h][L - 1:L, :] for h in heads]
    b_last = [b_col[h][L - 1:L, :] for h in heads]
    ws_col = [jnp.exp(b_last[h] - b_col[h] + ig_col[h] - m_new[h]) for h in heads]
    decay = [jnp.exp(b_last[h] + m_prev[h] - m_new[h]) for h in heads]
    for h in heads:
        c_ref[h] = decay[h] * c_prev[h] + _dot_tn(kb[h], (ws_col[h] * vb[h].astype(F32)).astype(BF16))
        n_ref[h] = decay[h] * n_prev[h] + jnp.sum(ws_col[h] * kc[h], axis=0, keepdims=True)
        m_ref[h] = m_new[h]

    for h, ls in zip(heads, lanes):
        hn = hs[h] * lax.rsqrt(jnp.mean(hs[h] * hs[h], axis=-1, keepdims=True) + EPS) * nw_ref[:, ls]
        o_ref[:, ls] = (jax.nn.sigmoid(mo_ref[:, ls].astype(F32)) * hn).astype(o_ref.dtype)


def _mlstm(mq, mk, mv, mo, g_col, g_row, gate_b, conv_wq, conv_wk, conv_bq, conv_bk, norm_w):
    b, s, w = mq.shape
    L = MLSTM_CHUNK
    d = MLSTM_HEAD_DIM
    hps = MLSTM_HEADS_PER_STEP
    assert s % L == 0 and (w // d) % hps == 0
    seq = pl.BlockSpec((None, s, hps * d), lambda bi, g, c: (bi, 0, g))
    chunk = pl.BlockSpec((None, L, hps * d), lambda bi, g, c: (bi, c, g))
    per_group = lambda rows: pl.BlockSpec((rows, hps * d), lambda bi, g, c: (0, g))
    return pl.pallas_call(
        _mlstm_kernel,
        grid=(b, w // d // hps, s // L),
        in_specs=[pl.BlockSpec(memory_space=pltpu.SMEM), seq, seq, chunk, chunk,
                  pl.BlockSpec((None, hps, L, 2), lambda bi, g, c: (bi, g, c, 0)),
                  pl.BlockSpec((None, hps, 2, L), lambda bi, g, c: (bi, g, 0, c)),
                  per_group(CONV_WIDTH), per_group(CONV_WIDTH), per_group(1), per_group(1), per_group(1)],
        out_specs=chunk,
        out_shape=jax.ShapeDtypeStruct((b, s, w), BF16),
        scratch_shapes=[pltpu.VMEM((hps, d, d), F32), pltpu.VMEM((hps, 1, d), F32),
                        pltpu.VMEM((hps, 1, 1), F32)],
        compiler_params=_params(("parallel", "parallel", "arbitrary")),
        name="mlstm",
    )(gate_b, mq, mk, mv, mo, g_col, g_row, conv_wq, conv_wk, conv_bq, conv_bk, norm_w)


def _merge_kernel(x_ref, attn_ref, hm_ref, ga_ref, gm_ref, wab_ref, wmb_ref, mb_ref, wout_ref,
                  fnw_ref, wq_ref, x2_ref, h2_ref, q_ref):
    ya = _dot(attn_ref[...], wab_ref[...])
    ym = _dot(hm_ref[...], wmb_ref[...])
    y = (jax.nn.sigmoid(ga_ref[...].astype(F32) + mb_ref[0:1, :]) * ya
         + jax.nn.sigmoid(gm_ref[...].astype(F32) + mb_ref[1:2, :]) * ym)
    x2 = x_ref[...] + _dot(y.astype(BF16), wout_ref[...])
    x2_ref[...] = x2
    h2 = x2 * lax.rsqrt(jnp.mean(x2 * x2, axis=-1, keepdims=True) + EPS) * fnw_ref[...]
    h2b = h2.astype(BF16)
    h2_ref[...] = pltpu.bitcast(h2b, jnp.int32)
    q_ref[...] = _dot(h2b, wq_ref[...])


def _merge(x2d, attn, hm, ga, gm, wab, wmb, merge_b, wout, fnw, wq):
    t, d = x2d.shape
    tm = MERGE_TM
    nq = wq.shape[1]
    row = lambda w: pl.BlockSpec((tm, w), lambda i: (i, 0))
    const = lambda a: pl.BlockSpec(a.shape, lambda i: (0, 0))
    return pl.pallas_call(
        _merge_kernel,
        grid=(t // tm,),
        in_specs=[row(d), row(ATTN_W), row(MLSTM_W), row(d), row(d), const(wab), const(wmb),
                  const(merge_b), const(wout), const(fnw), const(wq)],
        out_specs=[row(d), pl.BlockSpec((tm // 2, d), lambda i: (i, 0)), row(nq)],
        out_shape=[jax.ShapeDtypeStruct((t, d), F32), jax.ShapeDtypeStruct((t // 2, d), jnp.int32),
                   jax.ShapeDtypeStruct((t, nq), F32)],
        compiler_params=_params(("parallel",)),
        name="merge",
    )(x2d, attn, hm, ga, gm, wab, wmb, merge_b, wout, fnw, wq)


_CAND_GROUPS = (
    ((0, 0, 0, 8),), ((0, 8, 0, 8),), ((1, 0, 0, 8),),
    ((2, 0, 0, 5), (4, 0, 5, 3)),
    ((3, 0, 0, 4), (5, 0, 4, 2), (6, 0, 6, 2)),
    ((7, 0, 0, 2),) + tuple((i, 0, i - 6, 1) for i in range(8, 14)),
    ((14, 0, 0, 1), (15, 0, 1, 1)),
)


def _select_kernel(q1_ref, q2_ref, k1_ref, k2_ref, e1_ref, na_ref, e2_ref, r2_ref,
                   s1_ref, s2_ref, w1_ref, w2_ref, rk1_ref, rk2_ref, v_ref, c0_ref, c_ref, e_ref):
    tn = q1_ref.shape[0]
    nk = PEER_N_KEYS
    topk = PEER_TOPK
    n_lt = tn // LANES
    lane_tiles = [slice(lt * LANES, (lt + 1) * LANES) for lt in range(n_lt)]
    halves = ((s1_ref, w1_ref, rk1_ref), (s2_ref, w2_ref, rk2_ref))

    def scores(k_ref, q_ref):
        kh, kl = _split2(k_ref[...])
        qh, ql = _split2(q_ref[...])
        return _dot_nt(kh, qh) + _dot_nt(kh, ql) + _dot_nt(kl, qh)

    s1_ref[...] = scores(k1_ref, q1_ref)
    s2_ref[...] = scores(k2_ref, q2_ref)
    key = lax.broadcasted_iota(jnp.int32, (nk, LANES), 0).astype(F32)

    def extract(break_ties):
        for s_ref, w_ref, rk_ref in halves:
            w_ref[...] = s_ref[...]
            rk_ref[...] = jnp.full(rk_ref.shape, float(topk), F32)
        for it in range(topk):
            for half, (_, w_ref, rk_ref) in enumerate(halves):
                for cols in lane_tiles:
                    s_cur = w_ref[:, cols]
                    m = jnp.max(s_cur, axis=0, keepdims=True)
                    if break_ties:
                        hit = key == jnp.min(jnp.where(s_cur == m, key, float(nk)), axis=0, keepdims=True)
                    else:
                        hit = s_cur == m
                    w_ref[:, cols] = jnp.where(hit, -jnp.inf, s_cur)
                    rk_ref[:, cols] = jnp.where(hit, float(it), rk_ref[:, cols])
                    v_ref[half, it, :, cols] = m

    extract(break_ties=False)
    off_count = jnp.zeros((1, LANES), F32)
    for _, _, rk_ref in halves:
        for cols in lane_tiles:
            taken = jnp.sum(jnp.where(rk_ref[:, cols] < float(topk), 1.0, 0.0), axis=0, keepdims=True)
            off_count = off_count + jnp.where(taken == float(topk), 0.0, 1.0)

    @pl.when(jnp.max(off_count) > 0.0)
    def _():
        extract(break_ties=True)

    row8 = lax.broadcasted_iota(jnp.int32, (SUBLANES, LANES), 0)
    seg_mask = {seg: (row8 >= seg[2]) & (row8 < seg[2] + seg[3]) for grp in _CAND_GROUPS for seg in grp}
    flats = []
    for grp in _CAND_GROUPS:
        f = jnp.full((SUBLANES, LANES), 1e9, F32)
        for seg in grp:
            i, j0, r0, _ = seg
            f = jnp.where(seg_mask[seg], (row8 + (i * topk + j0 - r0)).astype(F32), f)
        flats.append(f)
    flat = jnp.concatenate(flats, axis=0)
    for cols in lane_tiles:
        v1 = [v_ref[0, i, :, cols] for i in range(topk)]
        v2 = [v_ref[1, j, :, cols] for j in range(topk)]
        v2_groups = []
        for j0 in (0, SUBLANES):
            dense = jnp.zeros((SUBLANES, LANES), F32)
            for r in range(SUBLANES):
                dense = jnp.where(row8 == r, v2[j0 + r], dense)
            v2_groups.append(dense)
        top = v1[0] + v2[0]
        for g, grp in enumerate(_CAND_GROUPS):
            cand = jnp.full((SUBLANES, LANES), -jnp.inf, F32)
            for seg in grp:
                i, j0, r0, n = seg
                if n == 1 and j0 == 0:
                    piece = v1[i] + v2[0]
                else:
                    src = v2_groups[j0 // SUBLANES]
                    piece = v1[i] + (pltpu.roll(src, r0, axis=0) if r0 else src)
                cand = jnp.where(seg_mask[seg], piece, cand)
            rows = slice(g * SUBLANES, (g + 1) * SUBLANES)
            c0_ref[rows, cols] = cand
            e_ref[rows, cols] = jnp.exp(cand - top)

    def pick(break_ties):
        c_ref[...] = c0_ref[...]
        for _ in range(topk):
            for cols in lane_tiles:
                c_cur = c_ref[:, cols]
                m = jnp.max(c_cur, axis=0, keepdims=True)
                if break_ties:
                    hit = flat == jnp.min(jnp.where(c_cur == m, flat, 1e9), axis=0, keepdims=True)
                else:
                    hit = c_cur == m
                c_ref[:, cols] = jnp.where(hit, -jnp.inf, c_cur)

    pick(break_ties=False)
    off_count = jnp.zeros((1, LANES), F32)
    for cols in lane_tiles:
        taken = jnp.sum(jnp.where(flat < 1e8, jnp.where(c_ref[:, cols] == -jnp.inf, 1.0, 0.0), 0.0),
                        axis=0, keepdims=True)
        off_count = off_count + jnp.where(taken == float(topk), 0.0, 1.0)

    @pl.when(jnp.max(off_count) > 0.0)
    def _():
        pick(break_ties=True)

    for lt, cols in enumerate(lane_tiles):
        chosen = jnp.where(flat < 1e8, jnp.where(c_ref[:, cols] == -jnp.inf, 1.0, 0.0), 0.0)
        z = jnp.sum(chosen * e_ref[:, cols], axis=0, keepdims=True)
        counts = [jnp.zeros((1, LANES), F32) for _ in range(topk)]
        for g, grp in enumerate(_CAND_GROUPS):
            ch = chosen[g * SUBLANES:(g + 1) * SUBLANES, :]
            for seg in grp:
                counts[seg[0]] = counts[seg[0]] + jnp.sum(
                    jnp.where(seg_mask[seg], ch, 0.0), axis=0, keepdims=True)
        v1_top = v_ref[0, 0, :, cols]
        v2_top = v_ref[1, 0, :, cols]

        r1 = rk1_ref[:, cols]
        r2 = rk2_ref[:, cols]
        n_a = jnp.zeros((nk, LANES), F32)
        for i in range(topk):
            n_a = jnp.where(r1 == float(i), counts[i], n_a)
        in1 = r1 < float(topk)
        in2 = r2 < float(topk)
        e1 = jnp.where(in1, jnp.exp(s1_ref[:, cols] - v1_top) * (0.5 / z), 0.0)
        e2 = jnp.where(in2, jnp.exp(s2_ref[:, cols] - v2_top), 0.0)
        e1_ref[:, cols] = _pack_pairs(e1, e1)
        na_ref[:, cols] = _pack_pairs(n_a, n_a)
        e2_ref[:, cols] = _pack_pairs(e2[:nk // 2], e2[nk // 2:])
        r2_ref[:, cols] = _pack_pairs(r2[:nk // 2], r2[nk // 2:])


def _select(q, keys_1, keys_2):
    t = q.shape[0]
    tn = SELECT_TN
    assert t % tn == 0
    nk = PEER_N_KEYS
    q1 = pl.BlockSpec((tn, PEER_HALF), lambda i, h: (i, 2 * h))
    q2 = pl.BlockSpec((tn, PEER_HALF), lambda i, h: (i, 2 * h + 1))
    kspec = pl.BlockSpec((None, nk, PEER_HALF), lambda i, h: (h, 0, 0))
    ospec = lambda rows: pl.BlockSpec((None, rows, tn), lambda i, h: (h, 0, i))
    oshape = lambda rows: jax.ShapeDtypeStruct((PEER_HEADS, rows, t), jnp.int32)
    keys_buf = pltpu.VMEM((nk, tn), F32)
    return pl.pallas_call(
        _select_kernel,
        grid=(t // tn, PEER_HEADS),
        in_specs=[q1, q2, kspec, kspec],
        out_specs=[ospec(nk), ospec(nk), ospec(nk // 2), ospec(nk // 2)],
        out_shape=[oshape(nk), oshape(nk), oshape(nk // 2), oshape(nk // 2)],
        scratch_shapes=[keys_buf] * 6 + [pltpu.VMEM((2, PEER_TOPK, 1, tn), F32)]
        + [pltpu.VMEM((SUBLANES * len(_CAND_GROUPS), tn), F32)] * 3,
        compiler_params=_params(("parallel", "parallel")),
        name="peer_select",
    )(q, q, keys_1, keys_2)


def _experts_kernel(h2_ref, w_ref, e1_in, na_in, e2_in, r2_in, x2_ref, o_ref,
                    acc_ref, z0_ref, z1_ref, act0_ref, act1_ref, e1_ref, na_ref, e2_ref, r2_ref):
    s = pl.program_id(1)
    n_chunks = pl.num_programs(1) - 2
    nk = PEER_N_KEYS
    tn = z0_ref.shape[1]
    wd_ref, wut_ref = w_ref.at[0], w_ref.at[1]
    rows_per_chunk = EXPERT_CHUNK // nk

    @pl.when(s == 0)
    def _():
        acc_ref[...] = jnp.zeros_like(acc_ref)
        e2_ref[...] = e2_in[...]
        r2_ref[...] = r2_in[...]

    chunk = jnp.clip(s - 1, 0, n_chunks - 1)
    r0 = (chunk % (SUBLANES // rows_per_chunk)) * rows_per_chunk
    for h in range(PEER_HEADS):
        for aa in range(rows_per_chunk):
            e1_ref[h, aa:aa + 1, :] = e1_in[h, pl.ds(r0 + aa, 1), :]
            na_ref[h, aa:aa + 1, :] = na_in[h, pl.ds(r0 + aa, 1), :]

    half = nk // 2
    groups = half // SUBLANES

    def gate_piece(lt, k_lo, k_hi, z_r, act_w):
        cols = slice(lt * LANES, (lt + 1) * LANES)
        g = [[jnp.zeros((2 * SUBLANES, LANES), BF16) for _ in range(k_lo, k_hi)]
             for _ in range(rows_per_chunk)]
        for h in range(PEER_HEADS):
            as_pair = lambda ref, aa: _as_bf16_pairs(
                jnp.broadcast_to(ref[h, aa:aa + 1, cols], (SUBLANES, LANES)))
            e1 = [as_pair(e1_ref, aa) for aa in range(rows_per_chunk)]
            n_a = [as_pair(na_ref, aa) for aa in range(rows_per_chunk)]
            for k in range(k_lo, k_hi):
                grp = slice(k * SUBLANES, (k + 1) * SUBLANES)
                r2 = _as_bf16_pairs(r2_ref[h, grp, cols])
                e2 = _as_bf16_pairs(e2_ref[h, grp, cols])
                for aa in range(rows_per_chunk):
                    g[aa][k - k_lo] = (g[aa][k - k_lo]
                                       + jnp.where(r2 < n_a[aa], e2, jnp.zeros_like(e2)) * e1[aa])
        for aa in range(rows_per_chunk):
            for k in range(k_lo, k_hi):
                g_lo, g_hi = _unpack_pairs(g[aa][k - k_lo])
                for g_half, base in ((g_lo, aa * nk), (g_hi, aa * nk + half)):
                    rows = slice(base + k * SUBLANES, base + (k + 1) * SUBLANES)
                    z = z_r[rows, cols]
                    act = z * (1.0 + lax.erf(z * (2.0 ** -0.5)))
                    act_w[rows, cols] = (act * g_half).astype(BF16)

    def step(parity, do_z, do_gate, do_up):
        z_w, z_r = (z0_ref, z1_ref) if parity == 0 else (z1_ref, z0_ref)
        act_w, act_r = (act1_ref, act0_ref) if parity == 0 else (act0_ref, act1_ref)
        d = acc_ref.shape[0]
        mxu = []
        for p in range(MXU_PIECES):
            pc = slice(p * tn // MXU_PIECES, (p + 1) * tn // MXU_PIECES)
            pw = slice(pc.start // 2, pc.stop // 2)

            def z_piece(pc=pc, pw=pw):
                z_w[:, pc] = _dot_nt(_rows_bf16(wd_ref[...]), _rows_bf16(h2_ref[pw, :]))

            if do_z:
                mxu.append(z_piece)
            for r in range(UP_ROW_PIECES if do_up else 0):
                rs = slice(r * d // UP_ROW_PIECES, (r + 1) * d // UP_ROW_PIECES)
                rw = slice(rs.start // 2, rs.stop // 2)

                def up_piece(pc=pc, rs=rs, rw=rw):
                    acc_ref[rs, pc] += _dot(_rows_bf16(wut_ref[rw, :]), act_r[:, pc])

                mxu.append(up_piece)
        gates = [(lt, k, k + GATE_GROUPS) for lt in range(tn // LANES)
                 for k in range(0, groups, GATE_GROUPS)] if do_gate else []
        per_mxu = -(-len(gates) // len(mxu))
        for m, piece in enumerate(mxu):
            piece()
            for lt, k_lo, k_hi in gates[m * per_mxu:(m + 1) * per_mxu]:
                gate_piece(lt, k_lo, k_hi, z_r, act_w)
        for lt, k_lo, k_hi in gates[len(mxu) * per_mxu:]:
            gate_piece(lt, k_lo, k_hi, z_r, act_w)

    last = n_chunks + 1
    variants = ((s == 0, 0, (True, False, False)),
                (s == 1, 1, (True, True, False)),
                ((s >= 2) & (s < n_chunks) & (s % 2 == 0), 0, (True, True, True)),
                ((s >= 2) & (s < n_chunks) & (s % 2 == 1), 1, (True, True, True)),
                (s == n_chunks, n_chunks % 2, (False, True, True)),
                (s == last, last % 2, (False, False, True)))
    for cond, parity, stages in variants:
        pl.when(cond)(functools.partial(step, parity, *stages))

    @pl.when(s == pl.num_programs(1) - 1)
    def _():
        o_ref[...] = x2_ref[...] + acc_ref[...].T


def _experts(h2, wd, wut, e1, na, e2, r2, x2):
    t, d = x2.shape
    tn = EXPERT_TN
    ne = 2 * wd.shape[0]
    nk = PEER_N_KEYS
    assert t % tn == 0 and ne % EXPERT_CHUNK == 0 and EXPERT_CHUNK % nk == 0
    n_chunks = ne // EXPERT_CHUNK
    assert n_chunks % 2 == 0 and EXPERT_CHUNK == d
    idle = jnp.zeros((2, EXPERT_CHUNK // 2, d), wd.dtype)
    weights = jnp.stack([jnp.concatenate([wd.reshape(n_chunks, EXPERT_CHUNK // 2, d), idle]),
                         jnp.concatenate([idle, wut])], axis=1)
    tok = pl.BlockSpec((tn, d), lambda i, s: (i, 0))
    gate = pl.BlockSpec((PEER_HEADS, nk // 2, tn), lambda i, s: (0, 0, i))
    rpc = EXPERT_CHUNK // nk
    assert SUBLANES % rpc == 0
    chunks_per_group = SUBLANES // rpc
    row = pl.BlockSpec((PEER_HEADS, None, SUBLANES, tn),
                       lambda i, s: (0, jnp.clip(s - 1, 0, n_chunks - 1) // chunks_per_group, 0, i))
    e1 = e1.reshape(PEER_HEADS, nk // SUBLANES, SUBLANES, t)
    na = na.reshape(PEER_HEADS, nk // SUBLANES, SUBLANES, t)
    return pl.pallas_call(
        _experts_kernel,
        grid=(t // tn, n_chunks + 2),
        in_specs=[pl.BlockSpec((tn // 2, d), lambda i, s: (i, 0)),
                  pl.BlockSpec((None, 2, d // 2, d), lambda i, s: (s, 0, 0, 0)),
                  row, row, gate, gate, tok],
        out_specs=tok,
        out_shape=jax.ShapeDtypeStruct((t, d), F32),
        scratch_shapes=[pltpu.VMEM((d, tn), F32),
                        pltpu.VMEM((EXPERT_CHUNK, tn), F32), pltpu.VMEM((EXPERT_CHUNK, tn), F32),
                        pltpu.VMEM((EXPERT_CHUNK, tn), BF16), pltpu.VMEM((EXPERT_CHUNK, tn), BF16),
                        pltpu.VMEM((PEER_HEADS, rpc, tn), jnp.int32), pltpu.VMEM((PEER_HEADS, rpc, tn), jnp.int32),
                        pltpu.VMEM((PEER_HEADS, nk // 2, tn), jnp.int32),
                        pltpu.VMEM((PEER_HEADS, nk // 2, tn), jnp.int32)],
        compiler_params=_params(("parallel", "arbitrary")),
        name="peer_experts",
    )(h2, weights, e1, na, e2, r2, x2)


def _rope_tables(seq):
    pos = jnp.arange(seq, dtype=F32)
    inv_freq = ROPE_THETA ** (-jnp.arange(0, ROT_DIM, 2, dtype=F32) / ROT_DIM)
    ang = pos[:, None] * inv_freq[None, :]
    cos, sin = jnp.cos(ang), jnp.sin(ang)
    rest = ATTN_HEAD_DIM - ROT_DIM
    cos_h = jnp.concatenate([cos, cos, jnp.ones((seq, rest), F32)], axis=-1)
    sin_h = jnp.concatenate([-sin, sin, jnp.zeros((seq, rest), F32)], axis=-1)
    return jnp.tile(cos_h, (1, ATTN_HEADS)), jnp.tile(sin_h, (1, ATTN_HEADS))


def kernel(x, mix_norm_w, w_in, q_norm_w, k_norm_w, conv_w, conv_b, igate_b, fgate_b, mlstm_norm_w,
           w_attn_branch, w_mlstm_branch, merge_b, w_out, ffn_norm_w, peer_w_query, peer_keys_1,
           peer_keys_2, peer_w_down, peer_w_up):
    b, s, d = x.shape
    t = b * s
    x2d = x.reshape(t, d)

    n_main = 3 * ATTN_W + 4 * MLSTM_W
    n_gate = 2 * MLSTM_HEADS
    w_cat = jnp.concatenate(
        [w_in[:, :n_main], w_in[:, n_main + n_gate:], w_in[:, n_main:n_main + n_gate],
         jnp.zeros((d, LANES - n_gate), w_in.dtype)], axis=1).astype(BF16)
    cos_t, sin_t = _rope_tables(s)
    head_of = jnp.arange(ATTN_W) // ATTN_HEAD_DIM
    bd = (head_of[:, None] == head_of[None, :]).astype(BF16)
    qnw = jnp.tile(q_norm_w, ATTN_HEADS).reshape(1, ATTN_W)
    knw = jnp.tile(k_norm_w, ATTN_HEADS).reshape(1, ATTN_W)

    aq, ak, av, mq, mk, mv, mo, ga, gm, gt = _inproj(
        x2d, mix_norm_w.reshape(1, d), w_cat, cos_t, sin_t, bd, qnw, knw, s)

    attn = _moba(aq.reshape(b, s, ATTN_W), ak.reshape(b, s, ATTN_W), av.reshape(b, s, ATTN_W))

    gates = gt[:, :n_gate].reshape(b, s, 2, MLSTM_HEADS)
    g_col = gates.transpose(0, 3, 1, 2)
    g_row = gates.transpose(0, 3, 2, 1)
    gate_b = jnp.stack([igate_b, fgate_b]).astype(F32)
    r3 = lambda a: a.reshape(b, s, MLSTM_W)
    hm = _mlstm(r3(mq), r3(mk), r3(mv), r3(mo), g_col, g_row, gate_b,
                conv_w[:, :MLSTM_W], conv_w[:, MLSTM_W:], conv_b[:MLSTM_W].reshape(1, MLSTM_W),
                conv_b[MLSTM_W:].reshape(1, MLSTM_W), mlstm_norm_w.reshape(1, MLSTM_W))

    x2, h2, q = _merge(x2d, attn.reshape(t, ATTN_W), hm.reshape(t, MLSTM_W), ga, gm,
                       w_attn_branch.astype(BF16), w_mlstm_branch.astype(BF16), merge_b,
                       w_out.astype(BF16), ffn_norm_w.reshape(1, d), peer_w_query.astype(BF16))

    e1, na, e2, r2 = _select(q, peer_keys_1, peer_keys_2)
    wut = peer_w_up.astype(BF16).reshape(-1, EXPERT_CHUNK, d).transpose(0, 2, 1)
    out = _experts(h2, _pack_rows(peer_w_down.astype(BF16)), _pack_rows(wut), e1, na, e2, r2, x2)
    return out.reshape(b, s, d)
```
